```python
import math
import jax
import jax.numpy as jnp
from jax import lax
import numpy as np

D_MODEL = 1024
BATCH = 32
SEQ = 256
DEPTH = 4
DEC_BATCH = 8
DEC_SEQ = 1024
PAST_LEN = 256

GRID_W = 64
SSD_HEADS = 6
SSD_HEAD_DIM = 64
SSD_D = SSD_HEADS * SSD_HEAD_DIM
SSD_GROUPS = 2
SSD_STATE = 64
SSD_CONV = 5
SSD_CHUNK = 128
SSD_CONV_DIM = SSD_D + 2 * SSD_GROUPS * SSD_STATE
S5_GROUPS = 16
S5_GROUP_CH = 16
S5_D = S5_GROUPS * S5_GROUP_CH
S5_STATE = 64
NAT_HEADS = 6
NAT_HEAD_DIM = 64
NAT_D = NAT_HEADS * NAT_HEAD_DIM
NAT_KH = 8
NAT_KW = 16
Q_BLOCK = 128
MIX_D = SSD_D + S5_D + NAT_D
D_FF = 4 * D_MODEL
IN_COLS = SSD_D + SSD_CONV_DIM + 2 * SSD_HEADS + S5_D + 3 * NAT_D
N_MOD = 6
EPS = 1e-6

kernel_name = 'hybrid_dit_ssd_s5_natten_step'


def rmsnorm(x, g):
    xf = x.astype(jnp.float32)
    y = xf * lax.rsqrt(jnp.mean(xf * xf, axis=-1, keepdims=True) + EPS)
    return (y * g.astype(jnp.float32)).astype(x.dtype)


def split_proj(p):
    sizes = (SSD_D, SSD_CONV_DIM, 2 * SSD_HEADS, S5_D, NAT_D, NAT_D)
    return jnp.split(p, np.cumsum(sizes).tolist(), axis=-1)


def centred_depthwise_conv(x, w, b):
    ch = x.shape[-1]
    y = lax.conv_general_dilated(x, w[:, None, :].astype(x.dtype), window_strides=(1,),
                                 padding=[(SSD_CONV // 2, SSD_CONV // 2)],
                                 dimension_numbers=('NWC', 'WIO', 'NWC'), feature_group_count=ch)
    return y + b.astype(x.dtype)


def ssd_scan(x, dt, a, bm, cm, h0):
    b, L, H, P = x.shape
    G, N = bm.shape[2], bm.shape[3]
    f32 = jnp.float32
    nc, Q = L // SSD_CHUNK, SSD_CHUNK
    xc = x.astype(f32).reshape(b, nc, Q, H, P)
    dtc = dt.reshape(b, nc, Q, H)
    bc = jnp.repeat(bm.astype(f32), H // G, axis=2).reshape(b, nc, Q, H, N)
    cc = jnp.repeat(cm.astype(f32), H // G, axis=2).reshape(b, nc, Q, H, N)
    a_cum = jnp.cumsum(dtc * a, axis=2)
    lower = jnp.tril(jnp.ones((Q, Q), dtype=bool))[None, None, :, :, None]
    seg = a_cum[:, :, :, None, :] - a_cum[:, :, None, :, :]
    decay = jnp.exp(jnp.where(lower, seg, -jnp.inf))
    xdt = xc * dtc[..., None]
    scores = jnp.einsum('bcihn,bcjhn->bcijh', cc, bc) * decay
    y_intra = jnp.einsum('bcijh,bcjhp->bcihp', scores, xdt)
    decay_end = jnp.exp(a_cum[:, :, -1:, :] - a_cum)
    chunk_states = jnp.einsum('bcjhn,bcjh,bcjhp->bchpn', bc, decay_end, xdt)
    chunk_decay = jnp.exp(a_cum[:, :, -1, :])

    def step(h, inp):
        s, d = inp
        return d[:, :, None, None] * h + s, h

    h_last, h_in = lax.scan(step, h0.astype(f32),
                            (jnp.moveaxis(chunk_states, 1, 0), jnp.moveaxis(chunk_decay, 1, 0)))
    h_in = jnp.moveaxis(h_in, 0, 1)
    y_inter = jnp.einsum('bcihn,bchpn->bcihp', cc, h_in) * jnp.exp(a_cum)[..., None]
    return (y_intra + y_inter).reshape(b, L, H, P), h_last


def ssd_mixer(z, xbc, dt_raw, lw, h0_f, h0_b):
    b, L, _ = xbc.shape
    f32 = jnp.float32
    xbc = jax.nn.silu(centred_depthwise_conv(xbc, lw['ssd_conv_w'], lw['ssd_conv_b']))
    xs, bm, cm = jnp.split(xbc, [SSD_D, SSD_D + SSD_GROUPS * SSD_STATE], axis=-1)
    xs = xs.reshape(b, L, SSD_HEADS, SSD_HEAD_DIM)
    bm = bm.reshape(b, L, SSD_GROUPS, SSD_STATE)
    cm = cm.reshape(b, L, SSD_GROUPS, SSD_STATE)
    dt = jax.nn.softplus(dt_raw.astype(f32).reshape(b, L, 2, SSD_HEADS) + lw['ssd_dt_bias'].astype(f32))
    a = -jnp.exp(lw['ssd_a_log'].astype(f32))
    y_f, h_f = ssd_scan(xs, dt[:, :, 0], a[0], bm, cm, h0_f)
    y_b, h_b = ssd_scan(jnp.flip(xs, 1), jnp.flip(dt[:, :, 1], 1), a[1],
                        jnp.flip(bm, 1), jnp.flip(cm, 1), h0_b)
    y = y_f + jnp.flip(y_b, 1) + lw['ssd_d'].astype(f32)[:, None] * xs.astype(f32)
    y = y.reshape(b, L, SSD_D) * jax.nn.silu(z.astype(f32))
    return rmsnorm(y, lw['ssd_norm']), jnp.stack([h_f, h_b], axis=1)


def s5_scan(ug, lam, log_dt, bmat, h0):
    a_bar = jnp.exp(lam * jnp.exp(log_dt)[:, None])
    b_bar = ((a_bar - 1.0) / lam)[..., None] * bmat
    bu = jnp.einsum('blgc,gpc->blgp', ug, b_bar)
    bu = bu.at[:, 0].add(a_bar * h0)

    def combine(e1, e2):
        a1, b1 = e1
        a2, b2 = e2
        return a1 * a2, a2 * b1 + b2

    _, h = lax.associative_scan(combine, (jnp.broadcast_to(a_bar, bu.shape), bu), axis=1)
    return h, h[:, -1]


def s5_mixer(u, lw, h0_re, h0_im):
    b, L, _ = u.shape
    f32 = jnp.float32
    uf = u.astype(f32)
    ug = uf.reshape(b, L, S5_GROUPS, S5_GROUP_CH)
    lam = lax.complex(lw['s5_a_re'].astype(f32), lw['s5_a_im'].astype(f32))
    bmat = lax.complex(lw['s5_b_re'].astype(f32), lw['s5_b_im'].astype(f32))
    cmat = lax.complex(lw['s5_c_re'].astype(f32), lw['s5_c_im'].astype(f32))
    h0 = lax.complex(h0_re.astype(f32), h0_im.astype(f32))
    log_dt = lw['s5_log_dt'].astype(f32)
    h_f, last_f = s5_scan(ug, lam[0], log_dt[0], bmat, h0[:, 0])
    h_b, last_b = s5_scan(jnp.flip(ug, 1), lam[1], log_dt[1], bmat, h0[:, 1])
    h = h_f + jnp.flip(h_b, 1)
    y = jnp.einsum('blgp,gcp->blgc', h, cmat).real.reshape(b, L, S5_D) + lw['s5_d'].astype(f32) * uf
    g = jax.nn.gelu(y)
    out = g * jax.nn.sigmoid(g @ lw['s5_w_glu'].astype(f32) + lw['s5_b_glu'].astype(f32))
    last = jnp.stack([last_f, last_b], axis=1)
    return out, last.real, last.imag


def context_attention(q, k, v):
    b, Lc, H, Dh = q.shape
    scale = Dh ** -0.5
    qb = jnp.moveaxis(q.reshape(b, Lc // Q_BLOCK, Q_BLOCK, H, Dh), 1, 0)

    def block(qi):
        s = jnp.einsum('bqhd,bkhd->bhqk', qi, k).astype(jnp.float32) * scale
        p = jax.nn.softmax(s, axis=-1).astype(v.dtype)
        return jnp.einsum('bhqk,bkhd->bqhd', p, v)

    out = lax.map(block, qb)
    return jnp.moveaxis(out, 0, 1).reshape(b, Lc, H * Dh)


def neighbourhood_attention(q, k, v, k_ctx, v_ctx, rpb):
    b, L, H, Dh = q.shape
    rows = L // GRID_W
    kh = min(NAT_KH, rows)
    scale = Dh ** -0.5
    f32 = jnp.float32
    qg = q.reshape(b, rows, GRID_W, H, Dh)
    kg = k.reshape(b, rows, GRID_W, H, Dh)
    vg = v.reshape(b, rows, GRID_W, H, Dh)
    r_idx = jnp.arange(rows)
    r_start = jnp.clip(r_idx - kh // 2, 0, rows - kh)
    cols = jnp.arange(GRID_W)
    c_start = jnp.clip(cols - NAT_KW // 2, 0, GRID_W - NAT_KW)
    col_mask = (cols[None, :] >= c_start[:, None]) & (cols[None, :] < c_start[:, None] + NAT_KW)
    col_bias_idx = jnp.clip(cols[None, :] - cols[:, None] + NAT_KW - 1, 0, 2 * NAT_KW - 2)
    rpb = rpb.astype(f32)

    def row_block(args):
        q_r, rs, r = args
        k_b = lax.dynamic_slice_in_dim(kg, rs, kh, axis=1)
        v_b = lax.dynamic_slice_in_dim(vg, rs, kh, axis=1)
        row_bias_idx = rs + jnp.arange(kh) - r + NAT_KH - 1
        bias = jnp.take(rpb[:, row_bias_idx], col_bias_idx, axis=2)
        s_lat = jnp.einsum('bqhd,bikhd->bhqik', q_r, k_b).astype(f32) * scale
        s_lat = jnp.where(col_mask[:, None, :], s_lat + bias.transpose(0, 2, 1, 3), -jnp.inf)
        s_lat = s_lat.reshape(b, H, GRID_W, kh * GRID_W)
        s_ctx = jnp.einsum('bqhd,bmhd->bhqm', q_r, k_ctx).astype(f32) * scale
        p = jax.nn.softmax(jnp.concatenate([s_lat, s_ctx], axis=-1), axis=-1).astype(v.dtype)
        p_lat = p[..., :kh * GRID_W].reshape(b, H, GRID_W, kh, GRID_W)
        p_ctx = p[..., kh * GRID_W:]
        return (jnp.einsum('bhqik,bikhd->bqhd', p_lat, v_b)
                + jnp.einsum('bhqm,bmhd->bqhd', p_ctx, v_ctx.astype(v.dtype)))

    out = lax.map(row_block, (jnp.moveaxis(qg, 1, 0), r_start, r_idx))
    return jnp.moveaxis(out, 0, 1).reshape(b, L, H * Dh)


def trunk_layer(x, cond, lw, cache=None):
    b, L, _ = x.shape
    mods = jax.nn.silu(cond) @ lw['w_mod'] + lw['b_mod']
    sh1, sc1, g1, sh2, sc2, g2 = jnp.split(mods[:, None, :], N_MOD, axis=-1)
    h = rmsnorm(x, lw['norm_mix']) * (1 + sc1) + sh1
    z, xbc, dt_raw, u, q, k, v = split_proj(h @ lw['w_in'])
    q = q.reshape(b, L, NAT_HEADS, NAT_HEAD_DIM)
    k = k.reshape(b, L, NAT_HEADS, NAT_HEAD_DIM)
    v = v.reshape(b, L, NAT_HEADS, NAT_HEAD_DIM)
    if cache is None:
        ssd0 = jnp.zeros((b, 2, SSD_HEADS, SSD_HEAD_DIM, SSD_STATE), jnp.float32)
        s5_re0 = jnp.zeros((b, 2, S5_GROUPS, S5_STATE), jnp.float32)
        s5_im0 = s5_re0
        y_nat = context_attention(q, k, v)
    else:
        k_ctx, v_ctx, ssd0, s5_re0, s5_im0 = cache
        y_nat = neighbourhood_attention(q, k, v, k_ctx, v_ctx, lw['nat_rpb'])
    y_ssd, ssd_state = ssd_mixer(z, xbc, dt_raw, lw, ssd0[:, 0], ssd0[:, 1])
    y_s5, s5_re, s5_im = s5_mixer(u, lw, s5_re0, s5_im0)
    mix = jnp.concatenate([y_ssd.astype(x.dtype), y_s5.astype(x.dtype), y_nat.astype(x.dtype)], axis=-1)
    x = x + g1 * (mix @ lw['w_out'])
    h2 = rmsnorm(x, lw['norm_mlp']) * (1 + sc2) + sh2
    x = x + g2 * (jnp.square(jax.nn.relu(h2 @ lw['w_ff1'])) @ lw['w_ff2'])
    return x, (k, v, ssd_state, s5_re, s5_im)


def setup_inputs(seed: int = 0) -> dict:
    key = jax.random.key(seed)
    ks = jax.random.split(key, 40)
    f32 = jnp.float32

    def nrm(k, shape, scale):
        return jax.random.normal(k, shape, f32) * scale

    log_lo, log_hi = math.log(1e-3), math.log(1e-1)
    ssd_dt = jnp.exp(jax.random.uniform(ks[10], (DEPTH, 2, SSD_HEADS), f32, log_lo, log_hi))
    s5_shape = (DEPTH, 2, S5_GROUPS, S5_STATE)
    return {
        'x_prompt': nrm(ks[0], (BATCH, SEQ, D_MODEL), 1.0),
        'x_sample': nrm(ks[1], (DEC_BATCH, DEC_SEQ, D_MODEL), 1.0),
        'cache_nat_k': nrm(ks[2], (DEC_BATCH, DEPTH, PAST_LEN, NAT_HEADS, NAT_HEAD_DIM), 1.0),
        'cache_nat_v': nrm(ks[3], (DEC_BATCH, DEPTH, PAST_LEN, NAT_HEADS, NAT_HEAD_DIM), 1.0),
        'state_ssd': nrm(ks[4], (DEC_BATCH, DEPTH, 2, SSD_HEADS, SSD_HEAD_DIM, SSD_STATE), 0.1),
        'state_s5_re': nrm(ks[5], (DEC_BATCH, DEPTH, 2, S5_GROUPS, S5_STATE), 0.05),
        'state_s5_im': nrm(ks[6], (DEC_BATCH, DEPTH, 2, S5_GROUPS, S5_STATE), 0.05),
        'c': nrm(ks[7], (DEC_BATCH, D_MODEL), 1.0),
        'c_ctx': nrm(ks[8], (D_MODEL,), 1.0),
        'w_mod': nrm(ks[9], (DEPTH, D_MODEL, N_MOD * D_MODEL), 0.5 * D_MODEL ** -0.5),
        'b_mod': nrm(ks[11], (DEPTH, N_MOD * D_MODEL), 0.02),
        'norm_mix': 1.0 + nrm(ks[12], (DEPTH, D_MODEL), 0.01),
        'norm_mlp': 1.0 + nrm(ks[13], (DEPTH, D_MODEL), 0.01),
        'w_in': nrm(ks[14], (DEPTH, D_MODEL, IN_COLS), D_MODEL ** -0.5),
        'ssd_conv_w': nrm(ks[15], (DEPTH, SSD_CONV, SSD_CONV_DIM), SSD_CONV ** -0.5),
        'ssd_conv_b': nrm(ks[16], (DEPTH, SSD_CONV_DIM), 0.01),
        'ssd_dt_bias': ssd_dt + jnp.log(-jnp.expm1(-ssd_dt)),
        'ssd_a_log': jnp.log(jax.random.uniform(ks[17], (DEPTH, 2, SSD_HEADS), f32, 1.0, 16.0)),
        'ssd_d': 1.0 + nrm(ks[18], (DEPTH, SSD_HEADS), 0.1),
        'ssd_norm': 1.0 + nrm(ks[19], (DEPTH, SSD_D), 0.01),
        's5_a_re': -0.5 + nrm(ks[20], s5_shape, 0.01),
        's5_a_im': jnp.pi * jnp.arange(S5_STATE, dtype=f32) + nrm(ks[21], s5_shape, 0.01),
        's5_log_dt': jax.random.uniform(ks[22], (DEPTH, 2, S5_GROUPS), f32, log_lo, log_hi),
        's5_b_re': nrm(ks[23], (DEPTH, S5_GROUPS, S5_STATE, S5_GROUP_CH), (2 * S5_GROUP_CH) ** -0.5),
        's5_b_im': nrm(ks[24], (DEPTH, S5_GROUPS, S5_STATE, S5_GROUP_CH), (2 * S5_GROUP_CH) ** -0.5),
        's5_c_re': nrm(ks[25], (DEPTH, S5_GROUPS, S5_GROUP_CH, S5_STATE), 0.5),
        's5_c_im': nrm(ks[26], (DEPTH, S5_GROUPS, S5_GROUP_CH, S5_STATE), 0.5),
        's5_d': nrm(ks[27], (DEPTH, S5_D), 1.0),
        's5_w_glu': nrm(ks[28], (DEPTH, S5_D, S5_D), S5_D ** -0.5),
        's5_b_glu': nrm(ks[29], (DEPTH, S5_D), 0.01),
        'nat_rpb': nrm(ks[30], (DEPTH, NAT_HEADS, 2 * NAT_KH - 1, 2 * NAT_KW - 1), 0.1),
        'w_out': nrm(ks[31], (DEPTH, MIX_D, D_MODEL), MIX_D ** -0.5),
        'w_ff1': nrm(ks[32], (DEPTH, D_MODEL, D_FF), D_MODEL ** -0.5),
        'w_ff2': nrm(ks[33], (DEPTH, D_FF, D_MODEL), D_FF ** -0.5),
        'norm_f': 1.0 + nrm(ks[34], (D_MODEL,), 0.01),
    }


def reference(x_prompt, x_sample, cache_nat_k, cache_nat_v, state_ssd, state_s5_re, state_s5_im,
              c, c_ctx, w_mod, b_mod, norm_mix, norm_mlp, w_in, ssd_conv_w, ssd_conv_b,
              ssd_dt_bias, ssd_a_log, ssd_d, ssd_norm, s5_a_re, s5_a_im, s5_log_dt,
              s5_b_re, s5_b_im, s5_c_re, s5_c_im, s5_d, s5_w_glu, s5_b_glu, nat_rpb,
              w_out, w_ff1, w_ff2, norm_f):
    xp = x_prompt
    xs = x_sample
    new_k, new_v, new_ssd, new_re, new_im = [], [], [], [], []
    for l in range(DEPTH):
        lw = {
            'w_mod': w_mod[l], 'b_mod': b_mod[l], 'norm_mix': norm_mix[l], 'norm_mlp': norm_mlp[l],
            'w_in': w_in[l], 'ssd_conv_w': ssd_conv_w[l], 'ssd_conv_b': ssd_conv_b[l],
            'ssd_dt_bias': ssd_dt_bias[l], 'ssd_a_log': ssd_a_log[l], 'ssd_d': ssd_d[l],
            'ssd_norm': ssd_norm[l], 's5_a_re': s5_a_re[l], 's5_a_im': s5_a_im[l],
            's5_log_dt': s5_log_dt[l], 's5_b_re': s5_b_re[l], 's5_b_im': s5_b_im[l],
            's5_c_re': s5_c_re[l], 's5_c_im': s5_c_im[l], 's5_d': s5_d[l],
            's5_w_glu': s5_w_glu[l], 's5_b_glu': s5_b_glu[l], 'nat_rpb': nat_rpb[l],
            'w_out': w_out[l], 'w_ff1': w_ff1[l], 'w_ff2': w_ff2[l],
        }
        xp, (k_l, v_l, ssd_l, re_l, im_l) = trunk_layer(xp, c_ctx[None, :], lw)
        new_k.append(k_l)
        new_v.append(v_l)
        new_ssd.append(ssd_l)
        new_re.append(re_l)
        new_im.append(im_l)
        xs, _ = trunk_layer(xs, c, lw, (cache_nat_k[:, l], cache_nat_v[:, l], state_ssd[:, l],
                                        state_s5_re[:, l], state_s5_im[:, l]))
    y_prompt = rmsnorm(xp, norm_f)
    y_sample = rmsnorm(xs, norm_f)
    return (y_prompt, y_sample, jnp.stack(new_k, axis=1), jnp.stack(new_v, axis=1),
            jnp.stack(new_ssd, axis=1), jnp.stack(new_re, axis=1), jnp.stack(new_im, axis=1))
```

```python
import functools
import math

import numpy as np
import jax
import jax.numpy as jnp
from jax import lax
from jax.experimental import pallas as pl
from jax.experimental.pallas import tpu as pltpu

F32 = jnp.float32
BF16 = jnp.bfloat16
HIGHEST = lax.Precision.HIGHEST

D_MODEL = 1024
DEPTH = 4
GRID_W = 64
SSD_HEADS = 6
SSD_HEAD_DIM = 64
SSD_D = SSD_HEADS * SSD_HEAD_DIM
SSD_GROUPS = 2
SSD_STATE = 64
SSD_CONV = 5
SSD_CHUNK = 128
SSD_CONV_DIM = SSD_D + 2 * SSD_GROUPS * SSD_STATE
S5_GROUPS = 16
S5_GROUP_CH = 16
S5_D = S5_GROUPS * S5_GROUP_CH
S5_STATE = 64
NAT_HEADS = 6
NAT_HEAD_DIM = 64
NAT_D = NAT_HEADS * NAT_HEAD_DIM
NAT_KH = 8
NAT_KW = 16
D_FF = 4 * D_MODEL
N_MOD = 6
EPS = 1e-6

LANES = 128
TOK_TILE = 256
IN_PACKED = 2560
S5_T = 32
S5_W = S5_T * S5_GROUP_CH
S5_SW = 4 * LANES
VMEM_LIMIT = 56 * 1024 * 1024


def _cparams(*sem):
    return pltpu.CompilerParams(dimension_semantics=sem, vmem_limit_bytes=VMEM_LIMIT)


def _dot(a, b, precision=None):
    return jnp.dot(a, b, preferred_element_type=F32, precision=precision)


def _dot_nt(a, b, precision=None):
    return lax.dot_general(a, b, (((1,), (1,)), ((), ())), preferred_element_type=F32,
                           precision=precision)


def _dot_tn(a, b, precision=None):
    return lax.dot_general(a, b, (((0,), (0,)), ((), ())), preferred_element_type=F32,
                           precision=precision)


def _silu(x):
    return x * jax.nn.sigmoid(x)


def _rms(x, g):
    return x * lax.rsqrt(jnp.mean(x * x, axis=-1, keepdims=True) + EPS) * g


def _mods_body(cond_ref, w_ref, b_ref, o_ref):
    s = _silu(cond_ref[...])
    o_ref[...] = _dot(s.astype(BF16), w_ref[...].astype(BF16)) + b_ref[...]


def _mods_call(cond, w_mod, b_mod):
    ncp = cond.shape[0]
    blk = D_MODEL
    return pl.pallas_call(
        _mods_body,
        out_shape=jax.ShapeDtypeStruct((DEPTH, ncp, N_MOD * D_MODEL), F32),
        grid=(DEPTH, N_MOD),
        in_specs=[
            pl.BlockSpec((ncp, D_MODEL), lambda l, j: (0, 0)),
            pl.BlockSpec((None, D_MODEL, blk), lambda l, j: (l, 0, j)),
            pl.BlockSpec((None, 1, blk), lambda l, j: (l, 0, j)),
        ],
        out_specs=pl.BlockSpec((None, ncp, blk), lambda l, j: (l, 0, j)),
        compiler_params=_cparams("arbitrary", "arbitrary"),
    )(cond, w_mod, b_mod.reshape(DEPTH, 1, N_MOD * D_MODEL))


def _inproj_body(x_ref, mod_ref, g_ref, w_ref, zx_ref, dt_ref, u_ref, q_ref, k_ref, v_ref):
    x = x_ref[...]
    h = _rms(x, g_ref[...]) * (1.0 + mod_ref[:, D_MODEL:2 * D_MODEL]) + mod_ref[:, 0:D_MODEL]
    p = _dot(h.astype(BF16), w_ref[...])
    zx_ref[...] = p[:, 0:1024]
    u_ref[...] = p[:, 1024:1280]
    q_ref[...] = p[:, 1280:1664]
    k_ref[...] = p[:, 1664:2048]
    v_ref[...] = p[:, 2048:2432]
    dt_ref[...] = p[:, 2432:2560]


def _inproj_call(x, mods4, norm_mix, w_in_p, layer, bp):
    rows = x.shape[0]
    nt = rows // TOK_TILE

    def cond_idx(i):
        return jnp.where(i < bp, 0, 1 + (i - bp) // 4)

    def tile(w):
        return pl.BlockSpec((TOK_TILE, w), lambda i: (i, 0))

    widths = (1024, LANES, S5_D, NAT_D, NAT_D, NAT_D)
    return pl.pallas_call(
        _inproj_body,
        out_shape=[jax.ShapeDtypeStruct((rows, w), F32) for w in widths],
        grid=(nt,),
        in_specs=[
            tile(D_MODEL),
            pl.BlockSpec((None, None, 1, 2 * D_MODEL), lambda i: (layer, cond_idx(i), 0, 0)),
            pl.BlockSpec((None, 1, D_MODEL), lambda i: (layer, 0, 0)),
            pl.BlockSpec((None, D_MODEL, IN_PACKED), lambda i: (layer, 0, 0),
                         pipeline_mode=pl.Buffered(1)),
        ],
        out_specs=[tile(w) for w in widths],
        compiler_params=_cparams("parallel"),
    )(x, mods4, norm_mix.reshape(DEPTH, 1, D_MODEL), w_in_p)


NAT_SCALE = NAT_HEAD_DIM ** -0.5


def _softmax_pv(scores, values):
    m = functools.reduce(jnp.maximum, [jnp.max(s, axis=-1, keepdims=True) for s in scores])
    ps = [jnp.exp(s - m) for s in scores]
    den = functools.reduce(lambda a, b: a + b, [jnp.sum(p, axis=-1, keepdims=True) for p in ps])
    num = functools.reduce(lambda a, b: a + b,
                           [_dot(p.astype(BF16), v) for p, v in zip(ps, values)])
    return num / den


def _ctx_attn_body(q_ref, k_ref, v_ref, o_ref):
    for h in range(NAT_HEADS):
        hs = slice(h * NAT_HEAD_DIM, (h + 1) * NAT_HEAD_DIM)
        q = (q_ref[:, hs] * NAT_SCALE).astype(BF16)
        k = k_ref[:, hs].astype(BF16)
        v = v_ref[:, hs].astype(BF16)
        o_ref[:, hs] = _softmax_pv([_dot_nt(q, k)], [v])


def _ctx_attn_call(q, k, v, bp):
    rows = q.shape[0]
    spec = pl.BlockSpec((TOK_TILE, NAT_D), lambda b: (b, 0))
    return pl.pallas_call(
        _ctx_attn_body,
        out_shape=jax.ShapeDtypeStruct((rows, NAT_D), F32),
        grid=(bp,),
        in_specs=[spec, spec, spec],
        out_specs=spec,
        compiler_params=_cparams("parallel"),
    )(q, k, v)


def _nat_bias_table(rpb):
    cols = np.arange(GRID_W)
    c_start = np.clip(cols - NAT_KW // 2, 0, GRID_W - NAT_KW)
    col_mask = (cols[None, :] >= c_start[:, None]) & (cols[None, :] < c_start[:, None] + NAT_KW)
    col_idx = np.clip(cols[None, :] - cols[:, None] + NAT_KW - 1, 0, 2 * NAT_KW - 2)
    o = np.arange(NAT_KH)
    i = np.arange(NAT_KH)
    row_idx = i[None, :] - o[:, None] + NAT_KH - 1
    t = rpb.astype(F32)[:, :, row_idx]
    t = t[..., col_idx]
    t = jnp.where(col_mask[None, None, None, None], t, -jnp.inf)
    t = jnp.transpose(t, (0, 1, 2, 4, 3, 5))
    return t.reshape(DEPTH, NAT_HEADS, NAT_KH, GRID_W, NAT_KH * GRID_W)


def _nat_attn_body(q_ref, k_ref, v_ref, kc_ref, vc_ref, bias_ref, yin_ref, o_ref, *, rows):
    del yin_ref
    win = NAT_KH * GRID_W
    for h in range(NAT_HEADS):
        hs = slice(h * NAT_HEAD_DIM, (h + 1) * NAT_HEAD_DIM)
        kc = kc_ref[:, hs].astype(BF16)
        vc = vc_ref[:, hs].astype(BF16)

        def row(r, carry):
            rs = jnp.clip(r - NAT_KH // 2, 0, rows - NAT_KH)
            q0 = pl.multiple_of(r * GRID_W, GRID_W)
            k0 = pl.multiple_of(rs * GRID_W, GRID_W)
            q = (q_ref[pl.ds(q0, GRID_W), hs] * NAT_SCALE).astype(BF16)
            kw = k_ref[pl.ds(k0, win), hs].astype(BF16)
            vw = v_ref[pl.ds(k0, win), hs].astype(BF16)
            s_lat = _dot_nt(q, kw) + bias_ref[h, r - rs]
            s_ctx = _dot_nt(q, kc)
            o_ref[pl.ds(q0, GRID_W), hs] = _softmax_pv([s_lat, s_ctx], [vw, vc])
            return carry

        lax.fori_loop(0, rows, row, 0)


def _nat_attn_call(q, k, v, cache_k, cache_v, bias, y_nat, layer, bp, bs, seq):
    rows = seq // GRID_W
    assert rows >= NAT_KH and (bp * TOK_TILE) % seq == 0
    off = bp * TOK_TILE // seq
    spec = pl.BlockSpec((seq, NAT_D), lambda b: (off + b, 0))
    cspec = pl.BlockSpec((None, None, cache_k.shape[2], NAT_D), lambda b: (b, layer, 0, 0))
    return pl.pallas_call(
        functools.partial(_nat_attn_body, rows=rows),
        out_shape=jax.ShapeDtypeStruct(y_nat.shape, F32),
        grid=(bs,),
        in_specs=[spec, spec, spec, cspec, cspec,
                  pl.BlockSpec((None, NAT_HEADS, NAT_KH, GRID_W, NAT_KH * GRID_W),
                               lambda b: (layer, 0, 0, 0, 0)),
                  pl.BlockSpec(memory_space=pl.ANY)],
        out_specs=spec,
        input_output_aliases={6: 0},
        compiler_params=_cparams("parallel"),
    )(q, k, v, cache_k, cache_v, bias, y_nat)


def _ssd_body(*refs, seq, has_h0):
    zx_ref, dt_ref, cw_ref, cb_ref, dtb_ref, a_ref, d_ref, nw_ref = refs[:8]
    refs = refs[8:]
    if has_h0:
        h0_ref, yin_ref, y_ref = refs[:3]
        refs = refs[3:]
        hout_ref = None
    else:
        y_ref, hout_ref = refs[:2]
        refs = refs[2:]
        h0_ref = None
    xbc_s, y_s, ac_s, dtv_s, st_s, h_s = refs

    q = SSD_CHUNK
    nc = seq // q
    nh = SSD_HEADS
    hd = SSD_HEAD_DIM
    b_off = SSD_D
    c_off = SSD_D + SSD_GROUPS * SSD_STATE

    row = lax.broadcasted_iota(jnp.int32, (seq, 1), 0)
    for cblk in range(SSD_CONV_DIM // LANES):
        cs = slice(cblk * LANES, (cblk + 1) * LANES)
        xin = zx_ref[:, SSD_D + cblk * LANES:SSD_D + (cblk + 1) * LANES]
        acc = cb_ref[:, cs] + cw_ref[SSD_CONV // 2:SSD_CONV // 2 + 1, cs] * xin
        for kk in range(SSD_CONV):
            d = kk - SSD_CONV // 2
            if d == 0:
                continue
            sh = pltpu.roll(xin, (-d) % seq, axis=0)
            ok = (row + d >= 0) & (row + d < seq)
            acc = acc + cw_ref[kk:kk + 1, cs] * jnp.where(ok, sh, 0.0)
        xbc_s[:, cs] = _silu(acc)

    lane = lax.broadcasted_iota(jnp.int32, (1, LANES), 1)
    xdt = dt_ref[...] + dtb_ref[...]
    dtv = jnp.maximum(xdt, 0.0) + jnp.log1p(jnp.exp(-jnp.abs(xdt)))
    dtv_s[...] = jnp.where(lane < 2 * nh, dtv, 0.0)

    ii = lax.broadcasted_iota(jnp.int32, (q, q), 0)
    jj = lax.broadcasted_iota(jnp.int32, (q, q), 1)
    lower = jj <= ii
    upper = jj >= ii
    is_fwd = lane < nh

    def chunk(c, carry):
        r0 = pl.multiple_of(c * q, q)
        rs = pl.ds(r0, q)
        dt_c = dtv_s[rs, :]
        da_c = dt_c * a_ref[...]
        ac = jnp.where(is_fwd,
                       _dot(lower.astype(F32), da_c, HIGHEST),
                       _dot(upper.astype(F32), da_c, HIGHEST))
        ac_s[rs, :] = ac
        ac_t = ac.T
        dt_t = dt_c.T
        last = jnp.where(is_fwd, ac[q - 1:q, :], ac[0:1, :])
        wend = dt_c * jnp.exp(last - ac)
        for g in range(SSD_GROUPS):
            bg = xbc_s[rs, b_off + g * SSD_STATE:b_off + (g + 1) * SSD_STATE].astype(BF16)
            cg = xbc_s[rs, c_off + g * SSD_STATE:c_off + (g + 1) * SSD_STATE].astype(BF16)
            cb = _dot_nt(cg, bg)
            for hh in range(nh // SSD_GROUPS):
                h = g * (nh // SSD_GROUPS) + hh
                hs = slice(h * hd, (h + 1) * hd)
                seg_f = ac[:, h:h + 1] - ac_t[h:h + 1, :]
                seg_b = ac[:, nh + h:nh + h + 1] - ac_t[nh + h:nh + h + 1, :]
                w = cb * (jnp.exp(jnp.where(lower, seg_f, -jnp.inf)) * dt_t[h:h + 1, :]
                          + jnp.exp(jnp.where(upper, seg_b, -jnp.inf)) * dt_t[nh + h:nh + h + 1, :])
                xh = xbc_s[rs, hs]
                y_s[rs, hs] = _dot(w.astype(BF16), xh.astype(BF16))
                st_s[c * 2 * nh + h] = _dot_tn((xh * wend[:, h:h + 1]).astype(BF16), bg)
                st_s[c * 2 * nh + nh + h] = _dot_tn(
                    (xh * wend[:, nh + h:nh + h + 1]).astype(BF16), bg)
        return carry

    lax.fori_loop(0, nc, chunk, 0)

    if has_h0:
        h_s[...] = h0_ref[...]
    else:
        h_s[...] = jnp.zeros(h_s.shape, F32)

    def carry_states(kstep, carry):
        for direction in range(2):
            c = kstep if direction == 0 else nc - 1 - kstep
            r0 = pl.multiple_of(c * q, q)
            rs = pl.ds(r0, q)
            ac = ac_s[rs, :]
            edge = ac[q - 1:q, :] if direction == 0 else ac[0:1, :]
            for g in range(SSD_GROUPS):
                cg = xbc_s[rs, c_off + g * SSD_STATE:c_off + (g + 1) * SSD_STATE].astype(BF16)
                for hh in range(nh // SSD_GROUPS):
                    h = g * (nh // SSD_GROUPS) + hh
                    hl = direction * nh + h
                    hs = slice(h * hd, (h + 1) * hd)
                    h_in = h_s[hl]
                    y_s[rs, hs] += _dot_nt(cg, h_in.astype(BF16)) * jnp.exp(ac[:, hl:hl + 1])
                    h_s[hl] = jnp.exp(edge[:, hl:hl + 1]) * h_in + st_s[c * 2 * nh + hl]
        return carry

    lax.fori_loop(0, nc, carry_states, 0)

    y = y_s[...] + d_ref[...] * xbc_s[:, 0:SSD_D]
    y = y * _silu(zx_ref[:, 0:SSD_D])
    y_ref[...] = _rms(y, nw_ref[...])
    if hout_ref is not None:
        hout_ref[...] = h_s[...]


def _ssd_call(zx, dt, lw, layer, y_ssd, state, bp, nseq, seq):
    has_h0 = state is not None
    rows = zx.shape[0]
    off = 0 if not has_h0 else bp * TOK_TILE // seq
    nc = seq // SSD_CHUNK
    nst = 2 * SSD_HEADS

    def seqspec(w):
        return pl.BlockSpec((seq, w), lambda b: (off + b, 0))

    def par(shape):
        return pl.BlockSpec((None,) + shape, lambda b: (layer,) + (0,) * len(shape))

    in_specs = [seqspec(1024), seqspec(LANES), par((8, SSD_CONV_DIM)), par((1, SSD_CONV_DIM)),
                par((1, LANES)), par((1, LANES)), par((1, SSD_D)), par((1, SSD_D))]
    args = [zx, dt, lw['conv_w'], lw['conv_b'], lw['dt_bias'], lw['a'], lw['d'], lw['norm']]
    st_spec = pl.BlockSpec((None, None, nst, SSD_HEAD_DIM, SSD_STATE), lambda b: (b, layer, 0, 0, 0))
    y_shape = jax.ShapeDtypeStruct((rows, SSD_D), F32)
    if has_h0:
        in_specs += [st_spec, pl.BlockSpec(memory_space=pl.ANY)]
        args += [state, y_ssd]
        out_shape = y_shape
        out_specs = seqspec(SSD_D)
        aliases = {9: 0}
    else:
        out_shape = [y_shape, jax.ShapeDtypeStruct((nseq, nst, SSD_HEAD_DIM, SSD_STATE), F32)]
        out_specs = [seqspec(SSD_D),
                     pl.BlockSpec((None, nst, SSD_HEAD_DIM, SSD_STATE), lambda b: (b, 0, 0, 0))]
        aliases = {}
    return pl.pallas_call(
        functools.partial(_ssd_body, seq=seq, has_h0=has_h0),
        out_shape=out_shape,
        grid=(nseq,),
        in_specs=in_specs,
        out_specs=out_specs,
        scratch_shapes=[
            pltpu.VMEM((seq, SSD_CONV_DIM), F32),
            pltpu.VMEM((seq, SSD_D), F32),
            pltpu.VMEM((seq, LANES), F32),
            pltpu.VMEM((seq, LANES), F32),
            pltpu.VMEM((nc * nst, SSD_HEAD_DIM, SSD_STATE), F32),
            pltpu.VMEM((nst, SSD_HEAD_DIM, SSD_STATE), F32),
        ],
        input_output_aliases=aliases,
        compiler_params=_cparams("parallel"),
    )(*args)


def _s5_tables(a_re, a_im, log_dt, b_re, b_im, c_re, c_im):
    t = S5_T
    lam = lax.complex(a_re.astype(F32), a_im.astype(F32))
    step = jnp.exp(log_dt.astype(F32))[..., None]
    ldt = lam * step
    a_bar = jnp.exp(ldt)
    bmat = lax.complex(b_re.astype(F32), b_im.astype(F32))
    cmat = lax.complex(c_re.astype(F32), c_im.astype(F32))
    b_bar = ((a_bar - 1.0) / lam)[..., None] * bmat[:, None]
    tau = jnp.arange(t + 1, dtype=F32)
    pw = jnp.exp(ldt[:, :, None] * tau[None, None, :, None, None])
    kern = jnp.einsum('dgcn,dztgn,dzgnk->dztgck', cmat, pw[:, :, :t], b_bar,
                      precision=HIGHEST).real
    kf, kb = kern[:, 0], kern[:, 1]
    kcat = jnp.concatenate([kb[:, 1:][:, ::-1], (kf[:, 0] + kb[:, 0])[:, None], kf[:, 1:]], axis=1)
    s_idx = np.arange(t)
    delta = s_idx[None, :] - s_idx[:, None] + t - 1
    toep = kcat[:, delta]
    toep = jnp.transpose(toep, (0, 3, 1, 5, 2, 4)).reshape(DEPTH, S5_GROUPS, S5_W, S5_W)

    def pad_state(x):
        return jnp.pad(x, [(0, 0)] * (x.ndim - 1) + [(0, LANES - S5_STATE)])

    pf = pw[:, 0, :t][:, ::-1]
    pb = pw[:, 1, :t]
    sf = jnp.einsum('dsgn,dgnk->dgskn', pf, b_bar[:, 0])
    sb = jnp.einsum('dsgn,dgnk->dgskn', pb, b_bar[:, 1])
    sop = jnp.concatenate([pad_state(sf.real), pad_state(sf.imag),
                           pad_state(sb.real), pad_state(sb.imag)], axis=-1)
    sop = sop.reshape(DEPTH, S5_GROUPS, S5_W, S5_SW)
    qf = pw[:, 0, 1:]
    qb = pw[:, 1, 1:][:, ::-1]
    af = jnp.einsum('dgcn,dtgn->dgntc', cmat, qf)
    ab = jnp.einsum('dgcn,dtgn->dgntc', cmat, qb)

    def pad_rows(x):
        x = x.reshape(DEPTH, S5_GROUPS, S5_STATE, S5_W)
        return jnp.pad(x, [(0, 0), (0, 0), (0, LANES - S5_STATE), (0, 0)])

    aop = jnp.concatenate([pad_rows(af.real), pad_rows(-af.imag),
                           pad_rows(ab.real), pad_rows(-ab.imag)], axis=2)
    at = pw[:, :, t]
    apow = jnp.concatenate([pad_state(at[:, 0].real), pad_state(at[:, 0].imag),
                            pad_state(at[:, 1].real), pad_state(at[:, 1].imag)], axis=-1)
    return toep.astype(BF16), sop, aop, apow.reshape(DEPTH, S5_GROUPS, 1, S5_SW)


def _s5_body(*refs, nc, nb, has_h0):
    u_ref, toep_ref, sop_ref, aop_ref, at_ref = refs[:5]
    refs = refs[5:]
    if has_h0:
        h0_ref, y_ref, hin_s = refs
        hout_ref = None
    else:
        y_ref, hout_ref, hin_s = refs
        h0_ref = None
    u = u_ref[...]
    y = _dot(u.astype(BF16), toep_ref[...])
    st = _dot(u, sop_ref[...], HIGHEST)
    at = at_ref[...]
    if has_h0:
        h0 = h0_ref[...]
    else:
        h0 = jnp.zeros((nb, S5_SW), F32)

    def sl(x, k):
        return x[:, k * LANES:(k + 1) * LANES]

    finals = []
    for direction in range(2):
        ar, ai = sl(at, 2 * direction), sl(at, 2 * direction + 1)
        hr, hi = sl(h0, 2 * direction), sl(h0, 2 * direction + 1)
        order = range(nc) if direction == 0 else range(nc - 1, -1, -1)
        for c in order:
            rows = slice(c * nb, (c + 1) * nb)
            hin_s[rows, 2 * direction * LANES:(2 * direction + 1) * LANES] = hr
            hin_s[rows, (2 * direction + 1) * LANES:(2 * direction + 2) * LANES] = hi
            sr = st[rows, 2 * direction * LANES:(2 * direction + 1) * LANES]
            si = st[rows, (2 * direction + 1) * LANES:(2 * direction + 2) * LANES]
            hr, hi = ar * hr - ai * hi + sr, ar * hi + ai * hr + si
        finals += [hr, hi]
    y_ref[...] = y + _dot(hin_s[...], aop_ref[...], HIGHEST)
    if hout_ref is not None:
        hout_ref[...] = jnp.concatenate(finals, axis=-1)


def _s5_call(ug, tables, layer, h0, nb, seq):
    toep, sop, aop, apow = tables
    nc = seq // S5_T
    nr = nc * nb
    has_h0 = h0 is not None

    def par(a):
        return pl.BlockSpec((None, None) + a.shape[2:], lambda g: (layer, g, 0, 0))

    gspec = pl.BlockSpec((None, nr, S5_W), lambda g: (g, 0, 0))
    hspec = pl.BlockSpec((None, nb, S5_SW), lambda g: (g, 0, 0))
    in_specs = [gspec, par(toep), par(sop), par(aop), par(apow)]
    args = [ug, toep, sop, aop, apow]
    y_shape = jax.ShapeDtypeStruct((S5_GROUPS, nr, S5_W), F32)
    if has_h0:
        in_specs.append(hspec)
        args.append(h0)
        out_shape, out_specs = y_shape, gspec
    else:
        out_shape = [y_shape, jax.ShapeDtypeStruct((S5_GROUPS, nb, S5_SW), F32)]
        out_specs = [gspec, hspec]
    return pl.pallas_call(
        functools.partial(_s5_body, nc=nc, nb=nb, has_h0=has_h0),
        out_shape=out_shape,
        grid=(S5_GROUPS,),
        in_specs=in_specs,
        out_specs=out_specs,
        scratch_shapes=[pltpu.VMEM((nr, S5_SW), F32)],
        compiler_params=_cparams("parallel"),
    )(*args)


def _s5_to_groups(u, nb, seq):
    nc = seq // S5_T
    x = u.reshape(nb, nc, S5_T, S5_GROUPS, S5_GROUP_CH)
    return jnp.transpose(x, (3, 1, 0, 2, 4)).reshape(S5_GROUPS, nc * nb, S5_W)


def _s5_from_groups(y, nb, seq):
    nc = seq // S5_T
    x = y.reshape(S5_GROUPS, nc, nb, S5_T, S5_GROUP_CH)
    return jnp.transpose(x, (2, 1, 3, 0, 4)).reshape(nb * seq, S5_D)


def _gelu_tanh(x):
    return 0.5 * x * (1.0 + jnp.tanh(math.sqrt(2.0 / math.pi) * (x + 0.044715 * (x * x * x))))


def _out_body(*refs, final):
    (x_ref, yssd_ref, y5_ref, u_ref, ynat_ref, g1_ref, m2_ref, d5_ref, wglu_ref, bglu_ref,
     wout_ref, nm_ref, w1_ref, w2_ref) = refs[:14]
    if final:
        nf_ref, o_ref = refs[14:]
    else:
        (o_ref,) = refs[14:]
    g = _gelu_tanh(y5_ref[...] + d5_ref[...] * u_ref[...])
    y5 = g * jax.nn.sigmoid(_dot(g.astype(BF16), wglu_ref[...]) + bglu_ref[...])
    mix = (_dot(yssd_ref[...].astype(BF16), wout_ref[0:SSD_D, :])
           + _dot(y5.astype(BF16), wout_ref[SSD_D:SSD_D + S5_D, :])
           + _dot(ynat_ref[...].astype(BF16), wout_ref[SSD_D + S5_D:, :]))
    x = x_ref[...] + g1_ref[...] * mix
    h2 = _rms(x, nm_ref[...]) * (1.0 + m2_ref[:, D_MODEL:2 * D_MODEL]) + m2_ref[:, 0:D_MODEL]
    f = jnp.maximum(_dot(h2.astype(BF16), w1_ref[...]), 0.0)
    f = (f * f).astype(BF16)
    x = x + m2_ref[:, 2 * D_MODEL:3 * D_MODEL] * _dot(f, w2_ref[...])
    if final:
        x = _rms(x, nf_ref[...])
    o_ref[...] = x


def _out_call(x, y_ssd, y5, u, y_nat, mods4, lw, norm_f, layer, bp, final):
    rows = x.shape[0]
    nt = rows // TOK_TILE

    def cond_idx(i):
        return jnp.where(i < bp, 0, 1 + (i - bp) // 4)

    def tile(w):
        return pl.BlockSpec((TOK_TILE, w), lambda i: (i, 0))

    def par(shape):
        return pl.BlockSpec((None,) + shape, lambda i: (layer,) + (0,) * len(shape),
                            pipeline_mode=pl.Buffered(1))

    in_specs = [
        tile(D_MODEL), tile(SSD_D), tile(S5_D), tile(S5_D), tile(NAT_D),
        pl.BlockSpec((None, None, 1, D_MODEL), lambda i: (layer, cond_idx(i), 0, 2)),
        pl.BlockSpec((None, None, 1, 3 * D_MODEL), lambda i: (layer, cond_idx(i), 0, 1)),
        par((1, S5_D)), par((S5_D, S5_D)), par((1, S5_D)),
        par((D_MODEL, D_MODEL)), par((1, D_MODEL)),
        par((D_MODEL, D_FF)), par((D_FF, D_MODEL)),
    ]
    args = [x, y_ssd, y5, u, y_nat, mods4, mods4, lw['s5_d'], lw['w_glu'], lw['b_glu'],
            lw['w_out'], lw['norm_mlp'], lw['w_ff1'], lw['w_ff2']]
    if final:
        in_specs.append(pl.BlockSpec((1, D_MODEL), lambda i: (0, 0)))
        args.append(norm_f.reshape(1, D_MODEL))
    return pl.pallas_call(
        functools.partial(_out_body, final=final),
        out_shape=jax.ShapeDtypeStruct((rows, D_MODEL), F32),
        grid=(nt,),
        in_specs=in_specs,
        out_specs=tile(D_MODEL),
        compiler_params=_cparams("parallel"),
    )(*args)


def _pack_w_in(w_in):
    o_dt = SSD_D + SSD_CONV_DIM
    o_u = o_dt + 2 * SSD_HEADS
    pad = IN_PACKED - w_in.shape[-1]
    return jnp.concatenate(
        [w_in[..., :o_dt], w_in[..., o_u:], w_in[..., o_dt:o_u],
         jnp.zeros(w_in.shape[:-1] + (pad,), w_in.dtype)], axis=-1).astype(BF16)


def _lane_pad(x):
    return jnp.pad(x, [(0, 0)] * (x.ndim - 1) + [(0, LANES - x.shape[-1])])


def kernel(x_prompt, x_sample, cache_nat_k, cache_nat_v, state_ssd, state_s5_re, state_s5_im,
           c, c_ctx, w_mod, b_mod, norm_mix, norm_mlp, w_in, ssd_conv_w, ssd_conv_b,
           ssd_dt_bias, ssd_a_log, ssd_d, ssd_norm, s5_a_re, s5_a_im, s5_log_dt,
           s5_b_re, s5_b_im, s5_c_re, s5_c_im, s5_d, s5_w_glu, s5_b_glu, nat_rpb,
           w_out, w_ff1, w_ff2, norm_f):
    bp, seq_p, _ = x_prompt.shape
    bs, seq_s, _ = x_sample.shape
    assert seq_p == TOK_TILE and seq_s == 4 * TOK_TILE and bp % 4 == 0
    rows_p = bp * seq_p

    ncp = -(-(1 + bs) // 8) * 8
    cond = jnp.concatenate([c_ctx[None, :], c, jnp.zeros((ncp - 1 - bs, D_MODEL), F32)], axis=0)
    mods = _mods_call(cond, w_mod, b_mod)
    mods4 = mods.reshape(DEPTH, ncp, 1, N_MOD * D_MODEL)

    w_in_p = _pack_w_in(w_in)
    ssd_w = {
        'conv_w': jnp.pad(ssd_conv_w.astype(F32), [(0, 0), (0, 8 - SSD_CONV), (0, 0)]),
        'conv_b': ssd_conv_b.astype(F32).reshape(DEPTH, 1, SSD_CONV_DIM),
        'dt_bias': _lane_pad(ssd_dt_bias.astype(F32).reshape(DEPTH, 1, 2 * SSD_HEADS)),
        'a': _lane_pad(-jnp.exp(ssd_a_log.astype(F32)).reshape(DEPTH, 1, 2 * SSD_HEADS)),
        'd': jnp.repeat(ssd_d.astype(F32), SSD_HEAD_DIM, axis=-1).reshape(DEPTH, 1, SSD_D),
        'norm': ssd_norm.astype(F32).reshape(DEPTH, 1, SSD_D),
    }
    out_w = {
        's5_d': s5_d.astype(F32).reshape(DEPTH, 1, S5_D),
        'w_glu': s5_w_glu.astype(BF16),
        'b_glu': s5_b_glu.astype(F32).reshape(DEPTH, 1, S5_D),
        'w_out': w_out.astype(BF16),
        'norm_mlp': norm_mlp.astype(F32).reshape(DEPTH, 1, D_MODEL),
        'w_ff1': w_ff1.astype(BF16),
        'w_ff2': w_ff2.astype(BF16),
    }
    s5_tabs = _s5_tables(s5_a_re, s5_a_im, s5_log_dt, s5_b_re, s5_b_im, s5_c_re, s5_c_im)
    nat_bias = _nat_bias_table(nat_rpb)

    cache_k = cache_nat_k.reshape(bs, DEPTH, -1, NAT_D)
    cache_v = cache_nat_v.reshape(bs, DEPTH, -1, NAT_D)
    st_ssd = state_ssd.reshape(bs, DEPTH, 2 * SSD_HEADS, SSD_HEAD_DIM, SSD_STATE)
    st5 = jnp.stack([state_s5_re[:, :, 0], state_s5_im[:, :, 0],
                     state_s5_re[:, :, 1], state_s5_im[:, :, 1]], axis=3)
    st5 = _lane_pad(st5.astype(F32)).reshape(bs, DEPTH, S5_GROUPS, S5_SW)
    st5 = jnp.transpose(st5, (1, 2, 0, 3))

    x = jnp.concatenate([x_prompt.reshape(rows_p, D_MODEL),
                         x_sample.reshape(bs * seq_s, D_MODEL)], axis=0)
    new_k, new_v, new_ssd, new_s5 = [], [], [], []
    for l in range(DEPTH):
        zx, dt, u, q, k, v = _inproj_call(x, mods4, norm_mix, w_in_p, l, bp)
        new_k.append(k[:rows_p])
        new_v.append(v[:rows_p])

        y_nat = _ctx_attn_call(q, k, v, bp)
        y_nat = _nat_attn_call(q, k, v, cache_k, cache_v, nat_bias, y_nat, l, bp, bs, seq_s)

        y_ssd, ssd_l = _ssd_call(zx, dt, ssd_w, l, None, None, bp, bp, seq_p)
        y_ssd = _ssd_call(zx, dt, ssd_w, l, y_ssd, st_ssd, bp, bs, seq_s)
        new_ssd.append(ssd_l)

        y5_p, s5_l = _s5_call(_s5_to_groups(u[:rows_p], bp, seq_p), s5_tabs, l, None, bp, seq_p)
        y5_s = _s5_call(_s5_to_groups(u[rows_p:], bs, seq_s), s5_tabs, l, st5[l], bs, seq_s)
        y5 = jnp.concatenate([_s5_from_groups(y5_p, bp, seq_p),
                              _s5_from_groups(y5_s, bs, seq_s)], axis=0)
        new_s5.append(s5_l)

        x = _out_call(x, y_ssd, y5, u, y_nat, mods4, out_w, norm_f, l, bp, l == DEPTH - 1)

    y_prompt = x[:rows_p].reshape(bp, seq_p, D_MODEL)
    y_sample = x[rows_p:].reshape(bs, seq_s, D_MODEL)
    out_k = jnp.stack(new_k, axis=0).reshape(DEPTH, bp, seq_p, NAT_HEADS, NAT_HEAD_DIM)
    out_v = jnp.stack(new_v, axis=0).reshape(DEPTH, bp, seq_p, NAT_HEADS, NAT_HEAD_DIM)
    out_k = jnp.transpose(out_k, (1, 0, 2, 3, 4))
    out_v = jnp.transpose(out_v, (1, 0, 2, 3, 4))
    out_ssd = jnp.stack(new_ssd, axis=1).reshape(bp, DEPTH, 2, SSD_HEADS, SSD_HEAD_DIM, SSD_STATE)
    s5 = jnp.stack(new_s5, axis=0).reshape(DEPTH, S5_GROUPS, bp, 4, LANES)[..., :S5_STATE]
    s5 = jnp.transpose(s5, (2, 0, 3, 1, 4))
    out_re = s5[:, :, 0::2]
    out_im = s5[:, :, 1::2]
    return y_prompt, y_sample, out_k, out_v, out_ssd, out_re, out_im
```

```python
import functools
import math

import numpy as np
import jax
import jax.numpy as jnp
from jax import lax
from jax.experimental import pallas as pl
from jax.experimental.pallas import tpu as pltpu

F32 = jnp.float32
BF16 = jnp.bfloat16
HIGHEST = lax.Precision.HIGHEST

D_MODEL = 1024
DEPTH = 4
GRID_W = 64
SSD_HEADS = 6
SSD_HEAD_DIM = 64
SSD_D = SSD_HEADS * SSD_HEAD_DIM
SSD_GROUPS = 2
SSD_STATE = 64
SSD_CONV = 5
SSD_CHUNK = 128
SSD_CONV_DIM = SSD_D + 2 * SSD_GROUPS * SSD_STATE
S5_GROUPS = 16
S5_GROUP_CH = 16
S5_D = S5_GROUPS * S5_GROUP_CH
S5_STATE = 64
NAT_HEADS = 6
NAT_HEAD_DIM = 64
NAT_D = NAT_HEADS * NAT_HEAD_DIM
NAT_KH = 8
NAT_KW = 16
D_FF = 4 * D_MODEL
N_MOD = 6
EPS = 1e-6

LANES = 128
TOK_TILE = 256
IN_PACKED = 2560
S5_T = 8
S5_HGROUPS = LANES // S5_GROUP_CH
S5_KW = S5_T * LANES
S5_SW = 4 * S5_GROUPS * S5_STATE
S5_SH = S5_SW // 2
S5_NC = 32
S5_HB = 8
S5_ROWS = S5_HB * S5_NC
VMEM_LIMIT = 56 * 1024 * 1024


def _cparams(*sem):
    return pltpu.CompilerParams(dimension_semantics=sem, vmem_limit_bytes=VMEM_LIMIT)


def _dot(a, b, precision=None):
    return jnp.dot(a, b, preferred_element_type=F32, precision=precision)


def _dot_nt(a, b, precision=None):
    return lax.dot_general(a, b, (((1,), (1,)), ((), ())), preferred_element_type=F32,
                           precision=precision)


def _dot_tn(a, b, precision=None):
    return lax.dot_general(a, b, (((0,), (0,)), ((), ())), preferred_element_type=F32,
                           precision=precision)


def _silu(x):
    return x * jax.nn.sigmoid(x)


def _rms(x, g):
    return x * lax.rsqrt(jnp.mean(x * x, axis=-1, keepdims=True) + EPS) * g


def _mods_body(cond_ref, w_ref, b_ref, o_ref):
    s = _silu(cond_ref[...])
    o_ref[...] = _dot(s.astype(BF16), w_ref[...].astype(BF16)) + b_ref[...]


def _mods_call(cond, w_mod, b_mod):
    ncp = cond.shape[0]
    blk = D_MODEL
    return pl.pallas_call(
        _mods_body,
        out_shape=jax.ShapeDtypeStruct((DEPTH, ncp, N_MOD * D_MODEL), F32),
        grid=(DEPTH, N_MOD),
        in_specs=[
            pl.BlockSpec((ncp, D_MODEL), lambda l, j: (0, 0)),
            pl.BlockSpec((None, D_MODEL, blk), lambda l, j: (l, 0, j)),
            pl.BlockSpec((None, 1, blk), lambda l, j: (l, 0, j)),
        ],
        out_specs=pl.BlockSpec((None, ncp, blk), lambda l, j: (l, 0, j)),
        compiler_params=_cparams("arbitrary", "arbitrary"),
        name="adaln_mods",
    )(cond, w_mod, b_mod.reshape(DEPTH, 1, N_MOD * D_MODEL))


def _inproj_body(x_ref, mod_ref, g_ref, w_ref, zx_ref, dt_ref, u_ref, q_ref, k_ref, v_ref):
    x = x_ref[...]
    h = _rms(x, g_ref[...]) * (1.0 + mod_ref[:, D_MODEL:2 * D_MODEL]) + mod_ref[:, 0:D_MODEL]
    p = _dot(h.astype(BF16), w_ref[...])
    zx_ref[...] = p[:, 0:1024]
    u_ref[...] = p[:, 1024:1280]
    q_ref[...] = p[:, 1280:1664]
    k_ref[...] = p[:, 1664:2048]
    v_ref[...] = p[:, 2048:2432]
    dt_ref[...] = p[:, 2432:2560]


def _inproj_call(x, mods4, norm_mix, w_in_p, layer, bp):
    rows = x.shape[0]
    nt = rows // TOK_TILE

    def cond_idx(i):
        return jnp.where(i < bp, 0, 1 + (i - bp) // 4)

    def tile(w):
        return pl.BlockSpec((TOK_TILE, w), lambda i: (i, 0))

    widths = (1024, LANES, S5_D, NAT_D, NAT_D, NAT_D)
    return pl.pallas_call(
        _inproj_body,
        out_shape=[jax.ShapeDtypeStruct((rows, w), F32) for w in widths],
        grid=(nt,),
        in_specs=[
            tile(D_MODEL),
            pl.BlockSpec((None, None, 1, 2 * D_MODEL), lambda i: (layer, cond_idx(i), 0, 0)),
            pl.BlockSpec((None, 1, D_MODEL), lambda i: (layer, 0, 0)),
            pl.BlockSpec((None, D_MODEL, IN_PACKED), lambda i: (layer, 0, 0),
                         pipeline_mode=pl.Buffered(1)),
        ],
        out_specs=[tile(w) for w in widths],
        compiler_params=_cparams("parallel"),
        name="in_proj",
    )(x, mods4, norm_mix.reshape(DEPTH, 1, D_MODEL), w_in_p)


NAT_SCALE = NAT_HEAD_DIM ** -0.5


def _softmax_pv(scores, values):
    m = functools.reduce(jnp.maximum, [jnp.max(s, axis=-1, keepdims=True) for s in scores])
    ps = [jnp.exp(s - m) for s in scores]
    den = functools.reduce(lambda a, b: a + b, [jnp.sum(p, axis=-1, keepdims=True) for p in ps])
    num = functools.reduce(lambda a, b: a + b,
                           [_dot(p.astype(BF16), v) for p, v in zip(ps, values)])
    return num / den


def _ctx_attn_body(q_ref, k_ref, v_ref, o_ref):
    for h in range(NAT_HEADS):
        hs = slice(h * NAT_HEAD_DIM, (h + 1) * NAT_HEAD_DIM)
        q = (q_ref[:, hs] * NAT_SCALE).astype(BF16)
        k = k_ref[:, hs].astype(BF16)
        v = v_ref[:, hs].astype(BF16)
        o_ref[:, hs] = _softmax_pv([_dot_nt(q, k)], [v])


def _ctx_attn_call(q, k, v, bp):
    rows = q.shape[0]
    spec = pl.BlockSpec((TOK_TILE, NAT_D), lambda b: (b, 0))
    return pl.pallas_call(
        _ctx_attn_body,
        out_shape=jax.ShapeDtypeStruct((rows, NAT_D), F32),
        grid=(bp,),
        in_specs=[spec, spec, spec],
        out_specs=spec,
        compiler_params=_cparams("parallel"),
        name="ctx_attn",
    )(q, k, v)


NAT_QROWS = 4
NAT_HEAD_PAIRS = NAT_HEADS // 2


def _nat_blocks(rows):
    kh = min(NAT_KH, rows)
    out = []
    for j in range(rows // NAT_QROWS):
        rs = [int(np.clip(r - kh // 2, 0, rows - kh)) for r in range(j * NAT_QROWS, (j + 1) * NAT_QROWS)]
        first = min(rs)
        n = -(-(max(rs) + kh - first) // NAT_QROWS) * NAT_QROWS
        first = min(first, rows - n)
        out.append((first, n))
    return out


def _nat_bias_blocks(rpb, rows):
    kh = min(NAT_KH, rows)
    blocks = _nat_blocks(rows)
    nmax = max(n for _, n in blocks)
    cols = np.arange(GRID_W)
    c_start = np.clip(cols - NAT_KW // 2, 0, GRID_W - NAT_KW)
    col_mask = (cols[None, :] >= c_start[:, None]) & (cols[None, :] < c_start[:, None] + NAT_KW)
    rpb = rpb.astype(F32)
    side = GRID_W - NAT_KW + 1
    neg = jnp.full(rpb.shape[:-1] + (side,), -jnp.inf, F32)
    padded = jnp.concatenate([neg, rpb, neg], axis=-1)
    tiles = jnp.stack([padded[..., side + NAT_KW - 1 - qc:side + NAT_KW - 1 - qc + GRID_W]
                       for qc in range(GRID_W)], axis=-2)
    tiles = jnp.where(col_mask, tiles, -jnp.inf)
    neg_tile = jnp.full(tiles.shape[:2] + (GRID_W, GRID_W), -jnp.inf, F32)
    out = []
    for j, (first, n) in enumerate(blocks):
        qrows = []
        for ql in range(NAT_QROWS):
            qr = j * NAT_QROWS + ql
            rs = int(np.clip(qr - kh // 2, 0, rows - kh))
            pieces = []
            for kl in range(nmax):
                kr = first + kl
                ok = kl < n and rs <= kr < rs + kh
                pieces.append(tiles[:, :, kr - qr + NAT_KH - 1] if ok else neg_tile)
            qrows.append(jnp.concatenate(pieces, axis=-1))
        out.append(jnp.concatenate(qrows, axis=-2))
    t = jnp.stack(out, axis=2)
    return t.reshape(DEPTH, NAT_HEAD_PAIRS, 2, len(blocks), NAT_QROWS * GRID_W, nmax * GRID_W)


def _nat_attn_body(q_ref, k_ref, v_ref, kc_ref, vc_ref, bias_ref, yin_ref, o_ref, *, rows):
    del yin_ref
    lo = lax.broadcasted_iota(jnp.int32, (1, LANES), 1) < NAT_HEAD_DIM
    kc = kc_ref[...].astype(BF16)
    vc = vc_ref[...].astype(BF16)
    nq = NAT_QROWS * GRID_W
    for j, (first, n) in enumerate(_nat_blocks(rows)):
        q = q_ref[j * nq:(j + 1) * nq, :] * NAT_SCALE
        kw = k_ref[first * GRID_W:(first + n) * GRID_W, :].astype(BF16)
        vw = v_ref[first * GRID_W:(first + n) * GRID_W, :].astype(BF16)
        outs = []
        for hh in range(2):
            qm = jnp.where(lo if hh == 0 else jnp.logical_not(lo), q, 0.0).astype(BF16)
            s_lat = _dot_nt(qm, kw) + bias_ref[hh, j, :, 0:n * GRID_W]
            s_ctx = _dot_nt(qm, kc)
            outs.append(_softmax_pv([s_lat, s_ctx], [vw, vc]))
        o_ref[j * nq:(j + 1) * nq, :] = jnp.where(lo, outs[0], outs[1])


def _nat_attn_call(q, k, v, cache_k, cache_v, bias, y_nat, layer, bp, bs, seq):
    rows = seq // GRID_W
    assert rows % NAT_QROWS == 0 and (bp * TOK_TILE) % seq == 0
    off = bp * TOK_TILE // seq
    spec = pl.BlockSpec((seq, LANES), lambda p, b: (off + b, p))
    cspec = pl.BlockSpec((None, None, cache_k.shape[2], LANES), lambda p, b: (b, layer, 0, p))
    return pl.pallas_call(
        functools.partial(_nat_attn_body, rows=rows),
        out_shape=jax.ShapeDtypeStruct(y_nat.shape, F32),
        grid=(NAT_HEAD_PAIRS, bs),
        in_specs=[spec, spec, spec, cspec, cspec,
                  pl.BlockSpec((None, None) + bias.shape[2:], lambda p, b: (layer, p, 0, 0, 0, 0)),
                  pl.BlockSpec(memory_space=pl.ANY)],
        out_specs=spec,
        input_output_aliases={6: 0},
        compiler_params=_cparams("parallel", "parallel"),
        name="nat_attn",
    )(q, k, v, cache_k, cache_v, bias, y_nat)


def _ssd_body(*refs, seq, has_h0):
    zx_ref, dt_ref, cw_ref, cb_ref, dtb_ref, a_ref, d_ref, nw_ref = refs[:8]
    refs = refs[8:]
    if has_h0:
        h0_ref, yin_ref, y_ref = refs[:3]
        refs = refs[3:]
        hout_ref = None
    else:
        y_ref, hout_ref = refs[:2]
        refs = refs[2:]
        h0_ref = None
    xbc_s, y_s, ac_s, dtv_s, st_s, h_s = refs

    q = SSD_CHUNK
    nc = seq // q
    nh = SSD_HEADS
    hd = SSD_HEAD_DIM
    b_off = SSD_D
    c_off = SSD_D + SSD_GROUPS * SSD_STATE

    row = lax.broadcasted_iota(jnp.int32, (seq, 1), 0)
    for cblk in range(SSD_CONV_DIM // LANES):
        cs = slice(cblk * LANES, (cblk + 1) * LANES)
        xin = zx_ref[:, SSD_D + cblk * LANES:SSD_D + (cblk + 1) * LANES]
        acc = cb_ref[:, cs] + cw_ref[SSD_CONV // 2:SSD_CONV // 2 + 1, cs] * xin
        for kk in range(SSD_CONV):
            d = kk - SSD_CONV // 2
            if d == 0:
                continue
            sh = pltpu.roll(xin, (-d) % seq, axis=0)
            ok = (row + d >= 0) & (row + d < seq)
            acc = acc + cw_ref[kk:kk + 1, cs] * jnp.where(ok, sh, 0.0)
        xbc_s[:, cs] = _silu(acc)

    lane = lax.broadcasted_iota(jnp.int32, (1, LANES), 1)
    xdt = dt_ref[...] + dtb_ref[...]
    dtv = jnp.maximum(xdt, 0.0) + jnp.log1p(jnp.exp(-jnp.abs(xdt)))
    dtv_s[...] = jnp.where(lane < 2 * nh, dtv, 0.0)

    ii = lax.broadcasted_iota(jnp.int32, (q, q), 0)
    jj = lax.broadcasted_iota(jnp.int32, (q, q), 1)
    lower = jj <= ii
    upper = jj >= ii
    is_fwd = lane < nh

    def chunk(c, carry):
        r0 = pl.multiple_of(c * q, q)
        rs = pl.ds(r0, q)
        dt_c = dtv_s[rs, :]
        da_c = dt_c * a_ref[...]
        ac = jnp.where(is_fwd,
                       _dot(lower.astype(F32), da_c, HIGHEST),
                       _dot(upper.astype(F32), da_c, HIGHEST))
        ac_s[rs, :] = ac
        ac_t = ac.T
        dt_t = dt_c.T
        last = jnp.where(is_fwd, ac[q - 1:q, :], ac[0:1, :])
        wend = dt_c * jnp.exp(last - ac)
        for g in range(SSD_GROUPS):
            bg = xbc_s[rs, b_off + g * SSD_STATE:b_off + (g + 1) * SSD_STATE].astype(BF16)
            cg = xbc_s[rs, c_off + g * SSD_STATE:c_off + (g + 1) * SSD_STATE].astype(BF16)
            cb = _dot_nt(cg, bg)
            for hh in range(nh // SSD_GROUPS):
                h = g * (nh // SSD_GROUPS) + hh
                hs = slice(h * hd, (h + 1) * hd)
                seg_f = ac[:, h:h + 1] - ac_t[h:h + 1, :]
                seg_b = ac[:, nh + h:nh + h + 1] - ac_t[nh + h:nh + h + 1, :]
                w = cb * (jnp.exp(jnp.where(lower, seg_f, -jnp.inf)) * dt_t[h:h + 1, :]
                          + jnp.exp(jnp.where(upper, seg_b, -jnp.inf)) * dt_t[nh + h:nh + h + 1, :])
                xh = xbc_s[rs, hs]
                y_s[rs, hs] = _dot(w.astype(BF16), xh.astype(BF16))
                st_s[c * 2 * nh + h] = _dot_tn((xh * wend[:, h:h + 1]).astype(BF16), bg)
                st_s[c * 2 * nh + nh + h] = _dot_tn(
                    (xh * wend[:, nh + h:nh + h + 1]).astype(BF16), bg)
        return carry

    lax.fori_loop(0, nc, chunk, 0)

    if has_h0:
        h_s[...] = h0_ref[...]
    else:
        h_s[...] = jnp.zeros(h_s.shape, F32)

    def carry_states(kstep, carry):
        for direction in range(2):
            c = kstep if direction == 0 else nc - 1 - kstep
            r0 = pl.multiple_of(c * q, q)
            rs = pl.ds(r0, q)
            ac = ac_s[rs, :]
            edge = ac[q - 1:q, :] if direction == 0 else ac[0:1, :]
            for g in range(SSD_GROUPS):
                cg = xbc_s[rs, c_off + g * SSD_STATE:c_off + (g + 1) * SSD_STATE].astype(BF16)
                for hh in range(nh // SSD_GROUPS):
                    h = g * (nh // SSD_GROUPS) + hh
                    hl = direction * nh + h
                    hs = slice(h * hd, (h + 1) * hd)
                    h_in = h_s[hl]
                    y_s[rs, hs] += _dot_nt(cg, h_in.astype(BF16)) * jnp.exp(ac[:, hl:hl + 1])
                    h_s[hl] = jnp.exp(edge[:, hl:hl + 1]) * h_in + st_s[c * 2 * nh + hl]
        return carry

    lax.fori_loop(0, nc, carry_states, 0)

    y = y_s[...] + d_ref[...] * xbc_s[:, 0:SSD_D]
    y = y * _silu(zx_ref[:, 0:SSD_D])
    y_ref[...] = _rms(y, nw_ref[...])
    if hout_ref is not None:
        hout_ref[...] = h_s[...]


def _ssd_call(zx, dt, lw, layer, y_ssd, state, bp, nseq, seq):
    has_h0 = state is not None
    rows = zx.shape[0]
    off = 0 if not has_h0 else bp * TOK_TILE // seq
    nc = seq // SSD_CHUNK
    nst = 2 * SSD_HEADS

    def seqspec(w):
        return pl.BlockSpec((seq, w), lambda b: (off + b, 0))

    def par(shape):
        return pl.BlockSpec((None,) + shape, lambda b: (layer,) + (0,) * len(shape))

    in_specs = [seqspec(1024), seqspec(LANES), par((8, SSD_CONV_DIM)), par((1, SSD_CONV_DIM)),
                par((1, LANES)), par((1, LANES)), par((1, SSD_D)), par((1, SSD_D))]
    args = [zx, dt, lw['conv_w'], lw['conv_b'], lw['dt_bias'], lw['a'], lw['d'], lw['norm']]
    st_spec = pl.BlockSpec((None, None, nst, SSD_HEAD_DIM, SSD_STATE), lambda b: (b, layer, 0, 0, 0))
    y_shape = jax.ShapeDtypeStruct((rows, SSD_D), F32)
    if has_h0:
        in_specs += [st_spec, pl.BlockSpec(memory_space=pl.ANY)]
        args += [state, y_ssd]
        out_shape = y_shape
        out_specs = seqspec(SSD_D)
        aliases = {9: 0}
    else:
        out_shape = [y_shape, jax.ShapeDtypeStruct((nseq, nst, SSD_HEAD_DIM, SSD_STATE), F32)]
        out_specs = [seqspec(SSD_D),
                     pl.BlockSpec((None, nst, SSD_HEAD_DIM, SSD_STATE), lambda b: (b, 0, 0, 0))]
        aliases = {}
    return pl.pallas_call(
        functools.partial(_ssd_body, seq=seq, has_h0=has_h0),
        out_shape=out_shape,
        grid=(nseq,),
        in_specs=in_specs,
        out_specs=out_specs,
        scratch_shapes=[
            pltpu.VMEM((seq, SSD_CONV_DIM), F32),
            pltpu.VMEM((seq, SSD_D), F32),
            pltpu.VMEM((seq, LANES), F32),
            pltpu.VMEM((seq, LANES), F32),
            pltpu.VMEM((nc * nst, SSD_HEAD_DIM, SSD_STATE), F32),
            pltpu.VMEM((nst, SSD_HEAD_DIM, SSD_STATE), F32),
        ],
        input_output_aliases=aliases,
        compiler_params=_cparams("parallel"),
        name="ssd_mixer",
    )(*args)


def _s5_tables(a_re, a_im, log_dt, b_re, b_im, c_re, c_im):
    t = S5_T
    gh = S5_HGROUPS
    f32 = lambda x: x.astype(F32)
    lam_r, lam_i = f32(a_re), f32(a_im)
    step = jnp.exp(f32(log_dt))[..., None]
    xr, xi = lam_r * step, lam_i * step
    tau = jnp.arange(t + 1, dtype=F32)[None, None, :, None, None]
    mag = jnp.exp(xr[:, :, None] * tau)
    pw_r = mag * jnp.cos(xi[:, :, None] * tau)
    pw_i = mag * jnp.sin(xi[:, :, None] * tau)
    nr, ni = pw_r[:, :, 1] - 1.0, pw_i[:, :, 1]
    den = lam_r * lam_r + lam_i * lam_i
    fr = (nr * lam_r + ni * lam_i) / den
    fi = (ni * lam_r - nr * lam_i) / den
    br, bi = f32(b_re)[:, None], f32(b_im)[:, None]
    bb_r = fr[..., None] * br - fi[..., None] * bi
    bb_i = fr[..., None] * bi + fi[..., None] * br
    cr, ci = f32(c_re), f32(c_im)

    def pows(direction, taus):
        return (jnp.stack([pw_r[:, direction, k] for k in taus], axis=1),
                jnp.stack([pw_i[:, direction, k] for k in taus], axis=1))

    def c_times_pow(direction, taus):
        pr, pi = pows(direction, taus)
        pr, pi = pr[:, :, :, None, :], pi[:, :, :, None, :]
        return cr[:, None] * pr - ci[:, None] * pi, cr[:, None] * pi + ci[:, None] * pr

    kern = []
    for direction in range(2):
        cpr, cpi = c_times_pow(direction, range(t))
        kern.append(jnp.einsum('dtgcn,dgnk->dtgck', cpr, bb_r[:, direction], precision=HIGHEST)
                    - jnp.einsum('dtgcn,dgnk->dtgck', cpi, bb_i[:, direction], precision=HIGHEST))
    kf, kb = kern
    lag = [kb[:, -d] if d < 0 else (kf[:, 0] + kb[:, 0] if d == 0 else kf[:, d])
           for d in range(-(t - 1), t)]
    eye = jnp.eye(gh, dtype=F32)

    def halves(x, lead):
        return x.reshape(x.shape[:lead] + (2, gh) + x.shape[lead + 1:])

    kst = jnp.stack([jnp.stack([lag[to - s + t - 1] for to in range(t)], axis=1)
                     for s in range(t)], axis=1)
    kst = jnp.transpose(halves(kst, 3), (0, 3, 1, 4, 6, 2, 5))
    toep = kst[:, :, :, :, :, :, None, :] * eye[None, None, None, :, None, None, :, None]
    toep = toep.reshape(DEPTH, 2, S5_KW, S5_KW)
    pfr, pfi = pows(0, [t - 1 - s for s in range(t)])
    pbr, pbi = pows(1, range(t))
    sfr = pfr[..., None] * bb_r[:, None, 0] - pfi[..., None] * bb_i[:, None, 0]
    sfi = pfr[..., None] * bb_i[:, None, 0] + pfi[..., None] * bb_r[:, None, 0]
    sbr = pbr[..., None] * bb_r[:, None, 1] - pbi[..., None] * bb_i[:, None, 1]
    sbi = pbr[..., None] * bb_i[:, None, 1] + pbi[..., None] * bb_r[:, None, 1]
    sop = jnp.stack([sfr, sfi, sbr, sbi], axis=2)
    sop = jnp.transpose(halves(sop, 3), (0, 3, 1, 4, 6, 2, 5))
    sop = sop[:, :, :, :, :, :, None, :] * eye[None, None, None, :, None, None, :, None]
    sop = sop.reshape(DEPTH, 2, S5_KW, S5_SH)
    afr, afi = c_times_pow(0, range(1, t + 1))
    abr, abi = c_times_pow(1, [t - k for k in range(t)])
    aop = jnp.stack([afr, -afi, abr, -abi], axis=1)
    aop = jnp.transpose(halves(aop, 3), (0, 3, 1, 4, 6, 2, 5))
    aop = aop[:, :, :, :, :, :, None, :] * eye[None, None, None, :, None, None, :, None]
    aop = aop.reshape(DEPTH, 2, S5_SH, S5_KW)
    apow = jnp.stack([pw_r[:, 0, t], pw_i[:, 0, t], pw_r[:, 1, t], pw_i[:, 1, t]], axis=1)
    kseg = t * jnp.arange(S5_NC + 1, dtype=F32)[None, None, :, None, None]
    smag = jnp.exp(xr[:, :, None] * kseg)
    seg_r = smag * jnp.cos(xi[:, :, None] * kseg)
    seg_i = smag * jnp.sin(xi[:, :, None] * kseg)
    back = [S5_NC - 1 - c for c in range(S5_NC)]
    pseg = jnp.stack([seg_r[:, 0, :S5_NC], seg_i[:, 0, :S5_NC],
                      jnp.stack([seg_r[:, 1, k] for k in back], axis=1),
                      jnp.stack([seg_i[:, 1, k] for k in back], axis=1)], axis=1)
    nkb = S5_GROUPS * S5_STATE // LANES
    pseg = jnp.transpose(pseg.reshape(DEPTH, 4, S5_NC, nkb, LANES), (0, 1, 3, 2, 4))
    pseg = pseg.reshape(DEPTH, 4 * nkb, S5_NC, LANES)
    aseg = jnp.stack([seg_r[:, 0, S5_NC], seg_i[:, 0, S5_NC],
                      seg_r[:, 1, S5_NC], seg_i[:, 1, S5_NC]], axis=1)
    return (toep.astype(BF16), sop.astype(BF16), aop.astype(BF16),
            apow.reshape(DEPTH, 1, S5_SW), pseg, aseg.reshape(DEPTH, 1, S5_SW))


def _s5_body(u_ref, toep_ref, sop_ref, aop_ref, at_ref, pseg_ref, aseg_ref, h0_ref,
             y_ref, hout_ref, st_s, hin_s, *, nblk_p, nseg_s):
    cw = S5_GROUPS * S5_STATE
    nkb = cw // LANES
    for hf in range(2):
        xh = jnp.concatenate(
            [u_ref[:, s * S5_D + hf * LANES:s * S5_D + (hf + 1) * LANES] for s in range(S5_T)],
            axis=1).astype(BF16)
        yh = _dot(xh, toep_ref[hf])
        for t in range(S5_T):
            y_ref[:, t * S5_D + hf * LANES:t * S5_D + (hf + 1) * LANES] = yh[:, t * LANES:(t + 1) * LANES]
        sh = _dot(xh, sop_ref[hf])
        for comp in range(4):
            for k in range(nkb // 2):
                col = (comp * (nkb // 2) + k) * LANES
                st_s[comp * nkb + hf * (nkb // 2) + k] = sh[:, col:col + LANES]

    def load(ref, comp, rows):
        return jnp.concatenate([ref[comp * nkb + k, rows, :] for k in range(nkb)], axis=1)

    def store(ref, comp, rows, val):
        for k in range(nkb):
            ref[comp * nkb + k, rows, :] = val[:, k * LANES:(k + 1) * LANES]

    def cmul(ar, ai, hr, hi):
        return ar * hr - ai * hi, ar * hi + ai * hr

    at = at_ref[...]
    ar_f, ai_f, ar_b, ai_b = [at[:, k * cw:(k + 1) * cw] for k in range(4)]

    def step(c, carry):
        hfr, hfi, hbr, hbi = carry
        rf = pl.ds(c, S5_HB, stride=S5_NC)
        rb = pl.ds(S5_NC - 1 - c, S5_HB, stride=S5_NC)
        for comp, val, rows in ((0, hfr, rf), (1, hfi, rf), (2, hbr, rb), (3, hbi, rb)):
            store(hin_s, comp, rows, val)
        fr, fi = cmul(ar_f, ai_f, hfr, hfi)
        br, bi = cmul(ar_b, ai_b, hbr, hbi)
        return (fr + load(st_s, 0, rf), fi + load(st_s, 1, rf),
                br + load(st_s, 2, rb), bi + load(st_s, 3, rb))

    zero = jnp.zeros((S5_HB, cw), F32)
    fin = lax.fori_loop(0, S5_NC, step, (zero, zero, zero, zero))
    hout_ref[...] = jnp.concatenate(fin, axis=-1)

    @pl.when(pl.program_id(0) >= nblk_p)
    def _():
        h0 = h0_ref[...]
        aseg = aseg_ref[...]
        sr_f, si_f, sr_b, si_b = [aseg[:, k * cw:(k + 1) * cw] for k in range(4)]
        ent = [[None] * S5_HB for _ in range(4)]
        for s in range(S5_HB // nseg_s):
            hr, hi = h0[s:s + 1, 0:cw], h0[s:s + 1, cw:2 * cw]
            for j in range(nseg_s):
                v = s * nseg_s + j
                ent[0][v], ent[1][v] = hr, hi
                hr, hi = cmul(sr_f, si_f, hr, hi)
                hr, hi = hr + fin[0][v:v + 1], hi + fin[1][v:v + 1]
            hr, hi = h0[s:s + 1, 2 * cw:3 * cw], h0[s:s + 1, 3 * cw:4 * cw]
            for j in range(nseg_s - 1, -1, -1):
                v = s * nseg_s + j
                ent[2][v], ent[3][v] = hr, hi
                hr, hi = cmul(sr_b, si_b, hr, hi)
                hr, hi = hr + fin[2][v:v + 1], hi + fin[3][v:v + 1]
        for direction in range(2):
            for k in range(nkb):
                kr = (2 * direction) * nkb + k
                ki = (2 * direction + 1) * nkb + k
                pr, pi = pseg_ref[kr], pseg_ref[ki]
                for v in range(S5_HB):
                    rows = slice(v * S5_NC, (v + 1) * S5_NC)
                    er = ent[2 * direction][v][:, k * LANES:(k + 1) * LANES]
                    ei = ent[2 * direction + 1][v][:, k * LANES:(k + 1) * LANES]
                    dr, di = cmul(pr, pi, er, ei)
                    hin_s[kr, rows, :] += dr
                    hin_s[ki, rows, :] += di

    for hf in range(2):
        hh = jnp.concatenate(
            [hin_s[comp * nkb + hf * (nkb // 2) + k] for comp in range(4) for k in range(nkb // 2)],
            axis=1).astype(BF16)
        yi = _dot(hh, aop_ref[hf])
        for t in range(S5_T):
            y_ref[:, t * S5_D + hf * LANES:t * S5_D + (hf + 1) * LANES] += yi[:, t * LANES:(t + 1) * LANES]


def _s5_call(ucat, tables, layer, h0, rows_p, seq_p, seq_s):
    toep, sop, aop, apow, pseg, aseg = tables
    seg_tok = S5_T * S5_NC
    assert seq_p == seg_tok and seq_s % seg_tok == 0 and S5_HB % (seq_s // seg_tok) == 0
    nblk = ucat.shape[0] // S5_ROWS
    nblk_p = rows_p // S5_T // S5_ROWS

    def par(a):
        return pl.BlockSpec((None,) + a.shape[1:], lambda i: (layer,) + (0,) * (a.ndim - 1),
                            pipeline_mode=pl.Buffered(1))

    rspec = pl.BlockSpec((S5_ROWS, S5_T * S5_D), lambda i: (i, 0))
    hspec = pl.BlockSpec((None, None, S5_HB, S5_SW), lambda i: (layer, i, 0, 0))
    ospec = pl.BlockSpec((None, S5_HB, S5_SW), lambda i: (i, 0, 0))
    nlb = S5_SW // LANES
    return pl.pallas_call(
        functools.partial(_s5_body, nblk_p=nblk_p, nseg_s=seq_s // seg_tok),
        out_shape=[jax.ShapeDtypeStruct(ucat.shape, F32),
                   jax.ShapeDtypeStruct((nblk, S5_HB, S5_SW), F32)],
        grid=(nblk,),
        in_specs=[rspec, par(toep), par(sop), par(aop), par(apow), par(pseg), par(aseg), hspec],
        out_specs=[rspec, ospec],
        scratch_shapes=[pltpu.VMEM((nlb, S5_ROWS, LANES), F32), pltpu.VMEM((nlb, S5_ROWS, LANES), F32)],
        compiler_params=_cparams("parallel"),
        name="s5_mixer",
    )(ucat, toep, sop, aop, apow, pseg, aseg, h0)


def _gelu_tanh(x):
    return 0.5 * x * (1.0 + jnp.tanh(math.sqrt(2.0 / math.pi) * (x + 0.044715 * (x * x * x))))


def _out_body(*refs, final):
    (x_ref, yssd_ref, y5_ref, u_ref, ynat_ref, g1_ref, m2_ref, d5_ref, wglu_ref, bglu_ref,
     wout_ref, nm_ref, w1_ref, w2_ref) = refs[:14]
    if final:
        nf_ref, o_ref = refs[14:]
    else:
        (o_ref,) = refs[14:]
    g = _gelu_tanh(y5_ref[...] + d5_ref[...] * u_ref[...])
    y5 = g * jax.nn.sigmoid(_dot(g.astype(BF16), wglu_ref[...]) + bglu_ref[...])
    mix = (_dot(yssd_ref[...].astype(BF16), wout_ref[0:SSD_D, :])
           + _dot(y5.astype(BF16), wout_ref[SSD_D:SSD_D + S5_D, :])
           + _dot(ynat_ref[...].astype(BF16), wout_ref[SSD_D + S5_D:, :]))
    x = x_ref[...] + g1_ref[...] * mix
    h2 = _rms(x, nm_ref[...]) * (1.0 + m2_ref[:, D_MODEL:2 * D_MODEL]) + m2_ref[:, 0:D_MODEL]
    f = jnp.maximum(_dot(h2.astype(BF16), w1_ref[...]), 0.0)
    f = (f * f).astype(BF16)
    x = x + m2_ref[:, 2 * D_MODEL:3 * D_MODEL] * _dot(f, w2_ref[...])
    if final:
        x = _rms(x, nf_ref[...])
    o_ref[...] = x


def _out_call(x, y_ssd, y5, u, y_nat, mods4, lw, norm_f, layer, bp, final):
    rows = x.shape[0]
    nt = rows // TOK_TILE

    def cond_idx(i):
        return jnp.where(i < bp, 0, 1 + (i - bp) // 4)

    def tile(w):
        return pl.BlockSpec((TOK_TILE, w), lambda i: (i, 0))

    def par(shape):
        return pl.BlockSpec((None,) + shape, lambda i: (layer,) + (0,) * len(shape),
                            pipeline_mode=pl.Buffered(1))

    in_specs = [
        tile(D_MODEL), tile(SSD_D), tile(S5_D), tile(S5_D), tile(NAT_D),
        pl.BlockSpec((None, None, 1, D_MODEL), lambda i: (layer, cond_idx(i), 0, 2)),
        pl.BlockSpec((None, None, 1, 3 * D_MODEL), lambda i: (layer, cond_idx(i), 0, 1)),
        par((1, S5_D)), par((S5_D, S5_D)), par((1, S5_D)),
        par((D_MODEL, D_MODEL)), par((1, D_MODEL)),
        par((D_MODEL, D_FF)), par((D_FF, D_MODEL)),
    ]
    args = [x, y_ssd, y5, u, y_nat, mods4, mods4, lw['s5_d'], lw['w_glu'], lw['b_glu'],
            lw['w_out'], lw['norm_mlp'], lw['w_ff1'], lw['w_ff2']]
    if final:
        in_specs.append(pl.BlockSpec((1, D_MODEL), lambda i: (0, 0)))
        args.append(norm_f.reshape(1, D_MODEL))
    return pl.pallas_call(
        functools.partial(_out_body, final=final),
        out_shape=jax.ShapeDtypeStruct((rows, D_MODEL), F32),
        grid=(nt,),
        in_specs=in_specs,
        out_specs=tile(D_MODEL),
        compiler_params=_cparams("parallel"),
        name="out_mlp",
    )(*args)


def _pack_w_in(w_in):
    o_dt = SSD_D + SSD_CONV_DIM
    o_u = o_dt + 2 * SSD_HEADS
    pad = IN_PACKED - w_in.shape[-1]
    return jnp.concatenate(
        [w_in[..., :o_dt], w_in[..., o_u:], w_in[..., o_dt:o_u],
         jnp.zeros(w_in.shape[:-1] + (pad,), w_in.dtype)], axis=-1).astype(BF16)


def _lane_pad(x):
    return jnp.pad(x, [(0, 0)] * (x.ndim - 1) + [(0, LANES - x.shape[-1])])


def kernel(x_prompt, x_sample, cache_nat_k, cache_nat_v, state_ssd, state_s5_re, state_s5_im,
           c, c_ctx, w_mod, b_mod, norm_mix, norm_mlp, w_in, ssd_conv_w, ssd_conv_b,
           ssd_dt_bias, ssd_a_log, ssd_d, ssd_norm, s5_a_re, s5_a_im, s5_log_dt,
           s5_b_re, s5_b_im, s5_c_re, s5_c_im, s5_d, s5_w_glu, s5_b_glu, nat_rpb,
           w_out, w_ff1, w_ff2, norm_f):
    bp, seq_p, _ = x_prompt.shape
    bs, seq_s, _ = x_sample.shape
    assert seq_p == TOK_TILE and seq_s == 4 * TOK_TILE and bp % 4 == 0
    rows_p = bp * seq_p

    ncp = -(-(1 + bs) // 8) * 8
    cond = jnp.concatenate([c_ctx[None, :], c, jnp.zeros((ncp - 1 - bs, D_MODEL), F32)], axis=0)
    mods = _mods_call(cond, w_mod, b_mod)
    mods4 = mods.reshape(DEPTH, ncp, 1, N_MOD * D_MODEL)

    w_in_p = _pack_w_in(w_in)
    ssd_w = {
        'conv_w': jnp.pad(ssd_conv_w.astype(F32), [(0, 0), (0, 8 - SSD_CONV), (0, 0)]),
        'conv_b': ssd_conv_b.astype(F32).reshape(DEPTH, 1, SSD_CONV_DIM),
        'dt_bias': _lane_pad(ssd_dt_bias.astype(F32).reshape(DEPTH, 1, 2 * SSD_HEADS)),
        'a': _lane_pad(-jnp.exp(ssd_a_log.astype(F32)).reshape(DEPTH, 1, 2 * SSD_HEADS)),
        'd': jnp.repeat(ssd_d.astype(F32), SSD_HEAD_DIM, axis=-1).reshape(DEPTH, 1, SSD_D),
        'norm': ssd_norm.astype(F32).reshape(DEPTH, 1, SSD_D),
    }
    out_w = {
        's5_d': s5_d.astype(F32).reshape(DEPTH, 1, S5_D),
        'w_glu': s5_w_glu.astype(BF16),
        'b_glu': s5_b_glu.astype(F32).reshape(DEPTH, 1, S5_D),
        'w_out': w_out.astype(BF16),
        'norm_mlp': norm_mlp.astype(F32).reshape(DEPTH, 1, D_MODEL),
        'w_ff1': w_ff1.astype(BF16),
        'w_ff2': w_ff2.astype(BF16),
    }
    s5_tabs = _s5_tables(s5_a_re, s5_a_im, s5_log_dt, s5_b_re, s5_b_im, s5_c_re, s5_c_im)
    nat_bias = _nat_bias_blocks(nat_rpb, seq_s // GRID_W)

    cache_k = cache_nat_k.reshape(bs, DEPTH, -1, NAT_D)
    cache_v = cache_nat_v.reshape(bs, DEPTH, -1, NAT_D)
    st_ssd = state_ssd.reshape(bs, DEPTH, 2 * SSD_HEADS, SSD_HEAD_DIM, SSD_STATE)
    s5_tok = S5_ROWS * S5_T
    assert rows_p % s5_tok == 0 and (bs * seq_s) % s5_tok == 0 and s5_tok % seq_s == 0
    nseq_s = s5_tok // seq_s
    st5 = jnp.stack([state_s5_re[:, :, 0], state_s5_im[:, :, 0],
                     state_s5_re[:, :, 1], state_s5_im[:, :, 1]], axis=2)
    st5 = jnp.transpose(st5.astype(F32).reshape(bs // nseq_s, nseq_s, DEPTH, S5_SW), (2, 0, 1, 3))
    st5 = jnp.pad(st5, [(0, 0), (rows_p // s5_tok, 0), (0, S5_HB - nseq_s), (0, 0)])

    x = jnp.concatenate([x_prompt.reshape(rows_p, D_MODEL),
                         x_sample.reshape(bs * seq_s, D_MODEL)], axis=0)
    new_k, new_v, new_ssd, new_s5 = [], [], [], []
    for l in range(DEPTH):
        zx, dt, u, q, k, v = _inproj_call(x, mods4, norm_mix, w_in_p, l, bp)
        new_k.append(k[:rows_p])
        new_v.append(v[:rows_p])

        y_nat = _ctx_attn_call(q, k, v, bp)
        y_nat = _nat_attn_call(q, k, v, cache_k, cache_v, nat_bias, y_nat, l, bp, bs, seq_s)

        y_ssd, ssd_l = _ssd_call(zx, dt, ssd_w, l, None, None, bp, bp, seq_p)
        y_ssd = _ssd_call(zx, dt, ssd_w, l, y_ssd, st_ssd, bp, bs, seq_s)
        new_ssd.append(ssd_l)

        y5, s5_l = _s5_call(u.reshape(-1, S5_T * S5_D), s5_tabs, l, st5, rows_p, seq_p, seq_s)
        y5 = y5.reshape(-1, S5_D)
        new_s5.append(s5_l[:rows_p // s5_tok])

        x = _out_call(x, y_ssd, y5, u, y_nat, mods4, out_w, norm_f, l, bp, l == DEPTH - 1)

    y_prompt = x[:rows_p].reshape(bp, seq_p, D_MODEL)
    y_sample = x[rows_p:].reshape(bs, seq_s, D_MODEL)
    out_k = jnp.stack(new_k, axis=0).reshape(DEPTH, bp, seq_p, NAT_HEADS, NAT_HEAD_DIM)
    out_v = jnp.stack(new_v, axis=0).reshape(DEPTH, bp, seq_p, NAT_HEADS, NAT_HEAD_DIM)
    out_k = jnp.transpose(out_k, (1, 0, 2, 3, 4))
    out_v = jnp.transpose(out_v, (1, 0, 2, 3, 4))
    out_ssd = jnp.stack(new_ssd, axis=1).reshape(bp, DEPTH, 2, SSD_HEADS, SSD_HEAD_DIM, SSD_STATE)
    s5 = jnp.stack(new_s5, axis=0)[:, :, :s5_tok // seq_p]
    s5 = s5.reshape(DEPTH, bp, 4, S5_GROUPS, S5_STATE)
    s5 = jnp.transpose(s5, (1, 0, 2, 3, 4))
    out_re = s5[:, :, 0::2]
    out_im = s5[:, :, 1::2]
    return y_prompt, y_sample, out_k, out_v, out_ssd, out_re, out_im
```

```python
import functools
import math

import numpy as np
import jax
import jax.numpy as jnp
from jax import lax
from jax.experimental import pallas as pl
from jax.experimental.pallas import tpu as pltpu

F32 = jnp.float32
BF16 = jnp.bfloat16
HIGHEST = lax.Precision.HIGHEST

D_MODEL = 1024
DEPTH = 4
GRID_W = 64
SSD_HEADS = 6
SSD_HEAD_DIM = 64
SSD_D = SSD_HEADS * SSD_HEAD_DIM
SSD_GROUPS = 2
SSD_STATE = 64
SSD_CONV = 5
SSD_CHUNK = 128
SSD_CONV_DIM = SSD_D + 2 * SSD_GROUPS * SSD_STATE
S5_GROUPS = 16
S5_GROUP_CH = 16
S5_D = S5_GROUPS * S5_GROUP_CH
S5_STATE = 64
NAT_HEADS = 6
NAT_HEAD_DIM = 64
NAT_D = NAT_HEADS * NAT_HEAD_DIM
NAT_KH = 8
NAT_KW = 16
D_FF = 4 * D_MODEL
N_MOD = 6
EPS = 1e-6

LANES = 128
TOK_TILE = 256
IN_PACKED = 2560
S5_T = 8
S5_HGROUPS = LANES // S5_GROUP_CH
S5_KW = S5_T * LANES
S5_CW = S5_GROUPS * S5_STATE
S5_SW = 4 * S5_CW
S5_SH = S5_SW // 2
S5_NC = 32
S5_HB = 8
S5_ROWS = S5_HB * S5_NC
S5_LAGS = 2 * S5_T - 1
VMEM_LIMIT = 56 * 1024 * 1024


def _cparams(*sem):
    return pltpu.CompilerParams(dimension_semantics=sem, vmem_limit_bytes=VMEM_LIMIT)


def _dot(a, b, precision=None):
    return jnp.dot(a, b, preferred_element_type=F32, precision=precision)


def _dot_nt(a, b, precision=None):
    return lax.dot_general(a, b, (((1,), (1,)), ((), ())), preferred_element_type=F32,
                           precision=precision)


def _dot_tn(a, b, precision=None):
    return lax.dot_general(a, b, (((0,), (0,)), ((), ())), preferred_element_type=F32,
                           precision=precision)


def _silu(x):
    return x * jax.nn.sigmoid(x)


def _rms(x, g):
    return x * lax.rsqrt(jnp.mean(x * x, axis=-1, keepdims=True) + EPS) * g


def _const_spec(shape, layer, single_buffer=True):
    return pl.BlockSpec((None,) + tuple(shape), lambda *_: (layer,) + (0,) * len(shape),
                        pipeline_mode=pl.Buffered(1) if single_buffer else None)


def _pair_specs(rows, width, na, off_a=0, off_b=0):
    return (pl.BlockSpec((rows, width), lambda i: (jnp.minimum(i, na - 1) + off_a, 0)),
            pl.BlockSpec((rows, width), lambda i: (jnp.maximum(i - na, 0) + off_b, 0)))


def _mods_body(cond_ref, w_ref, b_ref, o_ref):
    s = _silu(cond_ref[...])
    o_ref[...] = _dot(s.astype(BF16), w_ref[...].astype(BF16)) + b_ref[...]


def _mods_call(cond, w_mod, b_mod):
    ncp = cond.shape[0]
    blk = D_MODEL
    return pl.pallas_call(
        _mods_body,
        out_shape=jax.ShapeDtypeStruct((DEPTH, ncp, N_MOD * D_MODEL), F32),
        grid=(DEPTH, N_MOD),
        in_specs=[
            pl.BlockSpec((ncp, D_MODEL), lambda l, j: (0, 0)),
            pl.BlockSpec((None, D_MODEL, blk), lambda l, j: (l, 0, j)),
            pl.BlockSpec((None, 1, blk), lambda l, j: (l, 0, j)),
        ],
        out_specs=pl.BlockSpec((None, ncp, blk), lambda l, j: (l, 0, j)),
        compiler_params=_cparams("arbitrary", "arbitrary"),
        name="adaln_mods",
    )(cond, w_mod, b_mod.reshape(DEPTH, 1, N_MOD * D_MODEL))


def _inproj_body(x_ref, mod_ref, g_ref, w_ref, zx_ref, dt_ref, u_ref, q_ref, k_ref, v_ref):
    x = x_ref[...]
    h = _rms(x, g_ref[...]) * (1.0 + mod_ref[:, D_MODEL:2 * D_MODEL]) + mod_ref[:, 0:D_MODEL]
    p = _dot(h.astype(BF16), w_ref[...])
    zx_ref[...] = p[:, 0:1024]
    u_ref[...] = p[:, 1024:1280]
    q_ref[...] = p[:, 1280:1664]
    k_ref[...] = p[:, 1664:2048]
    v_ref[...] = p[:, 2048:2432]
    dt_ref[...] = p[:, 2432:2560]


def _inproj_call(x, tile_off, ntiles, cond_idx, mods4, norm_mix, w_in_p, layer):
    rows = ntiles * TOK_TILE

    def tile(w):
        return pl.BlockSpec((TOK_TILE, w), lambda i: (i, 0))

    widths = (1024, LANES, S5_D, NAT_D, NAT_D, NAT_D)
    return pl.pallas_call(
        _inproj_body,
        out_shape=[jax.ShapeDtypeStruct((rows, w), F32) for w in widths],
        grid=(ntiles,),
        in_specs=[
            pl.BlockSpec((TOK_TILE, D_MODEL), lambda i: (i + tile_off, 0)),
            pl.BlockSpec((None, None, 1, 2 * D_MODEL), lambda i: (layer, cond_idx(i), 0, 0)),
            _const_spec((1, D_MODEL), layer, single_buffer=False),
            _const_spec((D_MODEL, IN_PACKED), layer),
        ],
        out_specs=[tile(w) for w in widths],
        compiler_params=_cparams("parallel"),
        name="in_proj",
    )(x, mods4, norm_mix.reshape(DEPTH, 1, D_MODEL), w_in_p)


NAT_SCALE = NAT_HEAD_DIM ** -0.5


def _softmax_pv(scores, values):
    m = functools.reduce(jnp.maximum, [jnp.max(s, axis=-1, keepdims=True) for s in scores])
    ps = [jnp.exp(s - m) for s in scores]
    den = functools.reduce(lambda a, b: a + b, [jnp.sum(p, axis=-1, keepdims=True) for p in ps])
    num = functools.reduce(lambda a, b: a + b,
                           [_dot(p.astype(BF16), v) for p, v in zip(ps, values)])
    return num / den


def _ctx_attn_body(q_ref, k_ref, v_ref, o_ref):
    for h in range(NAT_HEADS):
        hs = slice(h * NAT_HEAD_DIM, (h + 1) * NAT_HEAD_DIM)
        q = (q_ref[:, hs] * NAT_SCALE).astype(BF16)
        k = k_ref[:, hs].astype(BF16)
        v = v_ref[:, hs].astype(BF16)
        o_ref[:, hs] = _softmax_pv([_dot_nt(q, k)], [v])


def _ctx_attn_call(q, k, v):
    rows = q.shape[0]
    spec = pl.BlockSpec((TOK_TILE, NAT_D), lambda b: (b, 0))
    return pl.pallas_call(
        _ctx_attn_body,
        out_shape=jax.ShapeDtypeStruct((rows, NAT_D), F32),
        grid=(rows // TOK_TILE,),
        in_specs=[spec, spec, spec],
        out_specs=spec,
        compiler_params=_cparams("parallel"),
        name="ctx_attn",
    )(q, k, v)


NAT_QROWS = 4
NAT_HEAD_PAIRS = NAT_HEADS // 2
NAT_NPAIR = 2 * NAT_KH


def _nat_window_start(r, rows):
    kh = min(NAT_KH, rows)
    return int(np.clip(r - kh // 2, 0, rows - kh))


def _nat_blocks(rows):
    kh = min(NAT_KH, rows)
    out = []
    for j in range(rows // NAT_QROWS):
        rs = [_nat_window_start(r, rows) for r in range(j * NAT_QROWS, (j + 1) * NAT_QROWS)]
        first = min(rs)
        n = -(-(max(rs) + kh - first) // NAT_QROWS) * NAT_QROWS
        first = min(first, rows - n)
        out.append((first, n))
    return out


def _nat_pair_tiles(rpb):
    cols = np.arange(GRID_W)
    c_start = np.clip(cols - NAT_KW // 2, 0, GRID_W - NAT_KW)
    col_mask = (cols[None, :] >= c_start[:, None]) & (cols[None, :] < c_start[:, None] + NAT_KW)
    idx = cols[None, :] - cols[:, None] + NAT_KW - 1
    sel = (idx[None] == np.arange(2 * NAT_KW - 1)[:, None, None]).astype(np.float32)
    tiles = jnp.einsum('dhab,bqk->dhaqk', rpb.astype(F32), sel, precision=HIGHEST)
    tiles = jnp.where(col_mask, tiles, -jnp.inf)
    neg = jnp.full(tiles.shape[:2] + (1, GRID_W, GRID_W), -jnp.inf, F32)
    ext = jnp.concatenate([neg, tiles, neg], axis=2)
    pairs = jnp.concatenate([ext[:, :, 0:NAT_NPAIR], ext[:, :, 1:NAT_NPAIR + 1]], axis=-1)
    return pairs.reshape(DEPTH, NAT_HEAD_PAIRS, 2, NAT_NPAIR, GRID_W, 2 * GRID_W)


def _nat_block_bias(bias_ref, hh, j, first, n, rows, lo):
    kh = min(NAT_KH, rows)
    neg = jnp.full((GRID_W, 2 * GRID_W), -jnp.inf, F32)
    row_blocks = []
    for ql in range(NAT_QROWS):
        qr = j * NAT_QROWS + ql
        rs = _nat_window_start(qr, rows)
        pieces = []
        for m in range(n // 2):
            k0 = first + 2 * m
            ok0 = rs <= k0 < rs + kh
            ok1 = rs <= k0 + 1 < rs + kh
            if not (ok0 or ok1):
                pieces.append(neg)
                continue
            t = bias_ref[hh, k0 - qr + NAT_KH]
            if ok0 and ok1:
                pieces.append(t)
            elif ok0:
                pieces.append(jnp.where(lo, t, -jnp.inf))
            else:
                pieces.append(jnp.where(lo, -jnp.inf, t))
        row_blocks.append(jnp.concatenate(pieces, axis=1))
    return jnp.concatenate(row_blocks, axis=0)


def _nat_attn_body(q_ref, k_ref, v_ref, kc_ref, vc_ref, bias_ref, o_ref, *, rows):
    lo = lax.broadcasted_iota(jnp.int32, (1, LANES), 1) < NAT_HEAD_DIM
    kc = kc_ref[...].astype(BF16)
    vc = vc_ref[...].astype(BF16)
    nq = NAT_QROWS * GRID_W
    for j, (first, n) in enumerate(_nat_blocks(rows)):
        q = q_ref[j * nq:(j + 1) * nq, :] * NAT_SCALE
        kw = k_ref[first * GRID_W:(first + n) * GRID_W, :].astype(BF16)
        vw = v_ref[first * GRID_W:(first + n) * GRID_W, :].astype(BF16)
        outs = []
        for hh in range(2):
            qm = jnp.where(lo if hh == 0 else jnp.logical_not(lo), q, 0.0).astype(BF16)
            s_lat = _dot_nt(qm, kw) + _nat_block_bias(bias_ref, hh, j, first, n, rows, lo)
            s_ctx = _dot_nt(qm, kc)
            outs.append(_softmax_pv([s_lat, s_ctx], [vw, vc]))
        o_ref[j * nq:(j + 1) * nq, :] = jnp.where(lo, outs[0], outs[1])


def _nat_attn_call(q, k, v, cache_k, cache_v, bias, layer, bs, seq):
    rows = seq // GRID_W
    assert rows % NAT_QROWS == 0 and 2 * GRID_W == LANES
    spec = pl.BlockSpec((seq, LANES), lambda p, b: (b, p))
    cspec = pl.BlockSpec((None, None, cache_k.shape[2], LANES), lambda p, b: (b, layer, 0, p))
    return pl.pallas_call(
        functools.partial(_nat_attn_body, rows=rows),
        out_shape=jax.ShapeDtypeStruct(q.shape, F32),
        grid=(NAT_HEAD_PAIRS, bs),
        in_specs=[spec, spec, spec, cspec, cspec,
                  pl.BlockSpec((None, None) + bias.shape[2:], lambda p, b: (layer, p, 0, 0, 0, 0))],
        out_specs=spec,
        compiler_params=_cparams("parallel", "parallel"),
        name="nat_attn",
    )(q, k, v, cache_k, cache_v, bias)


def _ssd_body(*refs, seq, has_h0):
    zx_ref, dt_ref, cw_ref, cb_ref, dtb_ref, a_ref, d_ref, nw_ref = refs[:8]
    refs = refs[8:]
    if has_h0:
        h0_ref, y_ref = refs[:2]
        refs = refs[2:]
        hout_ref = None
    else:
        y_ref, hout_ref = refs[:2]
        refs = refs[2:]
        h0_ref = None
    xbc_s, y_s, ac_s, dtv_s, st_s, h_s = refs

    q = SSD_CHUNK
    nc = seq // q
    nh = SSD_HEADS
    hd = SSD_HEAD_DIM
    b_off = SSD_D
    c_off = SSD_D + SSD_GROUPS * SSD_STATE

    row = lax.broadcasted_iota(jnp.int32, (seq, 1), 0)
    for cblk in range(SSD_CONV_DIM // LANES):
        cs = slice(cblk * LANES, (cblk + 1) * LANES)
        xin = zx_ref[:, SSD_D + cblk * LANES:SSD_D + (cblk + 1) * LANES]
        acc = cb_ref[:, cs] + cw_ref[SSD_CONV // 2:SSD_CONV // 2 + 1, cs] * xin
        for kk in range(SSD_CONV):
            d = kk - SSD_CONV // 2
            if d == 0:
                continue
            sh = pltpu.roll(xin, (-d) % seq, axis=0)
            ok = (row + d >= 0) & (row + d < seq)
            acc = acc + cw_ref[kk:kk + 1, cs] * jnp.where(ok, sh, 0.0)
        xbc_s[:, cs] = _silu(acc)

    lane = lax.broadcasted_iota(jnp.int32, (1, LANES), 1)
    xdt = dt_ref[...] + dtb_ref[...]
    dtv = jnp.maximum(xdt, 0.0) + jnp.log1p(jnp.exp(-jnp.abs(xdt)))
    dtv_s[...] = jnp.where(lane < 2 * nh, dtv, 0.0)

    ii = lax.broadcasted_iota(jnp.int32, (q, q), 0)
    jj = lax.broadcasted_iota(jnp.int32, (q, q), 1)
    lower = jj <= ii
    upper = jj >= ii
    is_fwd = lane < nh

    def chunk(c, carry):
        r0 = pl.multiple_of(c * q, q)
        rs = pl.ds(r0, q)
        dt_c = dtv_s[rs, :]
        da_c = dt_c * a_ref[...]
        ac = jnp.where(is_fwd,
                       _dot(lower.astype(F32), da_c, HIGHEST),
                       _dot(upper.astype(F32), da_c, HIGHEST))
        ac_s[rs, :] = ac
        ac_t = ac.T
        dt_t = dt_c.T
        last = jnp.where(is_fwd, ac[q - 1:q, :], ac[0:1, :])
        wend = dt_c * jnp.exp(last - ac)
        for g in range(SSD_GROUPS):
            bg = xbc_s[rs, b_off + g * SSD_STATE:b_off + (g + 1) * SSD_STATE].astype(BF16)
            cg = xbc_s[rs, c_off + g * SSD_STATE:c_off + (g + 1) * SSD_STATE].astype(BF16)
            cb = _dot_nt(cg, bg)
            for hh in range(nh // SSD_GROUPS):
                h = g * (nh // SSD_GROUPS) + hh
                hs = slice(h * hd, (h + 1) * hd)
                seg_f = ac[:, h:h + 1] - ac_t[h:h + 1, :]
                seg_b = ac[:, nh + h:nh + h + 1] - ac_t[nh + h:nh + h + 1, :]
                w = cb * (jnp.exp(jnp.where(lower, seg_f, -jnp.inf)) * dt_t[h:h + 1, :]
                          + jnp.exp(jnp.where(upper, seg_b, -jnp.inf)) * dt_t[nh + h:nh + h + 1, :])
                xh = xbc_s[rs, hs]
                y_s[rs, hs] = _dot(w.astype(BF16), xh.astype(BF16))
                st_s[c * 2 * nh + h] = _dot_tn((xh * wend[:, h:h + 1]).astype(BF16), bg)
                st_s[c * 2 * nh + nh + h] = _dot_tn(
                    (xh * wend[:, nh + h:nh + h + 1]).astype(BF16), bg)
        return carry

    lax.fori_loop(0, nc, chunk, 0)

    if has_h0:
        h_s[...] = h0_ref[...]
    else:
        h_s[...] = jnp.zeros(h_s.shape, F32)

    def carry_states(kstep, carry):
        for direction in range(2):
            c = kstep if direction == 0 else nc - 1 - kstep
            r0 = pl.multiple_of(c * q, q)
            rs = pl.ds(r0, q)
            ac = ac_s[rs, :]
            edge = ac[q - 1:q, :] if direction == 0 else ac[0:1, :]
            for g in range(SSD_GROUPS):
                cg = xbc_s[rs, c_off + g * SSD_STATE:c_off + (g + 1) * SSD_STATE].astype(BF16)
                for hh in range(nh // SSD_GROUPS):
                    h = g * (nh // SSD_GROUPS) + hh
                    hl = direction * nh + h
                    hs = slice(h * hd, (h + 1) * hd)
                    h_in = h_s[hl]
                    y_s[rs, hs] += _dot_nt(cg, h_in.astype(BF16)) * jnp.exp(ac[:, hl:hl + 1])
                    h_s[hl] = jnp.exp(edge[:, hl:hl + 1]) * h_in + st_s[c * 2 * nh + hl]
        return carry

    lax.fori_loop(0, nc, carry_states, 0)

    y = y_s[...] + d_ref[...] * xbc_s[:, 0:SSD_D]
    y = y * _silu(zx_ref[:, 0:SSD_D])
    y_ref[...] = _rms(y, nw_ref[...])
    if hout_ref is not None:
        hout_ref[...] = h_s[...]


def _ssd_call(zx, dt, lw, layer, state, seq):
    has_h0 = state is not None
    rows = zx.shape[0]
    nseq = rows // seq
    nc = seq // SSD_CHUNK
    nst = 2 * SSD_HEADS

    def seqspec(w):
        return pl.BlockSpec((seq, w), lambda b: (b, 0))

    def par(shape):
        return _const_spec(shape, layer, single_buffer=False)

    in_specs = [seqspec(1024), seqspec(LANES), par((8, SSD_CONV_DIM)), par((1, SSD_CONV_DIM)),
                par((1, LANES)), par((1, LANES)), par((1, SSD_D)), par((1, SSD_D))]
    args = [zx, dt, lw['conv_w'], lw['conv_b'], lw['dt_bias'], lw['a'], lw['d'], lw['norm']]
    y_shape = jax.ShapeDtypeStruct((rows, SSD_D), F32)
    if has_h0:
        in_specs.append(pl.BlockSpec((None, None, nst, SSD_HEAD_DIM, SSD_STATE),
                                     lambda b: (b, layer, 0, 0, 0)))
        args.append(state)
        out_shape = y_shape
        out_specs = seqspec(SSD_D)
    else:
        out_shape = [y_shape, jax.ShapeDtypeStruct((nseq, nst, SSD_HEAD_DIM, SSD_STATE), F32)]
        out_specs = [seqspec(SSD_D),
                     pl.BlockSpec((None, nst, SSD_HEAD_DIM, SSD_STATE), lambda b: (b, 0, 0, 0))]
    return pl.pallas_call(
        functools.partial(_ssd_body, seq=seq, has_h0=has_h0),
        out_shape=out_shape,
        grid=(nseq,),
        in_specs=in_specs,
        out_specs=out_specs,
        scratch_shapes=[
            pltpu.VMEM((seq, SSD_CONV_DIM), F32),
            pltpu.VMEM((seq, SSD_D), F32),
            pltpu.VMEM((seq, LANES), F32),
            pltpu.VMEM((seq, LANES), F32),
            pltpu.VMEM((nc * nst, SSD_HEAD_DIM, SSD_STATE), F32),
            pltpu.VMEM((nst, SSD_HEAD_DIM, SSD_STATE), F32),
        ],
        compiler_params=_cparams("parallel"),
        name="ssd_mixer",
    )(*args)


def _s5_tables(a_re, a_im, log_dt, b_re, b_im, c_re, c_im):
    t = S5_T
    gh = S5_HGROUPS
    f32 = lambda x: x.astype(F32)
    lam_r, lam_i = f32(a_re), f32(a_im)
    step = jnp.exp(f32(log_dt))[..., None]
    xr, xi = lam_r * step, lam_i * step

    def powers(ks):
        ks = jnp.asarray(ks, F32)[None, None, :, None, None]
        mag = jnp.exp(xr[:, :, None] * ks)
        return mag * jnp.cos(xi[:, :, None] * ks), mag * jnp.sin(xi[:, :, None] * ks)

    pw_r, pw_i = powers(np.arange(t + 1))
    nr, ni = pw_r[:, :, 1] - 1.0, pw_i[:, :, 1]
    den = lam_r * lam_r + lam_i * lam_i
    fr = (nr * lam_r + ni * lam_i) / den
    fi = (ni * lam_r - nr * lam_i) / den
    br, bi = f32(b_re)[:, None], f32(b_im)[:, None]
    bb_r = fr[..., None] * br - fi[..., None] * bi
    bb_i = fr[..., None] * bi + fi[..., None] * br
    cr, ci = f32(c_re), f32(c_im)

    kern = []
    for direction in range(2):
        pr = pw_r[:, direction, :t][:, :, :, None, :]
        pi = pw_i[:, direction, :t][:, :, :, None, :]
        cpr, cpi = cr[:, None] * pr - ci[:, None] * pi, cr[:, None] * pi + ci[:, None] * pr
        kern.append(jnp.einsum('dtgcn,dgnk->dtgck', cpr, bb_r[:, direction], precision=HIGHEST)
                    - jnp.einsum('dtgcn,dgnk->dtgck', cpi, bb_i[:, direction], precision=HIGHEST))
    kf, kb = kern
    lag = [kb[:, -d] if d < 0 else (kf[:, 0] + kb[:, 0] if d == 0 else kf[:, d])
           for d in range(-(t - 1), t)]
    kt = jnp.stack(lag, axis=1).reshape(DEPTH, S5_LAGS, 2, gh, S5_GROUP_CH, S5_GROUP_CH)
    kt = jnp.transpose(kt, (0, 2, 1, 5, 3, 4)).reshape(DEPTH, 2, S5_LAGS, S5_GROUP_CH, LANES)

    def by_half(x):
        return jnp.transpose(x.reshape(DEPTH, -1, 2, gh, S5_STATE), (0, 2, 1, 3, 4))

    def pick(p, direction, ks):
        return jnp.stack([p[:, direction, k] for k in ks], axis=1)

    bt_r = jnp.transpose(bb_r.reshape(DEPTH, 2, 2, gh, S5_STATE, S5_GROUP_CH), (0, 1, 2, 5, 3, 4))
    bt_i = jnp.transpose(bb_i.reshape(DEPTH, 2, 2, gh, S5_STATE, S5_GROUP_CH), (0, 1, 2, 5, 3, 4))
    sw = []
    for direction, ks in ((0, [t - 1 - s for s in range(t)]), (1, list(range(t)))):
        pr = by_half(pick(pw_r, direction, ks))[:, :, :, None]
        pi = by_half(pick(pw_i, direction, ks))[:, :, :, None]
        wr, wi = bt_r[:, direction][:, :, None], bt_i[:, direction][:, :, None]
        sw += [pr * wr - pi * wi, pr * wi + pi * wr]
    sw = jnp.stack(sw, axis=3).reshape(DEPTH, 2, t, 4, S5_GROUP_CH, gh * S5_STATE)
    ct_r = jnp.transpose(cr.reshape(DEPTH, 2, gh, S5_GROUP_CH, S5_STATE), (0, 1, 4, 2, 3))
    ct_i = jnp.transpose(ci.reshape(DEPTH, 2, gh, S5_GROUP_CH, S5_STATE), (0, 1, 4, 2, 3))
    aw = []
    for direction, ks in ((0, list(range(1, t + 1))), (1, [t - k for k in range(t)])):
        pr = jnp.swapaxes(by_half(pick(pw_r, direction, ks)), 3, 4)[..., None]
        pi = jnp.swapaxes(by_half(pick(pw_i, direction, ks)), 3, 4)[..., None]
        wr, wi = ct_r[:, :, None], ct_i[:, :, None]
        aw += [wr * pr - wi * pi, -(wr * pi + wi * pr)]
    aw = jnp.stack(aw, axis=2).reshape(DEPTH, 2, 4, t, S5_STATE, LANES)

    apow = jnp.stack([pw_r[:, 0, t], pw_i[:, 0, t], pw_r[:, 1, t], pw_i[:, 1, t]], axis=1)
    seg_r, seg_i = powers(t * np.arange(S5_NC + 1))
    back = [S5_NC - 1 - c for c in range(S5_NC)]
    pseg = jnp.stack([seg_r[:, 0, :S5_NC], seg_i[:, 0, :S5_NC],
                      pick(seg_r, 1, back), pick(seg_i, 1, back)], axis=1)
    nkb = S5_CW // LANES
    pseg = jnp.transpose(pseg.reshape(DEPTH, 4, S5_NC, nkb, LANES), (0, 1, 3, 2, 4))
    pseg = pseg.reshape(DEPTH, 4 * nkb, S5_NC, LANES)
    aseg = jnp.stack([seg_r[:, 0, S5_NC], seg_i[:, 0, S5_NC],
                      seg_r[:, 1, S5_NC], seg_i[:, 1, S5_NC]], axis=1)
    return (kt, sw, aw, apow.reshape(DEPTH, 1, S5_SW), pseg, aseg.reshape(DEPTH, 1, S5_SW))


def _s5_expand_operators(kt_ref, sw_ref, aw_ref, toep_s, sop_s, aop_s):
    gh = S5_HGROUPS
    sh_ch = S5_GROUP_CH.bit_length() - 1
    sh_st = S5_STATE.bit_length() - 1

    def group_mask(shape, row_shift, lane_shift):
        r = lax.broadcasted_iota(jnp.int32, shape, 0) >> row_shift
        c = lax.broadcasted_iota(jnp.int32, shape, 1) >> lane_shift
        return r == c

    def blockdiag(x, mask):
        return jnp.where(mask, jnp.concatenate([x] * gh, axis=0), 0.0).astype(BF16)

    m_kk = group_mask((LANES, LANES), sh_ch, sh_ch)
    m_ks = group_mask((LANES, gh * S5_STATE), sh_ch, sh_st)
    m_sk = group_mask((gh * S5_STATE, LANES), sh_st, sh_ch)
    cw = gh * S5_STATE
    for hf in range(2):
        lags = [blockdiag(kt_ref[hf, d], m_kk) for d in range(S5_LAGS)]
        for s in range(S5_T):
            for t in range(S5_T):
                toep_s[hf, s * LANES:(s + 1) * LANES, t * LANES:(t + 1) * LANES] = lags[t - s + S5_T - 1]
            for comp in range(4):
                sop_s[hf, s * LANES:(s + 1) * LANES, comp * cw:(comp + 1) * cw] = blockdiag(
                    sw_ref[hf, s, comp], m_ks)
        for comp in range(4):
            for t in range(S5_T):
                aop_s[hf, comp * cw:(comp + 1) * cw, t * LANES:(t + 1) * LANES] = blockdiag(
                    aw_ref[hf, comp, t], m_sk)


def _s5_body(up_ref, us_ref, kt_ref, sw_ref, aw_ref, at_ref, pseg_ref, aseg_ref, h0_ref,
             y_ref, hout_ref, toep_s, sop_s, aop_s, st_s, hin_s, *, nblk_p, nseg_s):
    i = pl.program_id(0)

    @pl.when(i == 0)
    def _():
        _s5_expand_operators(kt_ref, sw_ref, aw_ref, toep_s, sop_s, aop_s)

    cw = S5_CW
    nkb = cw // LANES
    is_p = i < nblk_p
    for hf in range(2):
        xh = jnp.concatenate(
            [jnp.where(is_p, up_ref[:, s * S5_D + hf * LANES:s * S5_D + (hf + 1) * LANES],
                       us_ref[:, s * S5_D + hf * LANES:s * S5_D + (hf + 1) * LANES])
             for s in range(S5_T)], axis=1).astype(BF16)
        yh = _dot(xh, toep_s[hf])
        for t in range(S5_T):
            y_ref[:, t * S5_D + hf * LANES:t * S5_D + (hf + 1) * LANES] = yh[:, t * LANES:(t + 1) * LANES]
        sh = _dot(xh, sop_s[hf])
        for comp in range(4):
            for k in range(nkb // 2):
                col = (comp * (nkb // 2) + k) * LANES
                st_s[comp * nkb + hf * (nkb // 2) + k] = sh[:, col:col + LANES]

    def load(ref, comp, rows):
        return jnp.concatenate([ref[comp * nkb + k, rows, :] for k in range(nkb)], axis=1)

    def store(ref, comp, rows, val):
        for k in range(nkb):
            ref[comp * nkb + k, rows, :] = val[:, k * LANES:(k + 1) * LANES]

    def cmul(ar, ai, hr, hi):
        return ar * hr - ai * hi, ar * hi + ai * hr

    at = at_ref[...]
    ar_f, ai_f, ar_b, ai_b = [at[:, k * cw:(k + 1) * cw] for k in range(4)]

    def step(c, carry):
        hfr, hfi, hbr, hbi = carry
        rf = pl.ds(c, S5_HB, stride=S5_NC)
        rb = pl.ds(S5_NC - 1 - c, S5_HB, stride=S5_NC)
        for comp, val, rows in ((0, hfr, rf), (1, hfi, rf), (2, hbr, rb), (3, hbi, rb)):
            store(hin_s, comp, rows, val)
        fr, fi = cmul(ar_f, ai_f, hfr, hfi)
        br, bi = cmul(ar_b, ai_b, hbr, hbi)
        return (fr + load(st_s, 0, rf), fi + load(st_s, 1, rf),
                br + load(st_s, 2, rb), bi + load(st_s, 3, rb))

    zero = jnp.zeros((S5_HB, cw), F32)
    fin = lax.fori_loop(0, S5_NC, step, (zero, zero, zero, zero))
    hout_ref[...] = jnp.concatenate(fin, axis=-1)

    @pl.when(i >= nblk_p)
    def _():
        h0 = h0_ref[...]
        aseg = aseg_ref[...]
        sr_f, si_f, sr_b, si_b = [aseg[:, k * cw:(k + 1) * cw] for k in range(4)]
        ent = [[None] * S5_HB for _ in range(4)]
        for s in range(S5_HB // nseg_s):
            hr, hi = h0[s:s + 1, 0:cw], h0[s:s + 1, cw:2 * cw]
            for j in range(nseg_s):
                v = s * nseg_s + j
                ent[0][v], ent[1][v] = hr, hi
                hr, hi = cmul(sr_f, si_f, hr, hi)
                hr, hi = hr + fin[0][v:v + 1], hi + fin[1][v:v + 1]
            hr, hi = h0[s:s + 1, 2 * cw:3 * cw], h0[s:s + 1, 3 * cw:4 * cw]
            for j in range(nseg_s - 1, -1, -1):
                v = s * nseg_s + j
                ent[2][v], ent[3][v] = hr, hi
                hr, hi = cmul(sr_b, si_b, hr, hi)
                hr, hi = hr + fin[2][v:v + 1], hi + fin[3][v:v + 1]
        for direction in range(2):
            for k in range(nkb):
                kr = (2 * direction) * nkb + k
                ki = (2 * direction + 1) * nkb + k
                pr, pi = pseg_ref[kr], pseg_ref[ki]
                for v in range(S5_HB):
                    rows = slice(v * S5_NC, (v + 1) * S5_NC)
                    er = ent[2 * direction][v][:, k * LANES:(k + 1) * LANES]
                    ei = ent[2 * direction + 1][v][:, k * LANES:(k + 1) * LANES]
                    dr, di = cmul(pr, pi, er, ei)
                    hin_s[kr, rows, :] += dr
                    hin_s[ki, rows, :] += di

    for hf in range(2):
        hh = jnp.concatenate(
            [hin_s[comp * nkb + hf * (nkb // 2) + k] for comp in range(4) for k in range(nkb // 2)],
            axis=1).astype(BF16)
        yi = _dot(hh, aop_s[hf])
        for t in range(S5_T):
            y_ref[:, t * S5_D + hf * LANES:t * S5_D + (hf + 1) * LANES] += yi[:, t * LANES:(t + 1) * LANES]


def _s5_call(u_p, u_s, tables, layer, h0, seq_p, seq_s):
    kt, sw, aw, apow, pseg, aseg = tables
    seg_tok = S5_T * S5_NC
    assert seq_p == seg_tok and seq_s % seg_tok == 0 and S5_HB % (seq_s // seg_tok) == 0
    nblk_p = u_p.shape[0] // S5_ROWS
    nblk = nblk_p + u_s.shape[0] // S5_ROWS
    width = S5_T * S5_D

    def par(a):
        return _const_spec(a.shape[1:], layer)

    nlb = S5_SW // LANES
    return pl.pallas_call(
        functools.partial(_s5_body, nblk_p=nblk_p, nseg_s=seq_s // seg_tok),
        out_shape=[jax.ShapeDtypeStruct((nblk * S5_ROWS, width), F32),
                   jax.ShapeDtypeStruct((nblk, S5_HB, S5_SW), F32)],
        grid=(nblk,),
        in_specs=[*_pair_specs(S5_ROWS, width, nblk_p), par(kt), par(sw), par(aw), par(apow),
                  par(pseg), par(aseg),
                  pl.BlockSpec((None, None, S5_HB, S5_SW), lambda i: (layer, i, 0, 0))],
        out_specs=[pl.BlockSpec((S5_ROWS, width), lambda i: (i, 0)),
                   pl.BlockSpec((None, S5_HB, S5_SW), lambda i: (i, 0, 0))],
        scratch_shapes=[pltpu.VMEM((2, S5_KW, S5_KW), BF16), pltpu.VMEM((2, S5_KW, S5_SH), BF16),
                        pltpu.VMEM((2, S5_SH, S5_KW), BF16),
                        pltpu.VMEM((nlb, S5_ROWS, LANES), F32), pltpu.VMEM((nlb, S5_ROWS, LANES), F32)],
        compiler_params=_cparams("arbitrary"),
        name="s5_mixer",
    )(u_p, u_s, kt, sw, aw, apow, pseg, aseg, h0)


def _gelu_tanh(x):
    return 0.5 * x * (1.0 + jnp.tanh(math.sqrt(2.0 / math.pi) * (x + 0.044715 * (x * x * x))))


def _out_body(*refs, final, ntile_p):
    (xa_ref, xb_ref, yssd_a, yssd_b, ynat_a, ynat_b, u_a, u_b, y5_ref, g1_ref, m2_ref, d5_ref,
     wglu_ref, bglu_ref, wout_ref, nm_ref, w1_ref, w2_ref) = refs[:18]
    if final:
        nf_ref, o_ref = refs[18:]
    else:
        (o_ref,) = refs[18:]
    is_p = pl.program_id(0) < ntile_p

    def pick(a_ref, b_ref):
        return jnp.where(is_p, a_ref[...], b_ref[...])

    g = _gelu_tanh(y5_ref[...] + d5_ref[...] * pick(u_a, u_b))
    y5 = g * jax.nn.sigmoid(_dot(g.astype(BF16), wglu_ref[...]) + bglu_ref[...])
    mix = (_dot(pick(yssd_a, yssd_b).astype(BF16), wout_ref[0:SSD_D, :])
           + _dot(y5.astype(BF16), wout_ref[SSD_D:SSD_D + S5_D, :])
           + _dot(pick(ynat_a, ynat_b).astype(BF16), wout_ref[SSD_D + S5_D:, :]))
    x = pick(xa_ref, xb_ref) + g1_ref[...] * mix
    h2 = _rms(x, nm_ref[...]) * (1.0 + m2_ref[:, D_MODEL:2 * D_MODEL]) + m2_ref[:, 0:D_MODEL]
    f = jnp.maximum(_dot(h2.astype(BF16), w1_ref[...]), 0.0)
    f = (f * f).astype(BF16)
    x = x + m2_ref[:, 2 * D_MODEL:3 * D_MODEL] * _dot(f, w2_ref[...])
    if final:
        x = _rms(x, nf_ref[...])
    o_ref[...] = x


def _out_call(x_pair, x_offs, y_ssd, y_nat, u, y5, mods4, lw, norm_f, layer, bp, ntiles, final):
    def cond_idx(i):
        return jnp.where(i < bp, 0, 1 + (i - bp) // 4)

    def pair(width, offs=(0, 0)):
        return list(_pair_specs(TOK_TILE, width, bp, *offs))

    def par(shape):
        return _const_spec(shape, layer)

    in_specs = (
        pair(D_MODEL, x_offs) + pair(SSD_D) + pair(NAT_D) + pair(S5_D)
        + [pl.BlockSpec((TOK_TILE, S5_D), lambda i: (i, 0)),
           pl.BlockSpec((None, None, 1, D_MODEL), lambda i: (layer, cond_idx(i), 0, 2)),
           pl.BlockSpec((None, None, 1, 3 * D_MODEL), lambda i: (layer, cond_idx(i), 0, 1)),
           par((1, S5_D)), par((S5_D, S5_D)), par((1, S5_D)),
           par((D_MODEL, D_MODEL)), par((1, D_MODEL)),
           par((D_MODEL, D_FF)), par((D_FF, D_MODEL))])
    args = [*x_pair, *y_ssd, *y_nat, *u, y5, mods4, mods4, lw['s5_d'], lw['w_glu'], lw['b_glu'],
            lw['w_out'], lw['norm_mlp'], lw['w_ff1'], lw['w_ff2']]
    if final:
        in_specs.append(pl.BlockSpec((1, D_MODEL), lambda i: (0, 0)))
        args.append(norm_f.reshape(1, D_MODEL))
    return pl.pallas_call(
        functools.partial(_out_body, final=final, ntile_p=bp),
        out_shape=jax.ShapeDtypeStruct((ntiles * TOK_TILE, D_MODEL), F32),
        grid=(ntiles,),
        in_specs=in_specs,
        out_specs=pl.BlockSpec((TOK_TILE, D_MODEL), lambda i: (i, 0)),
        compiler_params=_cparams("parallel"),
        name="out_mlp",
    )(*args)


def _pack_w_in(w_in):
    o_dt = SSD_D + SSD_CONV_DIM
    o_u = o_dt + 2 * SSD_HEADS
    pad = IN_PACKED - w_in.shape[-1]
    return jnp.concatenate(
        [w_in[..., :o_dt], w_in[..., o_u:], w_in[..., o_dt:o_u],
         jnp.zeros(w_in.shape[:-1] + (pad,), w_in.dtype)], axis=-1).astype(BF16)


def _lane_pad(x):
    return jnp.pad(x, [(0, 0)] * (x.ndim - 1) + [(0, LANES - x.shape[-1])])


def kernel(x_prompt, x_sample, cache_nat_k, cache_nat_v, state_ssd, state_s5_re, state_s5_im,
           c, c_ctx, w_mod, b_mod, norm_mix, norm_mlp, w_in, ssd_conv_w, ssd_conv_b,
           ssd_dt_bias, ssd_a_log, ssd_d, ssd_norm, s5_a_re, s5_a_im, s5_log_dt,
           s5_b_re, s5_b_im, s5_c_re, s5_c_im, s5_d, s5_w_glu, s5_b_glu, nat_rpb,
           w_out, w_ff1, w_ff2, norm_f):
    bp, seq_p, _ = x_prompt.shape
    bs, seq_s, _ = x_sample.shape
    assert seq_p == TOK_TILE and seq_s == 4 * TOK_TILE
    rows_p = bp * seq_p
    rows_s = bs * seq_s
    nt_s = rows_s // TOK_TILE
    ntiles = bp + nt_s

    ncp = -(-(1 + bs) // 8) * 8
    cond = jnp.concatenate([c_ctx[None, :], c, jnp.zeros((ncp - 1 - bs, D_MODEL), F32)], axis=0)
    mods = _mods_call(cond, w_mod, b_mod)
    mods4 = mods.reshape(DEPTH, ncp, 1, N_MOD * D_MODEL)

    w_in_p = _pack_w_in(w_in)
    ssd_w = {
        'conv_w': jnp.pad(ssd_conv_w.astype(F32), [(0, 0), (0, 8 - SSD_CONV), (0, 0)]),
        'conv_b': ssd_conv_b.astype(F32).reshape(DEPTH, 1, SSD_CONV_DIM),
        'dt_bias': _lane_pad(ssd_dt_bias.astype(F32).reshape(DEPTH, 1, 2 * SSD_HEADS)),
        'a': _lane_pad(-jnp.exp(ssd_a_log.astype(F32)).reshape(DEPTH, 1, 2 * SSD_HEADS)),
        'd': jnp.repeat(ssd_d.astype(F32), SSD_HEAD_DIM, axis=-1).reshape(DEPTH, 1, SSD_D),
        'norm': ssd_norm.astype(F32).reshape(DEPTH, 1, SSD_D),
    }
    out_w = {
        's5_d': s5_d.astype(F32).reshape(DEPTH, 1, S5_D),
        'w_glu': s5_w_glu.astype(BF16),
        'b_glu': s5_b_glu.astype(F32).reshape(DEPTH, 1, S5_D),
        'w_out': w_out.astype(BF16),
        'norm_mlp': norm_mlp.astype(F32).reshape(DEPTH, 1, D_MODEL),
        'w_ff1': w_ff1.astype(BF16),
        'w_ff2': w_ff2.astype(BF16),
    }
    s5_tabs = _s5_tables(s5_a_re, s5_a_im, s5_log_dt, s5_b_re, s5_b_im, s5_c_re, s5_c_im)
    nat_bias = _nat_pair_tiles(nat_rpb)

    cache_k = cache_nat_k.reshape(bs, DEPTH, -1, NAT_D)
    cache_v = cache_nat_v.reshape(bs, DEPTH, -1, NAT_D)
    st_ssd = state_ssd.reshape(bs, DEPTH, 2 * SSD_HEADS, SSD_HEAD_DIM, SSD_STATE)
    s5_tok = S5_ROWS * S5_T
    assert rows_p % s5_tok == 0 and rows_s % s5_tok == 0 and s5_tok % seq_s == 0
    nseq_s = s5_tok // seq_s
    st5 = jnp.stack([state_s5_re[:, :, 0], state_s5_im[:, :, 0],
                     state_s5_re[:, :, 1], state_s5_im[:, :, 1]], axis=2)
    st5 = jnp.transpose(st5.astype(F32).reshape(bs // nseq_s, nseq_s, DEPTH, S5_SW), (2, 0, 1, 3))
    st5 = jnp.pad(st5, [(0, 0), (rows_p // s5_tok, 0), (0, S5_HB - nseq_s), (0, 0)])

    x_pair = (x_prompt.reshape(rows_p, D_MODEL), x_sample.reshape(rows_s, D_MODEL))
    x_offs = (0, 0)
    new_k, new_v, new_ssd, new_s5 = [], [], [], []
    for l in range(DEPTH):
        zx_p, dt_p, u_p, q_p, k_p, v_p = _inproj_call(
            x_pair[0], x_offs[0], bp, lambda i: 0, mods4, norm_mix, w_in_p, l)
        zx_s, dt_s, u_s, q_s, k_s, v_s = _inproj_call(
            x_pair[1], x_offs[1], nt_s, lambda i: 1 + i // 4, mods4, norm_mix, w_in_p, l)
        new_k.append(k_p)
        new_v.append(v_p)

        y_nat = (_ctx_attn_call(q_p, k_p, v_p),
                 _nat_attn_call(q_s, k_s, v_s, cache_k, cache_v, nat_bias, l, bs, seq_s))

        y_ssd_p, ssd_l = _ssd_call(zx_p, dt_p, ssd_w, l, None, seq_p)
        y_ssd_s = _ssd_call(zx_s, dt_s, ssd_w, l, st_ssd, seq_s)
        new_ssd.append(ssd_l)

        y5, s5_l = _s5_call(u_p.reshape(-1, S5_T * S5_D), u_s.reshape(-1, S5_T * S5_D),
                            s5_tabs, l, st5, seq_p, seq_s)
        new_s5.append(s5_l[:rows_p // s5_tok])

        x = _out_call(x_pair, x_offs, (y_ssd_p, y_ssd_s), y_nat, (u_p, u_s),
                      y5.reshape(-1, S5_D), mods4, out_w, norm_f, l, bp, ntiles, l == DEPTH - 1)
        x_pair, x_offs = (x, x), (0, bp)

    y_prompt = x[:rows_p].reshape(bp, seq_p, D_MODEL)
    y_sample = x[rows_p:].reshape(bs, seq_s, D_MODEL)
    out_k = jnp.stack(new_k, axis=0).reshape(DEPTH, bp, seq_p, NAT_HEADS, NAT_HEAD_DIM)
    out_v = jnp.stack(new_v, axis=0).reshape(DEPTH, bp, seq_p, NAT_HEADS, NAT_HEAD_DIM)
    out_k = jnp.transpose(out_k, (1, 0, 2, 3, 4))
    out_v = jnp.transpose(out_v, (1, 0, 2, 3, 4))
    out_ssd = jnp.stack(new_ssd, axis=1).reshape(bp, DEPTH, 2, SSD_HEADS, SSD_HEAD_DIM, SSD_STATE)
    s5 = jnp.stack(new_s5, axis=0)[:, :, :s5_tok // seq_p]
    s5 = s5.reshape(DEPTH, bp, 4, S5_GROUPS, S5_STATE)
    s5 = jnp.transpose(s5, (1, 0, 2, 3, 4))
    out_re = s5[:, :, 0::2]
    out_im = s5[:, :, 1::2]
    return y_prompt, y_sample, out_k, out_v, out_ssd, out_re, out_im
```

```python
import functools
import math

import numpy as np
import jax
import jax.numpy as jnp
from jax import lax
from jax.experimental import pallas as pl
from jax.experimental.pallas import tpu as pltpu

F32 = jnp.float32
BF16 = jnp.bfloat16
HIGHEST = lax.Precision.HIGHEST

D_MODEL = 1024
DEPTH = 4
GRID_W = 64
SSD_HEADS = 6
SSD_HEAD_DIM = 64
SSD_D = SSD_HEADS * SSD_HEAD_DIM
SSD_GROUPS = 2
SSD_STATE = 64
SSD_CONV = 5
SSD_CHUNK = 128
SSD_CONV_DIM = SSD_D + 2 * SSD_GROUPS * SSD_STATE
S5_GROUPS = 16
S5_GROUP_CH = 16
S5_D = S5_GROUPS * S5_GROUP_CH
S5_STATE = 64
NAT_HEADS = 6
NAT_HEAD_DIM = 64
NAT_D = NAT_HEADS * NAT_HEAD_DIM
NAT_KH = 8
NAT_KW = 16
D_FF = 4 * D_MODEL
N_MOD = 6
EPS = 1e-6

LANES = 128
TOK_TILE = 256
IN_PACKED = 2560
S5_T = 8
S5_HGROUPS = LANES // S5_GROUP_CH
S5_KW = S5_T * LANES
S5_CW = S5_GROUPS * S5_STATE
S5_SW = 4 * S5_CW
S5_SH = S5_SW // 2
S5_NC = 32
S5_HB = 8
S5_ROWS = S5_HB * S5_NC
S5_LAGS = 2 * S5_T - 1
VMEM_LIMIT = 56 * 1024 * 1024


def _cparams(*sem):
    return pltpu.CompilerParams(dimension_semantics=sem, vmem_limit_bytes=VMEM_LIMIT)


def _dot(a, b, precision=None):
    return jnp.dot(a, b, preferred_element_type=F32, precision=precision)


def _dot_nt(a, b, precision=None):
    return lax.dot_general(a, b, (((1,), (1,)), ((), ())), preferred_element_type=F32,
                           precision=precision)


def _dot_tn(a, b, precision=None):
    return lax.dot_general(a, b, (((0,), (0,)), ((), ())), preferred_element_type=F32,
                           precision=precision)


def _silu(x):
    return x * jax.nn.sigmoid(x)


def _rms(x, g):
    return x * lax.rsqrt(jnp.mean(x * x, axis=-1, keepdims=True) + EPS) * g


def _const_spec(shape, layer, single_buffer=True):
    return pl.BlockSpec((None,) + tuple(shape), lambda *_: (layer,) + (0,) * len(shape),
                        pipeline_mode=pl.Buffered(1) if single_buffer else None)


def _pair_specs(rows, width, na, off_a=0, off_b=0):
    return (pl.BlockSpec((rows, width), lambda i: (jnp.minimum(i, na - 1) + off_a, 0)),
            pl.BlockSpec((rows, width), lambda i: (jnp.maximum(i - na, 0) + off_b, 0)))


def _mods_body(cond_ref, w_ref, b_ref, o_ref):
    s = _silu(cond_ref[...])
    o_ref[...] = _dot(s.astype(BF16), w_ref[...].astype(BF16)) + b_ref[...]


def _mods_call(cond, w_mod, b_mod):
    ncp = cond.shape[0]
    blk = D_MODEL
    return pl.pallas_call(
        _mods_body,
        out_shape=jax.ShapeDtypeStruct((DEPTH, ncp, N_MOD * D_MODEL), F32),
        grid=(DEPTH, N_MOD),
        in_specs=[
            pl.BlockSpec((ncp, D_MODEL), lambda l, j: (0, 0)),
            pl.BlockSpec((None, D_MODEL, blk), lambda l, j: (l, 0, j)),
            pl.BlockSpec((None, 1, blk), lambda l, j: (l, 0, j)),
        ],
        out_specs=pl.BlockSpec((None, ncp, blk), lambda l, j: (l, 0, j)),
        compiler_params=_cparams("arbitrary", "arbitrary"),
        name="adaln_mods",
    )(cond, w_mod, b_mod.reshape(DEPTH, 1, N_MOD * D_MODEL))


def _inproj_body(x_ref, mod_ref, g_ref, w_ref, zx_ref, dt_ref, u_ref, q_ref, k_ref, v_ref):
    x = x_ref[...]
    h = _rms(x, g_ref[...]) * (1.0 + mod_ref[:, D_MODEL:2 * D_MODEL]) + mod_ref[:, 0:D_MODEL]
    p = _dot(h.astype(BF16), w_ref[...])
    zx_ref[...] = p[:, 0:1024]
    u_ref[...] = p[:, 1024:1280]
    q_ref[...] = p[:, 1280:1664]
    k_ref[...] = p[:, 1664:2048]
    v_ref[...] = p[:, 2048:2432]
    dt_ref[...] = p[:, 2432:2560]


def _inproj_call(x, tile_off, ntiles, cond_idx, mods4, norm_mix, w_in_p, layer):
    rows = ntiles * TOK_TILE

    def tile(w):
        return pl.BlockSpec((TOK_TILE, w), lambda i: (i, 0))

    widths = (1024, LANES, S5_D, NAT_D, NAT_D, NAT_D)
    return pl.pallas_call(
        _inproj_body,
        out_shape=[jax.ShapeDtypeStruct((rows, w), F32) for w in widths],
        grid=(ntiles,),
        in_specs=[
            pl.BlockSpec((TOK_TILE, D_MODEL), lambda i: (i + tile_off, 0)),
            pl.BlockSpec((None, None, 1, 2 * D_MODEL), lambda i: (layer, cond_idx(i), 0, 0)),
            _const_spec((1, D_MODEL), layer, single_buffer=False),
            _const_spec((D_MODEL, IN_PACKED), layer),
        ],
        out_specs=[tile(w) for w in widths],
        compiler_params=_cparams("parallel"),
        name="in_proj",
    )(x, mods4, norm_mix.reshape(DEPTH, 1, D_MODEL), w_in_p)


NAT_SCALE = NAT_HEAD_DIM ** -0.5


def _softmax_pv(scores, values):
    m = functools.reduce(jnp.maximum, [jnp.max(s, axis=-1, keepdims=True) for s in scores])
    ps = [jnp.exp(s - m) for s in scores]
    den = functools.reduce(lambda a, b: a + b, [jnp.sum(p, axis=-1, keepdims=True) for p in ps])
    num = functools.reduce(lambda a, b: a + b,
                           [_dot(p.astype(BF16), v) for p, v in zip(ps, values)])
    return num / den


def _ctx_attn_body(q_ref, k_ref, v_ref, o_ref):
    for h in range(NAT_HEADS):
        hs = slice(h * NAT_HEAD_DIM, (h + 1) * NAT_HEAD_DIM)
        q = (q_ref[:, hs] * NAT_SCALE).astype(BF16)
        k = k_ref[:, hs].astype(BF16)
        v = v_ref[:, hs].astype(BF16)
        o_ref[:, hs] = _softmax_pv([_dot_nt(q, k)], [v])


def _ctx_attn_call(q, k, v):
    rows = q.shape[0]
    spec = pl.BlockSpec((TOK_TILE, NAT_D), lambda b: (b, 0))
    return pl.pallas_call(
        _ctx_attn_body,
        out_shape=jax.ShapeDtypeStruct((rows, NAT_D), F32),
        grid=(rows // TOK_TILE,),
        in_specs=[spec, spec, spec],
        out_specs=spec,
        compiler_params=_cparams("parallel"),
        name="ctx_attn",
    )(q, k, v)


NAT_QROWS = 4
NAT_HEAD_PAIRS = NAT_HEADS // 2
NAT_NPAIR = 2 * NAT_KH


def _nat_window_start(r, rows):
    kh = min(NAT_KH, rows)
    return int(np.clip(r - kh // 2, 0, rows - kh))


def _nat_blocks(rows):
    kh = min(NAT_KH, rows)
    out = []
    for j in range(rows // NAT_QROWS):
        rs = [_nat_window_start(r, rows) for r in range(j * NAT_QROWS, (j + 1) * NAT_QROWS)]
        first = min(rs)
        n = -(-(max(rs) + kh - first) // NAT_QROWS) * NAT_QROWS
        first = min(first, rows - n)
        out.append((first, n))
    return out


def _nat_pair_tiles(rpb):
    cols = np.arange(GRID_W)
    c_start = np.clip(cols - NAT_KW // 2, 0, GRID_W - NAT_KW)
    col_mask = (cols[None, :] >= c_start[:, None]) & (cols[None, :] < c_start[:, None] + NAT_KW)
    idx = cols[None, :] - cols[:, None] + NAT_KW - 1
    sel = (idx[None] == np.arange(2 * NAT_KW - 1)[:, None, None]).astype(np.float32)
    tiles = jnp.einsum('dhab,bqk->dhaqk', rpb.astype(F32), sel, precision=HIGHEST)
    tiles = jnp.where(col_mask, tiles, -jnp.inf)
    neg = jnp.full(tiles.shape[:2] + (1, GRID_W, GRID_W), -jnp.inf, F32)
    ext = jnp.concatenate([neg, tiles, neg], axis=2)
    pairs = jnp.concatenate([ext[:, :, 0:NAT_NPAIR], ext[:, :, 1:NAT_NPAIR + 1]], axis=-1)
    return pairs.reshape(DEPTH, NAT_HEAD_PAIRS, 2, NAT_NPAIR, GRID_W, 2 * GRID_W)


def _nat_block_bias(bias_ref, hh, j, first, n, rows, lo):
    kh = min(NAT_KH, rows)
    neg = jnp.full((GRID_W, 2 * GRID_W), -jnp.inf, F32)
    row_blocks = []
    for ql in range(NAT_QROWS):
        qr = j * NAT_QROWS + ql
        rs = _nat_window_start(qr, rows)
        pieces = []
        for m in range(n // 2):
            k0 = first + 2 * m
            ok0 = rs <= k0 < rs + kh
            ok1 = rs <= k0 + 1 < rs + kh
            if not (ok0 or ok1):
                pieces.append(neg)
                continue
            t = bias_ref[hh, k0 - qr + NAT_KH]
            if ok0 and ok1:
                pieces.append(t)
            elif ok0:
                pieces.append(jnp.where(lo, t, -jnp.inf))
            else:
                pieces.append(jnp.where(lo, -jnp.inf, t))
        row_blocks.append(jnp.concatenate(pieces, axis=1))
    return jnp.concatenate(row_blocks, axis=0)


def _nat_attn_body(q_ref, k_ref, v_ref, kc_ref, vc_ref, bias_ref, o_ref, *, rows):
    lo = lax.broadcasted_iota(jnp.int32, (1, LANES), 1) < NAT_HEAD_DIM
    kc = kc_ref[...].astype(BF16)
    vc = vc_ref[...].astype(BF16)
    nq = NAT_QROWS * GRID_W
    for j, (first, n) in enumerate(_nat_blocks(rows)):
        q = q_ref[j * nq:(j + 1) * nq, :] * NAT_SCALE
        kw = k_ref[first * GRID_W:(first + n) * GRID_W, :].astype(BF16)
        vw = v_ref[first * GRID_W:(first + n) * GRID_W, :].astype(BF16)
        outs = []
        for hh in range(2):
            qm = jnp.where(lo if hh == 0 else jnp.logical_not(lo), q, 0.0).astype(BF16)
            s_lat = _dot_nt(qm, kw) + _nat_block_bias(bias_ref, hh, j, first, n, rows, lo)
            s_ctx = _dot_nt(qm, kc)
            outs.append(_softmax_pv([s_lat, s_ctx], [vw, vc]))
        o_ref[j * nq:(j + 1) * nq, :] = jnp.where(lo, outs[0], outs[1])


def _nat_attn_call(q, k, v, cache_k, cache_v, bias, layer, bs, seq):
    rows = seq // GRID_W
    assert rows % NAT_QROWS == 0 and 2 * GRID_W == LANES
    spec = pl.BlockSpec((seq, LANES), lambda p, b: (b, p))
    cspec = pl.BlockSpec((None, None, cache_k.shape[2], LANES), lambda p, b: (b, layer, 0, p))
    return pl.pallas_call(
        functools.partial(_nat_attn_body, rows=rows),
        out_shape=jax.ShapeDtypeStruct(q.shape, F32),
        grid=(NAT_HEAD_PAIRS, bs),
        in_specs=[spec, spec, spec, cspec, cspec,
                  pl.BlockSpec((None, None) + bias.shape[2:], lambda p, b: (layer, p, 0, 0, 0, 0))],
        out_specs=spec,
        compiler_params=_cparams("parallel", "parallel"),
        name="nat_attn",
    )(q, k, v, cache_k, cache_v, bias)


def _ssd_body(*refs, seq, has_h0):
    zx_ref, dt_ref, cw_ref, cb_ref, dtb_ref, a_ref, d_ref, nw_ref = refs[:8]
    refs = refs[8:]
    if has_h0:
        h0_ref, y_ref = refs[:2]
        refs = refs[2:]
        hout_ref = None
    else:
        y_ref, hout_ref = refs[:2]
        refs = refs[2:]
        h0_ref = None
    xbc_s, y_s, eb_s, dtv_s, st_s, h_s = refs

    q = SSD_CHUNK
    nc = seq // q
    nh = SSD_HEADS
    hd = SSD_HEAD_DIM
    hpg = nh // SSD_GROUPS
    b_off = SSD_D
    c_off = SSD_D + SSD_GROUPS * SSD_STATE
    half = SSD_CONV // 2

    zeros8 = jnp.zeros((8, LANES), F32)
    for cblk in range(SSD_CONV_DIM // LANES):
        cs = slice(cblk * LANES, (cblk + 1) * LANES)
        xcol = slice(SSD_D + cblk * LANES, SSD_D + (cblk + 1) * LANES)
        xin = zx_ref[:, xcol]
        top = jnp.concatenate([zeros8, zx_ref[0:16, xcol]], axis=0)
        bot = jnp.concatenate([zx_ref[seq - 16:seq, xcol], zeros8], axis=0)
        acc = cb_ref[:, cs] + cw_ref[half:half + 1, cs] * xin
        acc_t = cb_ref[:, cs] + cw_ref[half:half + 1, cs] * top[8:16]
        acc_b = cb_ref[:, cs] + cw_ref[half:half + 1, cs] * bot[8:16]
        for kk in range(SSD_CONV):
            d = kk - half
            if d == 0:
                continue
            w = cw_ref[kk:kk + 1, cs]
            acc = acc + w * pltpu.roll(xin, (-d) % seq, axis=0)
            acc_t = acc_t + w * pltpu.roll(top, (-d) % 24, axis=0)[8:16]
            acc_b = acc_b + w * pltpu.roll(bot, (-d) % 24, axis=0)[8:16]
        xbc_s[:, cs] = _silu(acc)
        xbc_s[0:8, cs] = _silu(acc_t)
        xbc_s[seq - 8:seq, cs] = _silu(acc_b)

    lane = lax.broadcasted_iota(jnp.int32, (1, LANES), 1)
    xdt = dt_ref[...] + dtb_ref[...]
    dtv = jnp.maximum(xdt, 0.0) + jnp.log1p(jnp.exp(-jnp.abs(xdt)))
    dtv_s[...] = jnp.where(lane < 2 * nh, dtv, 0.0)

    ii = lax.broadcasted_iota(jnp.int32, (q, q), 0)
    jj = lax.broadcasted_iota(jnp.int32, (q, q), 1)
    lower = jj <= ii
    upper = jj >= ii
    is_fwd = lane < nh
    lo64 = lane < hd
    tri = jnp.concatenate([lower, upper], axis=1).astype(BF16)
    wide = 2 * SSD_D
    ek = lax.broadcasted_iota(jnp.int32, (2 * LANES, wide), 0) & (LANES - 1)
    el = lax.broadcasted_iota(jnp.int32, (2 * LANES, wide), 1) >> (hd.bit_length() - 1)
    expand = (ek == el).astype(BF16)
    grow = lax.broadcasted_iota(jnp.int32, (LANES, SSD_D), 0) >= SSD_STATE
    glane = lax.broadcasted_iota(jnp.int32, (LANES, SSD_D), 1) >= hpg * hd
    own = grow == glane

    def split2(x):
        hi = x.astype(BF16)
        return hi, (x - hi.astype(F32)).astype(BF16)

    def lane_expand(x):
        hi, mid = split2(x)
        return _dot(jnp.concatenate([hi, mid], axis=1), expand)

    def chunk(c, carry):
        r0 = pl.multiple_of(c * q, q)
        rs = pl.ds(r0, q)
        dt_c = dtv_s[rs, :]
        da_c = dt_c * a_ref[...]
        rhs = jnp.concatenate([jnp.where(is_fwd, da_c, 0.0), jnp.where(is_fwd, 0.0, da_c)], axis=0)
        p0, p1 = split2(rhs)
        p2 = (rhs - p0.astype(F32) - p1.astype(F32)).astype(BF16)
        ac = _dot(tri, p0) + _dot(tri, p1) + _dot(tri, p2)
        ac_t = ac.T
        dt_t = dt_c.T
        last = jnp.where(is_fwd, ac[q - 1:q, :], ac[0:1, :])
        wend = dt_c * jnp.exp(last - ac)
        eb_s[rs, :] = lane_expand(jnp.exp(ac))
        xs = xbc_s[rs, 0:SSD_D]
        ball = xbc_s[rs, b_off:c_off].astype(BF16)
        call = xbc_s[rs, c_off:c_off + SSD_GROUPS * SSD_STATE]
        xw = (jnp.concatenate([xs, xs], axis=1) * lane_expand(wend)).astype(BF16)
        st_s[c] = _dot_tn(ball, xw)
        cbs = [_dot_nt(jnp.where(lo64 if g == 0 else jnp.logical_not(lo64), call, 0.0).astype(BF16),
                       ball) for g in range(SSD_GROUPS)]
        for pr in range(nh // 2):
            ws = []
            for h in (2 * pr, 2 * pr + 1):
                seg_f = ac[:, h:h + 1] - ac_t[h:h + 1, :]
                seg_b = ac[:, nh + h:nh + h + 1] - ac_t[nh + h:nh + h + 1, :]
                w = cbs[h // hpg] * (
                    jnp.exp(jnp.where(lower, seg_f, -jnp.inf)) * dt_t[h:h + 1, :]
                    + jnp.exp(jnp.where(upper, seg_b, -jnp.inf)) * dt_t[nh + h:nh + h + 1, :])
                ws.append(w.astype(BF16))
            xp = xs[:, pr * LANES:(pr + 1) * LANES]
            xbd = jnp.concatenate([jnp.where(lo64, xp, 0.0), jnp.where(lo64, 0.0, xp)],
                                  axis=0).astype(BF16)
            y_s[rs, pr * LANES:(pr + 1) * LANES] = _dot(jnp.concatenate(ws, axis=1), xbd)
        return carry

    lax.fori_loop(0, nc, chunk, 0)

    h_s[...] = jnp.zeros(h_s.shape, F32)
    if has_h0:
        for hl in range(2 * nh):
            direction, h = divmod(hl, nh)
            g = h // hpg
            h_s[direction, g * SSD_STATE:(g + 1) * SSD_STATE, h * hd:(h + 1) * hd] = h0_ref[hl]

    def carry_states(kstep, carry):
        for direction in range(2):
            c = kstep if direction == 0 else nc - 1 - kstep
            r0 = pl.multiple_of(c * q, q)
            rs = pl.ds(r0, q)
            ds_ = slice(direction * SSD_D, (direction + 1) * SSD_D)
            edge = r0 + q - 1 if direction == 0 else r0
            call = xbc_s[rs, c_off:c_off + SSD_GROUPS * SSD_STATE].astype(BF16)
            h_in = h_s[direction]
            y_s[rs, :] += _dot(call, jnp.where(own, h_in, 0.0).astype(BF16)) * eb_s[rs, ds_]
            h_s[direction] = eb_s[pl.ds(edge, 1), ds_] * h_in + st_s[c][:, ds_]
        return carry

    lax.fori_loop(0, nc, carry_states, 0)

    y = y_s[...] + d_ref[...] * xbc_s[:, 0:SSD_D]
    y = y * _silu(zx_ref[:, 0:SSD_D])
    y_ref[...] = _rms(y, nw_ref[...])
    if hout_ref is not None:
        for hl in range(2 * nh):
            direction, h = divmod(hl, nh)
            g = h // hpg
            hout_ref[hl] = h_s[direction, g * SSD_STATE:(g + 1) * SSD_STATE, h * hd:(h + 1) * hd]


def _ssd_call(zx, dt, lw, layer, state, seq):
    has_h0 = state is not None
    rows = zx.shape[0]
    nseq = rows // seq
    nc = seq // SSD_CHUNK
    nst = 2 * SSD_HEADS

    def seqspec(w):
        return pl.BlockSpec((seq, w), lambda b: (b, 0))

    def par(shape):
        return _const_spec(shape, layer, single_buffer=False)

    in_specs = [seqspec(1024), seqspec(LANES), par((8, SSD_CONV_DIM)), par((1, SSD_CONV_DIM)),
                par((1, LANES)), par((1, LANES)), par((1, SSD_D)), par((1, SSD_D))]
    args = [zx, dt, lw['conv_w'], lw['conv_b'], lw['dt_bias'], lw['a'], lw['d'], lw['norm']]
    y_shape = jax.ShapeDtypeStruct((rows, SSD_D), F32)
    if has_h0:
        in_specs.append(pl.BlockSpec((None, None, nst, SSD_HEAD_DIM, SSD_STATE),
                                     lambda b: (b, layer, 0, 0, 0)))
        args.append(state)
        out_shape = y_shape
        out_specs = seqspec(SSD_D)
    else:
        out_shape = [y_shape, jax.ShapeDtypeStruct((nseq, nst, SSD_HEAD_DIM, SSD_STATE), F32)]
        out_specs = [seqspec(SSD_D),
                     pl.BlockSpec((None, nst, SSD_HEAD_DIM, SSD_STATE), lambda b: (b, 0, 0, 0))]
    return pl.pallas_call(
        functools.partial(_ssd_body, seq=seq, has_h0=has_h0),
        out_shape=out_shape,
        grid=(nseq,),
        in_specs=in_specs,
        out_specs=out_specs,
        scratch_shapes=[
            pltpu.VMEM((seq, SSD_CONV_DIM), F32),
            pltpu.VMEM((seq, SSD_D), F32),
            pltpu.VMEM((seq, 2 * SSD_D), F32),
            pltpu.VMEM((seq, LANES), F32),
            pltpu.VMEM((nc, SSD_GROUPS * SSD_STATE, 2 * SSD_D), F32),
            pltpu.VMEM((2, SSD_GROUPS * SSD_STATE, SSD_D), F32),
        ],
        compiler_params=_cparams("parallel"),
        name="ssd_mixer",
    )(*args)


def _s5_tables(a_re, a_im, log_dt, b_re, b_im, c_re, c_im):
    t = S5_T
    gh = S5_HGROUPS
    f32 = lambda x: x.astype(F32)
    lam_r, lam_i = f32(a_re), f32(a_im)
    step = jnp.exp(f32(log_dt))[..., None]
    xr, xi = lam_r * step, lam_i * step

    def powers(ks):
        ks = jnp.asarray(ks, F32)[None, None, :, None, None]
        mag = jnp.exp(xr[:, :, None] * ks)
        return mag * jnp.cos(xi[:, :, None] * ks), mag * jnp.sin(xi[:, :, None] * ks)

    pw_r, pw_i = powers(np.arange(t + 1))
    nr, ni = pw_r[:, :, 1] - 1.0, pw_i[:, :, 1]
    den = lam_r * lam_r + lam_i * lam_i
    fr = (nr * lam_r + ni * lam_i) / den
    fi = (ni * lam_r - nr * lam_i) / den
    br, bi = f32(b_re)[:, None], f32(b_im)[:, None]
    bb_r = fr[..., None] * br - fi[..., None] * bi
    bb_i = fr[..., None] * bi + fi[..., None] * br
    cr, ci = f32(c_re), f32(c_im)

    kern = []
    for direction in range(2):
        pr = pw_r[:, direction, :t][:, :, :, None, :]
        pi = pw_i[:, direction, :t][:, :, :, None, :]
        cpr, cpi = cr[:, None] * pr - ci[:, None] * pi, cr[:, None] * pi + ci[:, None] * pr
        kern.append(jnp.einsum('dtgcn,dgnk->dtgck', cpr, bb_r[:, direction], precision=HIGHEST)
                    - jnp.einsum('dtgcn,dgnk->dtgck', cpi, bb_i[:, direction], precision=HIGHEST))
    kf, kb = kern
    lag = [kb[:, -d] if d < 0 else (kf[:, 0] + kb[:, 0] if d == 0 else kf[:, d])
           for d in range(-(t - 1), t)]
    kt = jnp.stack(lag, axis=1).reshape(DEPTH, S5_LAGS, 2, gh, S5_GROUP_CH, S5_GROUP_CH)
    kt = jnp.transpose(kt, (0, 2, 1, 5, 3, 4)).reshape(DEPTH, 2, S5_LAGS, S5_GROUP_CH, LANES)

    def by_half(x):
        return jnp.transpose(x.reshape(DEPTH, -1, 2, gh, S5_STATE), (0, 2, 1, 3, 4))

    def dir_powers(direction, ks):
        pr, pi = powers(np.asarray(ks))
        return pr[:, direction], pi[:, direction]

    bt_r = jnp.transpose(bb_r.reshape(DEPTH, 2, 2, gh, S5_STATE, S5_GROUP_CH), (0, 1, 2, 5, 3, 4))
    bt_i = jnp.transpose(bb_i.reshape(DEPTH, 2, 2, gh, S5_STATE, S5_GROUP_CH), (0, 1, 2, 5, 3, 4))
    sw = []
    for direction, ks in ((0, [t - 1 - s for s in range(t)]), (1, list(range(t)))):
        pr, pi = dir_powers(direction, ks)
        pr = by_half(pr)[:, :, :, None]
        pi = by_half(pi)[:, :, :, None]
        wr, wi = bt_r[:, direction][:, :, None], bt_i[:, direction][:, :, None]
        sw += [pr * wr - pi * wi, pr * wi + pi * wr]
    sw = jnp.stack(sw, axis=3).reshape(DEPTH, 2, t, 4, S5_GROUP_CH, gh * S5_STATE)
    ct_r = jnp.transpose(cr.reshape(DEPTH, 2, gh, S5_GROUP_CH, S5_STATE), (0, 1, 4, 2, 3))
    ct_i = jnp.transpose(ci.reshape(DEPTH, 2, gh, S5_GROUP_CH, S5_STATE), (0, 1, 4, 2, 3))
    aw = []
    for direction, ks in ((0, list(range(1, t + 1))), (1, [t - k for k in range(t)])):
        pr, pi = dir_powers(direction, ks)
        pr = jnp.swapaxes(by_half(pr), 3, 4)[..., None]
        pi = jnp.swapaxes(by_half(pi), 3, 4)[..., None]
        wr, wi = ct_r[:, :, None], ct_i[:, :, None]
        aw += [wr * pr - wi * pi, -(wr * pi + wi * pr)]
    aw = jnp.stack(aw, axis=2).reshape(DEPTH, 2, 4, t, S5_STATE, LANES)

    apow = jnp.stack([pw_r[:, 0, t], pw_i[:, 0, t], pw_r[:, 1, t], pw_i[:, 1, t]], axis=1)
    seg_r, seg_i = powers(t * np.arange(S5_NC + 1))
    back_r, back_i = dir_powers(1, t * (S5_NC - 1 - np.arange(S5_NC)))
    pseg = jnp.stack([seg_r[:, 0, :S5_NC], seg_i[:, 0, :S5_NC], back_r, back_i], axis=1)
    nkb = S5_CW // LANES
    pseg = jnp.transpose(pseg.reshape(DEPTH, 4, S5_NC, nkb, LANES), (0, 1, 3, 2, 4))
    pseg = pseg.reshape(DEPTH, 4 * nkb, S5_NC, LANES)
    aseg = jnp.stack([seg_r[:, 0, S5_NC], seg_i[:, 0, S5_NC],
                      seg_r[:, 1, S5_NC], seg_i[:, 1, S5_NC]], axis=1)
    return (kt, sw, aw, apow.reshape(DEPTH, 1, S5_SW), pseg, aseg.reshape(DEPTH, 1, S5_SW))


def _s5_expand_operators(kt_ref, sw_ref, aw_ref, toep_s, sop_s, aop_s):
    gh = S5_HGROUPS
    sh_ch = S5_GROUP_CH.bit_length() - 1
    sh_st = S5_STATE.bit_length() - 1

    def group_mask(shape, row_shift, lane_shift):
        r = lax.broadcasted_iota(jnp.int32, shape, 0) >> row_shift
        c = lax.broadcasted_iota(jnp.int32, shape, 1) >> lane_shift
        return r == c

    def blockdiag(x, mask):
        return jnp.where(mask, jnp.concatenate([x] * gh, axis=0), 0.0).astype(BF16)

    m_kk = group_mask((LANES, LANES), sh_ch, sh_ch)
    m_ks = group_mask((LANES, gh * S5_STATE), sh_ch, sh_st)
    m_sk = group_mask((gh * S5_STATE, LANES), sh_st, sh_ch)
    cw = gh * S5_STATE
    for hf in range(2):
        lags = [blockdiag(kt_ref[hf, d], m_kk) for d in range(S5_LAGS)]
        for s in range(S5_T):
            for t in range(S5_T):
                toep_s[hf, s * LANES:(s + 1) * LANES, t * LANES:(t + 1) * LANES] = lags[t - s + S5_T - 1]
            for comp in range(4):
                sop_s[hf, s * LANES:(s + 1) * LANES, comp * cw:(comp + 1) * cw] = blockdiag(
                    sw_ref[hf, s, comp], m_ks)
        for comp in range(4):
            for t in range(S5_T):
                aop_s[hf, comp * cw:(comp + 1) * cw, t * LANES:(t + 1) * LANES] = blockdiag(
                    aw_ref[hf, comp, t], m_sk)


def _s5_body(up_ref, us_ref, kt_ref, sw_ref, aw_ref, at_ref, pseg_ref, aseg_ref, h0_ref,
             y_ref, hout_ref, toep_s, sop_s, aop_s, st_s, hin_s, *, nblk_p, nseg_s):
    i = pl.program_id(0)

    @pl.when(i == 0)
    def _():
        _s5_expand_operators(kt_ref, sw_ref, aw_ref, toep_s, sop_s, aop_s)

    cw = S5_CW
    nkb = cw // LANES
    is_p = i < nblk_p
    for hf in range(2):
        xh = jnp.concatenate(
            [jnp.where(is_p, up_ref[:, s * S5_D + hf * LANES:s * S5_D + (hf + 1) * LANES],
                       us_ref[:, s * S5_D + hf * LANES:s * S5_D + (hf + 1) * LANES])
             for s in range(S5_T)], axis=1).astype(BF16)
        yh = _dot(xh, toep_s[hf])
        for t in range(S5_T):
            y_ref[:, t * S5_D + hf * LANES:t * S5_D + (hf + 1) * LANES] = yh[:, t * LANES:(t + 1) * LANES]
        sh = _dot(xh, sop_s[hf])
        for comp in range(4):
            for k in range(nkb // 2):
                col = (comp * (nkb // 2) + k) * LANES
                st_s[comp * nkb + hf * (nkb // 2) + k] = sh[:, col:col + LANES]

    def load(ref, comp, rows):
        return jnp.concatenate([ref[comp * nkb + k, rows, :] for k in range(nkb)], axis=1)

    def store(ref, comp, rows, val):
        for k in range(nkb):
            ref[comp * nkb + k, rows, :] = val[:, k * LANES:(k + 1) * LANES]

    def cmul(ar, ai, hr, hi):
        return ar * hr - ai * hi, ar * hi + ai * hr

    at = at_ref[...]
    ar_f, ai_f, ar_b, ai_b = [at[:, k * cw:(k + 1) * cw] for k in range(4)]

    def step(c, carry):
        hfr, hfi, hbr, hbi = carry
        rf = pl.ds(c, S5_HB, stride=S5_NC)
        rb = pl.ds(S5_NC - 1 - c, S5_HB, stride=S5_NC)
        for comp, val, rows in ((0, hfr, rf), (1, hfi, rf), (2, hbr, rb), (3, hbi, rb)):
            store(hin_s, comp, rows, val)
        fr, fi = cmul(ar_f, ai_f, hfr, hfi)
        br, bi = cmul(ar_b, ai_b, hbr, hbi)
        return (fr + load(st_s, 0, rf), fi + load(st_s, 1, rf),
                br + load(st_s, 2, rb), bi + load(st_s, 3, rb))

    zero = jnp.zeros((S5_HB, cw), F32)
    fin = lax.fori_loop(0, S5_NC, step, (zero, zero, zero, zero))
    hout_ref[...] = jnp.concatenate(fin, axis=-1)

    @pl.when(i >= nblk_p)
    def _():
        h0 = h0_ref[...]
        aseg = aseg_ref[...]
        sr_f, si_f, sr_b, si_b = [aseg[:, k * cw:(k + 1) * cw] for k in range(4)]
        ent = [[None] * S5_HB for _ in range(4)]
        for s in range(S5_HB // nseg_s):
            hr, hi = h0[s:s + 1, 0:cw], h0[s:s + 1, cw:2 * cw]
            for j in range(nseg_s):
                v = s * nseg_s + j
                ent[0][v], ent[1][v] = hr, hi
                hr, hi = cmul(sr_f, si_f, hr, hi)
                hr, hi = hr + fin[0][v:v + 1], hi + fin[1][v:v + 1]
            hr, hi = h0[s:s + 1, 2 * cw:3 * cw], h0[s:s + 1, 3 * cw:4 * cw]
            for j in range(nseg_s - 1, -1, -1):
                v = s * nseg_s + j
                ent[2][v], ent[3][v] = hr, hi
                hr, hi = cmul(sr_b, si_b, hr, hi)
                hr, hi = hr + fin[2][v:v + 1], hi + fin[3][v:v + 1]
        for direction in range(2):
            for k in range(nkb):
                kr = (2 * direction) * nkb + k
                ki = (2 * direction + 1) * nkb + k
                pr, pi = pseg_ref[kr], pseg_ref[ki]
                for v in range(S5_HB):
                    rows = slice(v * S5_NC, (v + 1) * S5_NC)
                    er = ent[2 * direction][v][:, k * LANES:(k + 1) * LANES]
                    ei = ent[2 * direction + 1][v][:, k * LANES:(k + 1) * LANES]
                    dr, di = cmul(pr, pi, er, ei)
                    hin_s[kr, rows, :] += dr
                    hin_s[ki, rows, :] += di

    for hf in range(2):
        hh = jnp.concatenate(
            [hin_s[comp * nkb + hf * (nkb // 2) + k] for comp in range(4) for k in range(nkb // 2)],
            axis=1).astype(BF16)
        yi = _dot(hh, aop_s[hf])
        for t in range(S5_T):
            y_ref[:, t * S5_D + hf * LANES:t * S5_D + (hf + 1) * LANES] += yi[:, t * LANES:(t + 1) * LANES]


def _s5_call(u_p, u_s, tables, layer, h0, seq_p, seq_s):
    kt, sw, aw, apow, pseg, aseg = tables
    seg_tok = S5_T * S5_NC
    assert seq_p == seg_tok and seq_s % seg_tok == 0 and S5_HB % (seq_s // seg_tok) == 0
    nblk_p = u_p.shape[0] // S5_ROWS
    nblk = nblk_p + u_s.shape[0] // S5_ROWS
    width = S5_T * S5_D

    def par(a):
        return _const_spec(a.shape[1:], layer)

    nlb = S5_SW // LANES
    return pl.pallas_call(
        functools.partial(_s5_body, nblk_p=nblk_p, nseg_s=seq_s // seg_tok),
        out_shape=[jax.ShapeDtypeStruct((nblk * S5_ROWS, width), F32),
                   jax.ShapeDtypeStruct((nblk, S5_HB, S5_SW), F32)],
        grid=(nblk,),
        in_specs=[*_pair_specs(S5_ROWS, width, nblk_p), par(kt), par(sw), par(aw), par(apow),
                  par(pseg), par(aseg),
                  pl.BlockSpec((None, None, S5_HB, S5_SW), lambda i: (layer, i, 0, 0))],
        out_specs=[pl.BlockSpec((S5_ROWS, width), lambda i: (i, 0)),
                   pl.BlockSpec((None, S5_HB, S5_SW), lambda i: (i, 0, 0))],
        scratch_shapes=[pltpu.VMEM((2, S5_KW, S5_KW), BF16), pltpu.VMEM((2, S5_KW, S5_SH), BF16),
                        pltpu.VMEM((2, S5_SH, S5_KW), BF16),
                        pltpu.VMEM((nlb, S5_ROWS, LANES), F32), pltpu.VMEM((nlb, S5_ROWS, LANES), F32)],
        compiler_params=_cparams("arbitrary"),
        name="s5_mixer",
    )(u_p, u_s, kt, sw, aw, apow, pseg, aseg, h0)


def _gelu_tanh(x):
    return 0.5 * x * (1.0 + jnp.tanh(math.sqrt(2.0 / math.pi) * (x + 0.044715 * (x * x * x))))


def _out_body(*refs, final, ntile_p):
    (xa_ref, xb_ref, yssd_a, yssd_b, ynat_a, ynat_b, u_a, u_b, y5_ref, g1_ref, m2_ref, d5_ref,
     wglu_ref, bglu_ref, wout_ref, nm_ref, w1_ref, w2_ref) = refs[:18]
    if final:
        nf_ref, o_ref = refs[18:]
    else:
        (o_ref,) = refs[18:]
    is_p = pl.program_id(0) < ntile_p

    def pick(a_ref, b_ref):
        return jnp.where(is_p, a_ref[...], b_ref[...])

    g = _gelu_tanh(y5_ref[...] + d5_ref[...] * pick(u_a, u_b))
    y5 = g * jax.nn.sigmoid(_dot(g.astype(BF16), wglu_ref[...]) + bglu_ref[...])
    mix = (_dot(pick(yssd_a, yssd_b).astype(BF16), wout_ref[0:SSD_D, :])
           + _dot(y5.astype(BF16), wout_ref[SSD_D:SSD_D + S5_D, :])
           + _dot(pick(ynat_a, ynat_b).astype(BF16), wout_ref[SSD_D + S5_D:, :]))
    x = pick(xa_ref, xb_ref) + g1_ref[...] * mix
    h2 = _rms(x, nm_ref[...]) * (1.0 + m2_ref[:, D_MODEL:2 * D_MODEL]) + m2_ref[:, 0:D_MODEL]
    f = jnp.maximum(_dot(h2.astype(BF16), w1_ref[...]), 0.0)
    f = (f * f).astype(BF16)
    x = x + m2_ref[:, 2 * D_MODEL:3 * D_MODEL] * _dot(f, w2_ref[...])
    if final:
        x = _rms(x, nf_ref[...])
    o_ref[...] = x


def _out_call(x_pair, x_offs, y_ssd, y_nat, u, y5, mods4, lw, norm_f, layer, bp, ntiles, final):
    def cond_idx(i):
        return jnp.where(i < bp, 0, 1 + (i - bp) // 4)

    def pair(width, offs=(0, 0)):
        return list(_pair_specs(TOK_TILE, width, bp, *offs))

    def par(shape):
        return _const_spec(shape, layer)

    in_specs = (
        pair(D_MODEL, x_offs) + pair(SSD_D) + pair(NAT_D) + pair(S5_D)
        + [pl.BlockSpec((TOK_TILE, S5_D), lambda i: (i, 0)),
           pl.BlockSpec((None, None, 1, D_MODEL), lambda i: (layer, cond_idx(i), 0, 2)),
           pl.BlockSpec((None, None, 1, 3 * D_MODEL), lambda i: (layer, cond_idx(i), 0, 1)),
           par((1, S5_D)), par((S5_D, S5_D)), par((1, S5_D)),
           par((D_MODEL, D_MODEL)), par((1, D_MODEL)),
           par((D_MODEL, D_FF)), par((D_FF, D_MODEL))])
    args = [*x_pair, *y_ssd, *y_nat, *u, y5, mods4, mods4, lw['s5_d'], lw['w_glu'], lw['b_glu'],
            lw['w_out'], lw['norm_mlp'], lw['w_ff1'], lw['w_ff2']]
    if final:
        in_specs.append(pl.BlockSpec((1, D_MODEL), lambda i: (0, 0)))
        args.append(norm_f.reshape(1, D_MODEL))
    return pl.pallas_call(
        functools.partial(_out_body, final=final, ntile_p=bp),
        out_shape=jax.ShapeDtypeStruct((ntiles * TOK_TILE, D_MODEL), F32),
        grid=(ntiles,),
        in_specs=in_specs,
        out_specs=pl.BlockSpec((TOK_TILE, D_MODEL), lambda i: (i, 0)),
        compiler_params=_cparams("parallel"),
        name="out_mlp",
    )(*args)


def _pack_w_in(w_in):
    o_dt = SSD_D + SSD_CONV_DIM
    o_u = o_dt + 2 * SSD_HEADS
    pad = IN_PACKED - w_in.shape[-1]
    return jnp.concatenate(
        [w_in[..., :o_dt], w_in[..., o_u:], w_in[..., o_dt:o_u],
         jnp.zeros(w_in.shape[:-1] + (pad,), w_in.dtype)], axis=-1).astype(BF16)


def _lane_pad(x):
    return jnp.pad(x, [(0, 0)] * (x.ndim - 1) + [(0, LANES - x.shape[-1])])


def kernel(x_prompt, x_sample, cache_nat_k, cache_nat_v, state_ssd, state_s5_re, state_s5_im,
           c, c_ctx, w_mod, b_mod, norm_mix, norm_mlp, w_in, ssd_conv_w, ssd_conv_b,
           ssd_dt_bias, ssd_a_log, ssd_d, ssd_norm, s5_a_re, s5_a_im, s5_log_dt,
           s5_b_re, s5_b_im, s5_c_re, s5_c_im, s5_d, s5_w_glu, s5_b_glu, nat_rpb,
           w_out, w_ff1, w_ff2, norm_f):
    bp, seq_p, _ = x_prompt.shape
    bs, seq_s, _ = x_sample.shape
    assert seq_p == TOK_TILE and seq_s == 4 * TOK_TILE
    rows_p = bp * seq_p
    rows_s = bs * seq_s
    nt_s = rows_s // TOK_TILE
    ntiles = bp + nt_s

    ncp = -(-(1 + bs) // 8) * 8
    cond = jnp.concatenate([c_ctx[None, :], c, jnp.zeros((ncp - 1 - bs, D_MODEL), F32)], axis=0)
    mods = _mods_call(cond, w_mod, b_mod)
    mods4 = mods.reshape(DEPTH, ncp, 1, N_MOD * D_MODEL)

    w_in_p = _pack_w_in(w_in)
    ssd_w = {
        'conv_w': jnp.pad(ssd_conv_w.astype(F32), [(0, 0), (0, 8 - SSD_CONV), (0, 0)]),
        'conv_b': ssd_conv_b.astype(F32).reshape(DEPTH, 1, SSD_CONV_DIM),
        'dt_bias': _lane_pad(ssd_dt_bias.astype(F32).reshape(DEPTH, 1, 2 * SSD_HEADS)),
        'a': _lane_pad(-jnp.exp(ssd_a_log.astype(F32)).reshape(DEPTH, 1, 2 * SSD_HEADS)),
        'd': jnp.repeat(ssd_d.astype(F32), SSD_HEAD_DIM, axis=-1).reshape(DEPTH, 1, SSD_D),
        'norm': ssd_norm.astype(F32).reshape(DEPTH, 1, SSD_D),
    }
    out_w = {
        's5_d': s5_d.astype(F32).reshape(DEPTH, 1, S5_D),
        'w_glu': s5_w_glu.astype(BF16),
        'b_glu': s5_b_glu.astype(F32).reshape(DEPTH, 1, S5_D),
        'w_out': w_out.astype(BF16),
        'norm_mlp': norm_mlp.astype(F32).reshape(DEPTH, 1, D_MODEL),
        'w_ff1': w_ff1.astype(BF16),
        'w_ff2': w_ff2.astype(BF16),
    }
    s5_tabs = _s5_tables(s5_a_re, s5_a_im, s5_log_dt, s5_b_re, s5_b_im, s5_c_re, s5_c_im)
    nat_bias = _nat_pair_tiles(nat_rpb)

    cache_k = cache_nat_k.reshape(bs, DEPTH, -1, NAT_D)
    cache_v = cache_nat_v.reshape(bs, DEPTH, -1, NAT_D)
    st_ssd = jnp.swapaxes(state_ssd, -1, -2).reshape(bs, DEPTH, 2 * SSD_HEADS, SSD_STATE, SSD_HEAD_DIM)
    s5_tok = S5_ROWS * S5_T
    assert rows_p % s5_tok == 0 and rows_s % s5_tok == 0 and s5_tok % seq_s == 0
    nseq_s = s5_tok // seq_s
    st5 = jnp.stack([state_s5_re[:, :, 0], state_s5_im[:, :, 0],
                     state_s5_re[:, :, 1], state_s5_im[:, :, 1]], axis=2)
    st5 = jnp.transpose(st5.astype(F32).reshape(bs // nseq_s, nseq_s, DEPTH, S5_SW), (2, 0, 1, 3))
    st5 = jnp.pad(st5, [(0, 0), (rows_p // s5_tok, 0), (0, S5_HB - nseq_s), (0, 0)])

    x_pair = (x_prompt.reshape(rows_p, D_MODEL), x_sample.reshape(rows_s, D_MODEL))
    x_offs = (0, 0)
    new_k, new_v, new_ssd, new_s5 = [], [], [], []
    for l in range(DEPTH):
        zx_p, dt_p, u_p, q_p, k_p, v_p = _inproj_call(
            x_pair[0], x_offs[0], bp, lambda i: 0, mods4, norm_mix, w_in_p, l)
        zx_s, dt_s, u_s, q_s, k_s, v_s = _inproj_call(
            x_pair[1], x_offs[1], nt_s, lambda i: 1 + i // 4, mods4, norm_mix, w_in_p, l)
        new_k.append(k_p)
        new_v.append(v_p)

        y_nat = (_ctx_attn_call(q_p, k_p, v_p),
                 _nat_attn_call(q_s, k_s, v_s, cache_k, cache_v, nat_bias, l, bs, seq_s))

        y_ssd_p, ssd_l = _ssd_call(zx_p, dt_p, ssd_w, l, None, seq_p)
        y_ssd_s = _ssd_call(zx_s, dt_s, ssd_w, l, st_ssd, seq_s)
        new_ssd.append(ssd_l)

        y5, s5_l = _s5_call(u_p.reshape(-1, S5_T * S5_D), u_s.reshape(-1, S5_T * S5_D),
                            s5_tabs, l, st5, seq_p, seq_s)
        new_s5.append(s5_l[:rows_p // s5_tok])

        x = _out_call(x_pair, x_offs, (y_ssd_p, y_ssd_s), y_nat, (u_p, u_s),
                      y5.reshape(-1, S5_D), mods4, out_w, norm_f, l, bp, ntiles, l == DEPTH - 1)
        x_pair, x_offs = (x, x), (0, bp)

    y_prompt = x[:rows_p].reshape(bp, seq_p, D_MODEL)
    y_sample = x[rows_p:].reshape(bs, seq_s, D_MODEL)
    out_k = jnp.stack(new_k, axis=0).reshape(DEPTH, bp, seq_p, NAT_HEADS, NAT_HEAD_DIM)
    out_v = jnp.stack(new_v, axis=0).reshape(DEPTH, bp, seq_p, NAT_HEADS, NAT_HEAD_DIM)
    out_k = jnp.transpose(out_k, (1, 0, 2, 3, 4))
    out_v = jnp.transpose(out_v, (1, 0, 2, 3, 4))
    out_ssd = jnp.swapaxes(jnp.stack(new_ssd, axis=1), -1, -2)
    out_ssd = out_ssd.reshape(bp, DEPTH, 2, SSD_HEADS, SSD_HEAD_DIM, SSD_STATE)
    s5 = jnp.stack(new_s5, axis=0)[:, :, :s5_tok // seq_p]
    s5 = s5.reshape(DEPTH, bp, 4, S5_GROUPS, S5_STATE)
    s5 = jnp.transpose(s5, (1, 0, 2, 3, 4))
    out_re = s5[:, :, 0::2]
    out_im = s5[:, :, 1::2]
    return y_prompt, y_sample, out_k, out_v, out_ssd, out_re, out_im
```

```python
import functools
import math

import numpy as np
import jax
import jax.numpy as jnp
from jax import lax
from jax.experimental import pallas as pl
from jax.experimental.pallas import tpu as pltpu

F32 = jnp.float32
BF16 = jnp.bfloat16
HIGHEST = lax.Precision.HIGHEST

D_MODEL = 1024
DEPTH = 4
GRID_W = 64
SSD_HEADS = 6
SSD_HEAD_DIM = 64
SSD_D = SSD_HEADS * SSD_HEAD_DIM
SSD_GROUPS = 2
SSD_STATE = 64
SSD_CONV = 5
SSD_CHUNK = 128
SSD_CONV_DIM = SSD_D + 2 * SSD_GROUPS * SSD_STATE
S5_GROUPS = 16
S5_GROUP_CH = 16
S5_D = S5_GROUPS * S5_GROUP_CH
S5_STATE = 64
NAT_HEADS = 6
NAT_HEAD_DIM = 64
NAT_D = NAT_HEADS * NAT_HEAD_DIM
NAT_KH = 8
NAT_KW = 16
D_FF = 4 * D_MODEL
N_MOD = 6
EPS = 1e-6

LANES = 128
TOK_TILE = 256
ROW_TILE = 512
IN_PACKED = 2560
S5_T = 8
S5_HGROUPS = LANES // S5_GROUP_CH
S5_KW = S5_T * LANES
S5_CW = S5_GROUPS * S5_STATE
S5_SW = 4 * S5_CW
S5_SH = S5_SW // 2
S5_NC = 32
S5_HB = 8
S5_ROWS = S5_HB * S5_NC
S5_LAGS = 2 * S5_T - 1
VMEM_LIMIT = 56 * 1024 * 1024


def _cparams(*sem):
    return pltpu.CompilerParams(dimension_semantics=sem, vmem_limit_bytes=VMEM_LIMIT)


def _dot(a, b, precision=None):
    return jnp.dot(a, b, preferred_element_type=F32, precision=precision)


def _dot_nt(a, b, precision=None):
    return lax.dot_general(a, b, (((1,), (1,)), ((), ())), preferred_element_type=F32,
                           precision=precision)


def _dot_tn(a, b, precision=None):
    return lax.dot_general(a, b, (((0,), (0,)), ((), ())), preferred_element_type=F32,
                           precision=precision)


def _silu(x):
    hx = 0.5 * x
    return hx + hx * jnp.tanh(hx)


def _rms(x, g):
    return x * lax.rsqrt(jnp.mean(x * x, axis=-1, keepdims=True) + EPS) * g


def _const_spec(shape, layer, single_buffer=True):
    return pl.BlockSpec((None,) + tuple(shape), lambda *_: (layer,) + (0,) * len(shape),
                        pipeline_mode=pl.Buffered(1) if single_buffer else None)


def _mods_body(cond_ref, w_ref, b_ref, o_ref):
    s = _silu(cond_ref[...])
    o_ref[...] = _dot(s.astype(BF16), w_ref[...].astype(BF16)) + b_ref[...]


def _mods_call(cond, w_mod, b_mod):
    ncp = cond.shape[0]
    blk = D_MODEL
    return pl.pallas_call(
        _mods_body,
        out_shape=jax.ShapeDtypeStruct((DEPTH, ncp, N_MOD * D_MODEL), F32),
        grid=(DEPTH, N_MOD),
        in_specs=[
            pl.BlockSpec((ncp, D_MODEL), lambda l, j: (0, 0)),
            pl.BlockSpec((None, D_MODEL, blk), lambda l, j: (l, 0, j)),
            pl.BlockSpec((None, 1, blk), lambda l, j: (l, 0, j)),
        ],
        out_specs=pl.BlockSpec((None, ncp, blk), lambda l, j: (l, 0, j)),
        compiler_params=_cparams("arbitrary", "arbitrary"),
        name="adaln_mods",
    )(cond, w_mod, b_mod.reshape(DEPTH, 1, N_MOD * D_MODEL))


def _inproj_body(x_ref, mod_ref, g_ref, w_ref, zx_ref, dt_ref, u_ref, q_ref, k_ref, v_ref):
    x = x_ref[...]
    h = _rms(x, g_ref[...]) * (1.0 + mod_ref[:, D_MODEL:2 * D_MODEL]) + mod_ref[:, 0:D_MODEL]
    p = _dot(h.astype(BF16), w_ref[...])
    zx_ref[...] = p[:, 0:1024]
    for hf in range(S5_D // LANES):
        u_ref[hf] = p[:, 1024 + hf * LANES:1024 + (hf + 1) * LANES]
    q_ref[...] = p[:, 1280:1664]
    k_ref[...] = p[:, 1664:2048]
    v_ref[...] = p[:, 2048:2432]
    dt_ref[...] = p[:, 2432:2560]


def _inproj_call(x, cond_idx, mods4, norm_mix, w_in_p, layer):
    rows = x.shape[0]

    def tile(w):
        return pl.BlockSpec((ROW_TILE, w), lambda i: (i, 0))

    def out(w):
        return jax.ShapeDtypeStruct((rows, w), F32)

    uh = S5_D // LANES
    return pl.pallas_call(
        _inproj_body,
        out_shape=[out(1024), out(LANES), jax.ShapeDtypeStruct((uh, rows, LANES), F32),
                   out(NAT_D), out(NAT_D), out(NAT_D)],
        grid=(rows // ROW_TILE,),
        in_specs=[
            tile(D_MODEL),
            pl.BlockSpec((None, None, 1, 2 * D_MODEL), lambda i: (layer, cond_idx(i), 0, 0)),
            _const_spec((1, D_MODEL), layer, single_buffer=False),
            _const_spec((D_MODEL, IN_PACKED), layer),
        ],
        out_specs=[tile(1024), tile(LANES), pl.BlockSpec((uh, ROW_TILE, LANES), lambda i: (0, i, 0)),
                   tile(NAT_D), tile(NAT_D), tile(NAT_D)],
        compiler_params=_cparams("parallel"),
        name="in_proj",
    )(x, mods4, norm_mix.reshape(DEPTH, 1, D_MODEL), w_in_p)


NAT_SCALE = NAT_HEAD_DIM ** -0.5


def _softmax_pv(scores, values):
    m = functools.reduce(jnp.maximum, [jnp.max(s, axis=-1, keepdims=True) for s in scores])
    ps = [jnp.exp(s - m) for s in scores]
    den = functools.reduce(lambda a, b: a + b, [jnp.sum(p, axis=-1, keepdims=True) for p in ps])
    num = functools.reduce(lambda a, b: a + b,
                           [_dot(p.astype(BF16), v) for p, v in zip(ps, values)])
    return num / den


def _ctx_attn_body(q_ref, k_ref, v_ref, o_ref):
    for h in range(NAT_HEADS):
        hs = slice(h * NAT_HEAD_DIM, (h + 1) * NAT_HEAD_DIM)
        q = (q_ref[:, hs] * NAT_SCALE).astype(BF16)
        k = k_ref[:, hs].astype(BF16)
        v = v_ref[:, hs].astype(BF16)
        o_ref[:, hs] = _softmax_pv([_dot_nt(q, k)], [v])


def _ctx_attn_call(q, k, v):
    rows = q.shape[0]
    spec = pl.BlockSpec((TOK_TILE, NAT_D), lambda b: (b, 0))
    return pl.pallas_call(
        _ctx_attn_body,
        out_shape=jax.ShapeDtypeStruct((rows, NAT_D), F32),
        grid=(rows // TOK_TILE,),
        in_specs=[spec, spec, spec],
        out_specs=spec,
        compiler_params=_cparams("parallel"),
        name="ctx_attn",
    )(q, k, v)


NAT_QROWS = 4
NAT_HEAD_PAIRS = NAT_HEADS // 2
NAT_NPAIR = 2 * NAT_KH


def _nat_window_start(r, rows):
    kh = min(NAT_KH, rows)
    return int(np.clip(r - kh // 2, 0, rows - kh))


def _nat_blocks(rows):
    kh = min(NAT_KH, rows)
    out = []
    for j in range(rows // NAT_QROWS):
        rs = [_nat_window_start(r, rows) for r in range(j * NAT_QROWS, (j + 1) * NAT_QROWS)]
        first = min(rs)
        n = -(-(max(rs) + kh - first) // NAT_QROWS) * NAT_QROWS
        first = min(first, rows - n)
        out.append((first, n))
    return out


def _nat_pair_tiles(rpb):
    cols = np.arange(GRID_W)
    c_start = np.clip(cols - NAT_KW // 2, 0, GRID_W - NAT_KW)
    col_mask = (cols[None, :] >= c_start[:, None]) & (cols[None, :] < c_start[:, None] + NAT_KW)
    idx = cols[None, :] - cols[:, None] + NAT_KW - 1
    sel = (idx[None] == np.arange(2 * NAT_KW - 1)[:, None, None]).astype(np.float32)
    tiles = jnp.einsum('dhab,bqk->dhaqk', rpb.astype(F32), sel, precision=HIGHEST)
    tiles = jnp.where(col_mask, tiles, -jnp.inf)
    neg = jnp.full(tiles.shape[:2] + (1, GRID_W, GRID_W), -jnp.inf, F32)
    ext = jnp.concatenate([neg, tiles, neg], axis=2)
    pairs = jnp.concatenate([ext[:, :, 0:NAT_NPAIR], ext[:, :, 1:NAT_NPAIR + 1]], axis=-1)
    return pairs.reshape(DEPTH, NAT_HEAD_PAIRS, 2, NAT_NPAIR, GRID_W, 2 * GRID_W)


def _nat_block_bias(bias_ref, hh, j, first, n, rows, lo):
    kh = min(NAT_KH, rows)
    neg = jnp.full((GRID_W, 2 * GRID_W), -jnp.inf, F32)
    row_blocks = []
    for ql in range(NAT_QROWS):
        qr = j * NAT_QROWS + ql
        rs = _nat_window_start(qr, rows)
        pieces = []
        for m in range(n // 2):
            k0 = first + 2 * m
            ok0 = rs <= k0 < rs + kh
            ok1 = rs <= k0 + 1 < rs + kh
            if not (ok0 or ok1):
                pieces.append(neg)
                continue
            t = bias_ref[hh, k0 - qr + NAT_KH]
            if ok0 and ok1:
                pieces.append(t)
            elif ok0:
                pieces.append(jnp.where(lo, t, -jnp.inf))
            else:
                pieces.append(jnp.where(lo, -jnp.inf, t))
        row_blocks.append(jnp.concatenate(pieces, axis=1))
    return jnp.concatenate(row_blocks, axis=0)


def _nat_attn_body(q_ref, k_ref, v_ref, kc_ref, vc_ref, bias_ref, o_ref, *, rows):
    lo = lax.broadcasted_iota(jnp.int32, (1, LANES), 1) < NAT_HEAD_DIM
    kc = kc_ref[...].astype(BF16)
    vc = vc_ref[...].astype(BF16)
    nq = NAT_QROWS * GRID_W
    for j, (first, n) in enumerate(_nat_blocks(rows)):
        q = q_ref[j * nq:(j + 1) * nq, :] * NAT_SCALE
        kw = k_ref[first * GRID_W:(first + n) * GRID_W, :].astype(BF16)
        vw = v_ref[first * GRID_W:(first + n) * GRID_W, :].astype(BF16)
        outs = []
        for hh in range(2):
            qm = jnp.where(lo if hh == 0 else jnp.logical_not(lo), q, 0.0).astype(BF16)
            s_lat = _dot_nt(qm, kw) + _nat_block_bias(bias_ref, hh, j, first, n, rows, lo)
            s_ctx = _dot_nt(qm, kc)
            outs.append(_softmax_pv([s_lat, s_ctx], [vw, vc]))
        o_ref[j * nq:(j + 1) * nq, :] = jnp.where(lo, outs[0], outs[1])


def _nat_attn_call(q, k, v, cache_k, cache_v, bias, layer, bs, seq):
    rows = seq // GRID_W
    assert rows % NAT_QROWS == 0 and 2 * GRID_W == LANES
    spec = pl.BlockSpec((seq, LANES), lambda p, b: (b, p))
    cspec = pl.BlockSpec((None, None, cache_k.shape[2], LANES), lambda p, b: (b, layer, 0, p))
    return pl.pallas_call(
        functools.partial(_nat_attn_body, rows=rows),
        out_shape=jax.ShapeDtypeStruct(q.shape, F32),
        grid=(NAT_HEAD_PAIRS, bs),
        in_specs=[spec, spec, spec, cspec, cspec,
                  pl.BlockSpec((None, None) + bias.shape[2:], lambda p, b: (layer, p, 0, 0, 0, 0))],
        out_specs=spec,
        compiler_params=_cparams("parallel", "parallel"),
        name="nat_attn",
    )(q, k, v, cache_k, cache_v, bias)


def _ssd_body(*refs, seq, has_h0):
    zx_ref, dt_ref, cw_ref, cb_ref, dtb_ref, a_ref, d_ref, nw_ref = refs[:8]
    refs = refs[8:]
    if has_h0:
        h0_ref, y_ref = refs[:2]
        refs = refs[2:]
        hout_ref = None
    else:
        y_ref, hout_ref = refs[:2]
        refs = refs[2:]
        h0_ref = None
    xbc_s, y_s, eb_s, dtv_s, st_s, h_s = refs

    q = SSD_CHUNK
    nc = seq // q
    nh = SSD_HEADS
    hd = SSD_HEAD_DIM
    hpg = nh // SSD_GROUPS
    b_off = SSD_D
    c_off = SSD_D + SSD_GROUPS * SSD_STATE
    half = SSD_CONV // 2

    zeros8 = jnp.zeros((8, LANES), F32)
    for cblk in range(SSD_CONV_DIM // LANES):
        cs = slice(cblk * LANES, (cblk + 1) * LANES)
        xcol = slice(SSD_D + cblk * LANES, SSD_D + (cblk + 1) * LANES)
        xin = zx_ref[:, xcol]
        top = jnp.concatenate([zeros8, zx_ref[0:16, xcol]], axis=0)
        bot = jnp.concatenate([zx_ref[seq - 16:seq, xcol], zeros8], axis=0)
        acc = cb_ref[:, cs] + cw_ref[half:half + 1, cs] * xin
        acc_t = cb_ref[:, cs] + cw_ref[half:half + 1, cs] * top[8:16]
        acc_b = cb_ref[:, cs] + cw_ref[half:half + 1, cs] * bot[8:16]
        for kk in range(SSD_CONV):
            d = kk - half
            if d == 0:
                continue
            w = cw_ref[kk:kk + 1, cs]
            acc = acc + w * pltpu.roll(xin, (-d) % seq, axis=0)
            acc_t = acc_t + w * pltpu.roll(top, (-d) % 24, axis=0)[8:16]
            acc_b = acc_b + w * pltpu.roll(bot, (-d) % 24, axis=0)[8:16]
        xbc_s[:, cs] = _silu(acc)
        xbc_s[0:8, cs] = _silu(acc_t)
        xbc_s[seq - 8:seq, cs] = _silu(acc_b)

    lane = lax.broadcasted_iota(jnp.int32, (1, LANES), 1)
    xdt = dt_ref[...] + dtb_ref[...]
    dtv = jnp.maximum(xdt, 0.0) + jnp.log1p(jnp.exp(-jnp.abs(xdt)))
    dtv_s[...] = jnp.where(lane < 2 * nh, dtv, 0.0)

    ii = lax.broadcasted_iota(jnp.int32, (q, q), 0)
    jj = lax.broadcasted_iota(jnp.int32, (q, q), 1)
    lower = jj <= ii
    upper = jj >= ii
    is_fwd = lane < nh
    lo64 = lane < hd
    tri = jnp.concatenate([lower, upper], axis=1).astype(BF16)
    wide = 2 * SSD_D
    ek = lax.broadcasted_iota(jnp.int32, (2 * LANES, wide), 0) & (LANES - 1)
    el = lax.broadcasted_iota(jnp.int32, (2 * LANES, wide), 1) >> (hd.bit_length() - 1)
    expand = (ek == el).astype(BF16)
    grow = lax.broadcasted_iota(jnp.int32, (LANES, SSD_D), 0) >= SSD_STATE
    glane = lax.broadcasted_iota(jnp.int32, (LANES, SSD_D), 1) >= hpg * hd
    own = grow == glane

    def split2(x):
        hi = x.astype(BF16)
        return hi, (x - hi.astype(F32)).astype(BF16)

    def lane_expand(x):
        hi, mid = split2(x)
        return _dot(jnp.concatenate([hi, mid], axis=1), expand)

    def chunk(c, carry):
        r0 = pl.multiple_of(c * q, q)
        rs = pl.ds(r0, q)
        dt_c = dtv_s[rs, :]
        da_c = dt_c * a_ref[...]
        rhs = jnp.concatenate([jnp.where(is_fwd, da_c, 0.0), jnp.where(is_fwd, 0.0, da_c)], axis=0)
        p0, p1 = split2(rhs)
        p2 = (rhs - p0.astype(F32) - p1.astype(F32)).astype(BF16)
        ac = _dot(tri, p0) + _dot(tri, p1) + _dot(tri, p2)
        ac_t = ac.T
        dt_t = dt_c.T
        last = jnp.where(is_fwd, ac[q - 1:q, :], ac[0:1, :])
        wend = dt_c * jnp.exp(last - ac)
        eb_s[rs, :] = lane_expand(jnp.exp(ac))
        xs = xbc_s[rs, 0:SSD_D]
        ball = xbc_s[rs, b_off:c_off].astype(BF16)
        call = xbc_s[rs, c_off:c_off + SSD_GROUPS * SSD_STATE]
        xw = (jnp.concatenate([xs, xs], axis=1) * lane_expand(wend)).astype(BF16)
        st_s[c] = _dot_tn(ball, xw)
        cbs = [_dot_nt(jnp.where(lo64 if g == 0 else jnp.logical_not(lo64), call, 0.0).astype(BF16),
                       ball) for g in range(SSD_GROUPS)]
        for pr in range(nh // 2):
            ws = []
            for h in (2 * pr, 2 * pr + 1):
                seg_f = ac[:, h:h + 1] - ac_t[h:h + 1, :]
                seg_b = ac[:, nh + h:nh + h + 1] - ac_t[nh + h:nh + h + 1, :]
                w = cbs[h // hpg] * (
                    jnp.exp(jnp.where(lower, seg_f, -jnp.inf)) * dt_t[h:h + 1, :]
                    + jnp.exp(jnp.where(upper, seg_b, -jnp.inf)) * dt_t[nh + h:nh + h + 1, :])
                ws.append(w.astype(BF16))
            xp = xs[:, pr * LANES:(pr + 1) * LANES]
            xbd = jnp.concatenate([jnp.where(lo64, xp, 0.0), jnp.where(lo64, 0.0, xp)],
                                  axis=0).astype(BF16)
            y_s[rs, pr * LANES:(pr + 1) * LANES] = _dot(jnp.concatenate(ws, axis=1), xbd)
        return carry

    lax.fori_loop(0, nc, chunk, 0)

    h_s[...] = jnp.zeros(h_s.shape, F32)
    if has_h0:
        for hl in range(2 * nh):
            direction, h = divmod(hl, nh)
            g = h // hpg
            h_s[direction, g * SSD_STATE:(g + 1) * SSD_STATE, h * hd:(h + 1) * hd] = h0_ref[hl]

    def carry_states(kstep, carry):
        for direction in range(2):
            c = kstep if direction == 0 else nc - 1 - kstep
            r0 = pl.multiple_of(c * q, q)
            rs = pl.ds(r0, q)
            ds_ = slice(direction * SSD_D, (direction + 1) * SSD_D)
            edge = r0 + q - 1 if direction == 0 else r0
            call = xbc_s[rs, c_off:c_off + SSD_GROUPS * SSD_STATE].astype(BF16)
            h_in = h_s[direction]
            y_s[rs, :] += _dot(call, jnp.where(own, h_in, 0.0).astype(BF16)) * eb_s[rs, ds_]
            h_s[direction] = eb_s[pl.ds(edge, 1), ds_] * h_in + st_s[c][:, ds_]
        return carry

    lax.fori_loop(0, nc, carry_states, 0)

    y = y_s[...] + d_ref[...] * xbc_s[:, 0:SSD_D]
    y = y * _silu(zx_ref[:, 0:SSD_D])
    y_ref[...] = _rms(y, nw_ref[...])
    if hout_ref is not None:
        for hl in range(2 * nh):
            direction, h = divmod(hl, nh)
            g = h // hpg
            hout_ref[hl] = h_s[direction, g * SSD_STATE:(g + 1) * SSD_STATE, h * hd:(h + 1) * hd]


def _ssd_call(zx, dt, lw, layer, state, seq):
    has_h0 = state is not None
    rows = zx.shape[0]
    nseq = rows // seq
    nc = seq // SSD_CHUNK
    nst = 2 * SSD_HEADS

    def seqspec(w):
        return pl.BlockSpec((seq, w), lambda b: (b, 0))

    def par(shape):
        return _const_spec(shape, layer, single_buffer=False)

    in_specs = [seqspec(1024), seqspec(LANES), par((8, SSD_CONV_DIM)), par((1, SSD_CONV_DIM)),
                par((1, LANES)), par((1, LANES)), par((1, SSD_D)), par((1, SSD_D))]
    args = [zx, dt, lw['conv_w'], lw['conv_b'], lw['dt_bias'], lw['a'], lw['d'], lw['norm']]
    y_shape = jax.ShapeDtypeStruct((rows, SSD_D), F32)
    if has_h0:
        in_specs.append(pl.BlockSpec((None, None, nst, SSD_HEAD_DIM, SSD_STATE),
                                     lambda b: (b, layer, 0, 0, 0)))
        args.append(state)
        out_shape = y_shape
        out_specs = seqspec(SSD_D)
    else:
        out_shape = [y_shape, jax.ShapeDtypeStruct((nseq, nst, SSD_HEAD_DIM, SSD_STATE), F32)]
        out_specs = [seqspec(SSD_D),
                     pl.BlockSpec((None, nst, SSD_HEAD_DIM, SSD_STATE), lambda b: (b, 0, 0, 0))]
    return pl.pallas_call(
        functools.partial(_ssd_body, seq=seq, has_h0=has_h0),
        out_shape=out_shape,
        grid=(nseq,),
        in_specs=in_specs,
        out_specs=out_specs,
        scratch_shapes=[
            pltpu.VMEM((seq, SSD_CONV_DIM), F32),
            pltpu.VMEM((seq, SSD_D), F32),
            pltpu.VMEM((seq, 2 * SSD_D), F32),
            pltpu.VMEM((seq, LANES), F32),
            pltpu.VMEM((nc, SSD_GROUPS * SSD_STATE, 2 * SSD_D), F32),
            pltpu.VMEM((2, SSD_GROUPS * SSD_STATE, SSD_D), F32),
        ],
        compiler_params=_cparams("parallel"),
        name="ssd_mixer",
    )(*args)


def _s5_tables(a_re, a_im, log_dt, b_re, b_im, c_re, c_im):
    t = S5_T
    gh = S5_HGROUPS
    f32 = lambda x: x.astype(F32)
    lam_r, lam_i = f32(a_re), f32(a_im)
    step = jnp.exp(f32(log_dt))[..., None]
    xr, xi = lam_r * step, lam_i * step

    def powers(ks):
        ks = jnp.asarray(ks, F32)[None, None, :, None, None]
        mag = jnp.exp(xr[:, :, None] * ks)
        return mag * jnp.cos(xi[:, :, None] * ks), mag * jnp.sin(xi[:, :, None] * ks)

    pw_r, pw_i = powers(np.arange(t + 1))
    nr, ni = pw_r[:, :, 1] - 1.0, pw_i[:, :, 1]
    den = lam_r * lam_r + lam_i * lam_i
    fr = (nr * lam_r + ni * lam_i) / den
    fi = (ni * lam_r - nr * lam_i) / den
    br, bi = f32(b_re)[:, None], f32(b_im)[:, None]
    bb_r = fr[..., None] * br - fi[..., None] * bi
    bb_i = fr[..., None] * bi + fi[..., None] * br
    cr, ci = f32(c_re), f32(c_im)

    kern = []
    for direction in range(2):
        pr = pw_r[:, direction, :t][:, :, :, None, :]
        pi = pw_i[:, direction, :t][:, :, :, None, :]
        cpr, cpi = cr[:, None] * pr - ci[:, None] * pi, cr[:, None] * pi + ci[:, None] * pr
        kern.append(jnp.einsum('dtgcn,dgnk->dtgck', cpr, bb_r[:, direction], precision=HIGHEST)
                    - jnp.einsum('dtgcn,dgnk->dtgck', cpi, bb_i[:, direction], precision=HIGHEST))
    kf, kb = kern
    lag = [kb[:, -d] if d < 0 else (kf[:, 0] + kb[:, 0] if d == 0 else kf[:, d])
           for d in range(-(t - 1), t)]
    kt = jnp.stack(lag, axis=1).reshape(DEPTH, S5_LAGS, 2, gh, S5_GROUP_CH, S5_GROUP_CH)
    kt = jnp.transpose(kt, (0, 2, 1, 5, 3, 4)).reshape(DEPTH, 2, S5_LAGS, S5_GROUP_CH, LANES)

    def by_half(x):
        return jnp.transpose(x.reshape(DEPTH, -1, 2, gh, S5_STATE), (0, 2, 1, 3, 4))

    def dir_powers(direction, ks):
        pr, pi = powers(np.asarray(ks))
        return pr[:, direction], pi[:, direction]

    bt_r = jnp.transpose(bb_r.reshape(DEPTH, 2, 2, gh, S5_STATE, S5_GROUP_CH), (0, 1, 2, 5, 3, 4))
    bt_i = jnp.transpose(bb_i.reshape(DEPTH, 2, 2, gh, S5_STATE, S5_GROUP_CH), (0, 1, 2, 5, 3, 4))
    sw = []
    for direction, ks in ((0, [t - 1 - s for s in range(t)]), (1, list(range(t)))):
        pr, pi = dir_powers(direction, ks)
        pr = by_half(pr)[:, :, :, None]
        pi = by_half(pi)[:, :, :, None]
        wr, wi = bt_r[:, direction][:, :, None], bt_i[:, direction][:, :, None]
        sw += [pr * wr - pi * wi, pr * wi + pi * wr]
    sw = jnp.stack(sw, axis=3).reshape(DEPTH, 2, t, 4, S5_GROUP_CH, gh * S5_STATE)
    ct_r = jnp.transpose(cr.reshape(DEPTH, 2, gh, S5_GROUP_CH, S5_STATE), (0, 1, 4, 2, 3))
    ct_i = jnp.transpose(ci.reshape(DEPTH, 2, gh, S5_GROUP_CH, S5_STATE), (0, 1, 4, 2, 3))
    aw = []
    for direction, ks in ((0, list(range(1, t + 1))), (1, [t - k for k in range(t)])):
        pr, pi = dir_powers(direction, ks)
        pr = jnp.swapaxes(by_half(pr), 3, 4)[..., None]
        pi = jnp.swapaxes(by_half(pi), 3, 4)[..., None]
        wr, wi = ct_r[:, :, None], ct_i[:, :, None]
        aw += [wr * pr - wi * pi, -(wr * pi + wi * pr)]
    aw = jnp.stack(aw, axis=2).reshape(DEPTH, 2, 4, t, S5_STATE, LANES)

    apow = jnp.stack([pw_r[:, 0, t], pw_i[:, 0, t], pw_r[:, 1, t], pw_i[:, 1, t]], axis=1)
    seg_r, seg_i = powers(t * np.arange(S5_NC + 1))
    back_r, back_i = dir_powers(1, t * (S5_NC - 1 - np.arange(S5_NC)))
    pseg = jnp.stack([seg_r[:, 0, :S5_NC], seg_i[:, 0, :S5_NC], back_r, back_i], axis=1)
    nkb = S5_CW // LANES
    pseg = jnp.transpose(pseg.reshape(DEPTH, 4, S5_NC, nkb, LANES), (0, 1, 3, 2, 4))
    pseg = pseg.reshape(DEPTH, 4 * nkb, S5_NC, LANES)
    aseg = jnp.stack([seg_r[:, 0, S5_NC], seg_i[:, 0, S5_NC],
                      seg_r[:, 1, S5_NC], seg_i[:, 1, S5_NC]], axis=1)
    return (kt, sw, aw, apow.reshape(DEPTH, 1, S5_SW), pseg, aseg.reshape(DEPTH, 1, S5_SW))


def _s5_expand_operators(kt_ref, sw_ref, aw_ref, toep_s, sop_s, aop_s):
    gh = S5_HGROUPS
    sh_ch = S5_GROUP_CH.bit_length() - 1
    sh_st = S5_STATE.bit_length() - 1

    def group_mask(shape, row_shift, lane_shift):
        r = lax.broadcasted_iota(jnp.int32, shape, 0) >> row_shift
        c = lax.broadcasted_iota(jnp.int32, shape, 1) >> lane_shift
        return r == c

    def blockdiag(x, mask):
        return jnp.where(mask, jnp.concatenate([x] * gh, axis=0), 0.0).astype(BF16)

    m_kk = group_mask((LANES, LANES), sh_ch, sh_ch)
    m_ks = group_mask((LANES, gh * S5_STATE), sh_ch, sh_st)
    m_sk = group_mask((gh * S5_STATE, LANES), sh_st, sh_ch)
    cw = gh * S5_STATE
    for hf in range(2):
        lags = [blockdiag(kt_ref[hf, d], m_kk) for d in range(S5_LAGS)]
        for s in range(S5_T):
            for t in range(S5_T):
                toep_s[hf, s * LANES:(s + 1) * LANES, t * LANES:(t + 1) * LANES] = lags[t - s + S5_T - 1]
            for comp in range(4):
                sop_s[hf, s * LANES:(s + 1) * LANES, comp * cw:(comp + 1) * cw] = blockdiag(
                    sw_ref[hf, s, comp], m_ks)
        for comp in range(4):
            for t in range(S5_T):
                aop_s[hf, comp * cw:(comp + 1) * cw, t * LANES:(t + 1) * LANES] = blockdiag(
                    aw_ref[hf, comp, t], m_sk)


def _s5_body(up_ref, us_ref, kt_ref, sw_ref, aw_ref, at_ref, pseg_ref, aseg_ref, h0_ref,
             y_ref, hout_ref, toep_s, sop_s, aop_s, st_s, hin_s, yacc_s, *, nblk_p, nseg_s):
    i = pl.program_id(0)

    @pl.when(i == 0)
    def _():
        _s5_expand_operators(kt_ref, sw_ref, aw_ref, toep_s, sop_s, aop_s)

    cw = S5_CW
    nkb = cw // LANES
    is_p = i < nblk_p

    def time_rows(s):
        return pl.ds(s, S5_ROWS, stride=S5_T)

    for hf in range(2):
        xh = jnp.concatenate(
            [jnp.where(is_p, up_ref[hf, time_rows(s), :], us_ref[hf, time_rows(s), :])
             for s in range(S5_T)], axis=1).astype(BF16)
        yacc_s[hf] = _dot(xh, toep_s[hf])
        sh = _dot(xh, sop_s[hf])
        for comp in range(4):
            for k in range(nkb // 2):
                col = (comp * (nkb // 2) + k) * LANES
                st_s[comp * nkb + hf * (nkb // 2) + k] = sh[:, col:col + LANES]

    def load(ref, comp, rows):
        return jnp.concatenate([ref[comp * nkb + k, rows, :] for k in range(nkb)], axis=1)

    def store(ref, comp, rows, val):
        for k in range(nkb):
            ref[comp * nkb + k, rows, :] = val[:, k * LANES:(k + 1) * LANES]

    def cmul(ar, ai, hr, hi):
        return ar * hr - ai * hi, ar * hi + ai * hr

    at = at_ref[...]
    ar_f, ai_f, ar_b, ai_b = [at[:, k * cw:(k + 1) * cw] for k in range(4)]

    def step(c, carry):
        hfr, hfi, hbr, hbi = carry
        rf = pl.ds(c, S5_HB, stride=S5_NC)
        rb = pl.ds(S5_NC - 1 - c, S5_HB, stride=S5_NC)
        for comp, val, rows in ((0, hfr, rf), (1, hfi, rf), (2, hbr, rb), (3, hbi, rb)):
            store(hin_s, comp, rows, val)
        fr, fi = cmul(ar_f, ai_f, hfr, hfi)
        br, bi = cmul(ar_b, ai_b, hbr, hbi)
        return (fr + load(st_s, 0, rf), fi + load(st_s, 1, rf),
                br + load(st_s, 2, rb), bi + load(st_s, 3, rb))

    zero = jnp.zeros((S5_HB, cw), F32)
    fin = lax.fori_loop(0, S5_NC, step, (zero, zero, zero, zero))
    hout_ref[...] = jnp.concatenate(fin, axis=-1)

    @pl.when(i >= nblk_p)
    def _():
        h0 = h0_ref[...]
        aseg = aseg_ref[...]
        sr_f, si_f, sr_b, si_b = [aseg[:, k * cw:(k + 1) * cw] for k in range(4)]
        ent = [[None] * S5_HB for _ in range(4)]
        for s in range(S5_HB // nseg_s):
            hr, hi = h0[s:s + 1, 0:cw], h0[s:s + 1, cw:2 * cw]
            for j in range(nseg_s):
                v = s * nseg_s + j
                ent[0][v], ent[1][v] = hr, hi
                hr, hi = cmul(sr_f, si_f, hr, hi)
                hr, hi = hr + fin[0][v:v + 1], hi + fin[1][v:v + 1]
            hr, hi = h0[s:s + 1, 2 * cw:3 * cw], h0[s:s + 1, 3 * cw:4 * cw]
            for j in range(nseg_s - 1, -1, -1):
                v = s * nseg_s + j
                ent[2][v], ent[3][v] = hr, hi
                hr, hi = cmul(sr_b, si_b, hr, hi)
                hr, hi = hr + fin[2][v:v + 1], hi + fin[3][v:v + 1]
        for direction in range(2):
            for k in range(nkb):
                kr = (2 * direction) * nkb + k
                ki = (2 * direction + 1) * nkb + k
                pr, pi = pseg_ref[kr], pseg_ref[ki]
                for v in range(S5_HB):
                    rows = slice(v * S5_NC, (v + 1) * S5_NC)
                    er = ent[2 * direction][v][:, k * LANES:(k + 1) * LANES]
                    ei = ent[2 * direction + 1][v][:, k * LANES:(k + 1) * LANES]
                    dr, di = cmul(pr, pi, er, ei)
                    hin_s[kr, rows, :] += dr
                    hin_s[ki, rows, :] += di

    for hf in range(2):
        hh = jnp.concatenate(
            [hin_s[comp * nkb + hf * (nkb // 2) + k] for comp in range(4) for k in range(nkb // 2)],
            axis=1).astype(BF16)
        y = yacc_s[hf] + _dot(hh, aop_s[hf])
        for t in range(S5_T):
            y_ref[hf, time_rows(t), :] = y[:, t * LANES:(t + 1) * LANES]


def _s5_call(u_p, u_s, tables, layer, h0, seq_p, seq_s):
    kt, sw, aw, apow, pseg, aseg = tables
    seg_tok = S5_T * S5_NC
    assert seq_p == seg_tok and seq_s % seg_tok == 0 and S5_HB % (seq_s // seg_tok) == 0
    blk_tok = S5_ROWS * S5_T
    nblk_p = u_p.shape[1] // blk_tok
    nblk = nblk_p + u_s.shape[1] // blk_tok
    uh = S5_D // LANES

    def par(a):
        return _const_spec(a.shape[1:], layer)

    nlb = S5_SW // LANES
    return pl.pallas_call(
        functools.partial(_s5_body, nblk_p=nblk_p, nseg_s=seq_s // seg_tok),
        out_shape=[jax.ShapeDtypeStruct((uh, nblk * blk_tok, LANES), F32),
                   jax.ShapeDtypeStruct((nblk, S5_HB, S5_SW), F32)],
        grid=(nblk,),
        in_specs=[pl.BlockSpec((uh, blk_tok, LANES), lambda i: (0, jnp.minimum(i, nblk_p - 1), 0)),
                  pl.BlockSpec((uh, blk_tok, LANES), lambda i: (0, jnp.maximum(i - nblk_p, 0), 0)),
                  par(kt), par(sw), par(aw), par(apow), par(pseg), par(aseg),
                  pl.BlockSpec((None, None, S5_HB, S5_SW), lambda i: (layer, i, 0, 0))],
        out_specs=[pl.BlockSpec((uh, blk_tok, LANES), lambda i: (0, i, 0)),
                   pl.BlockSpec((None, S5_HB, S5_SW), lambda i: (i, 0, 0))],
        scratch_shapes=[pltpu.VMEM((2, S5_KW, S5_KW), BF16), pltpu.VMEM((2, S5_KW, S5_SH), BF16),
                        pltpu.VMEM((2, S5_SH, S5_KW), BF16),
                        pltpu.VMEM((nlb, S5_ROWS, LANES), F32), pltpu.VMEM((nlb, S5_ROWS, LANES), F32),
                        pltpu.VMEM((2, S5_ROWS, S5_KW), F32)],
        compiler_params=_cparams("arbitrary"),
        name="s5_mixer",
    )(u_p, u_s, kt, sw, aw, apow, pseg, aseg, h0)


def _gelu_tanh(x):
    return 0.5 * x * (1.0 + jnp.tanh(math.sqrt(2.0 / math.pi) * (x + 0.044715 * (x * x * x))))


def _out_body(*refs, final):
    (x_ref, yssd_ref, ynat_ref, u_ref, y5_ref, g1_ref, m2_ref, d5_ref,
     wglu_ref, bglu_ref, wout_ref, nm_ref, w1_ref, w2_ref) = refs[:14]
    if final:
        nf_ref, o_ref = refs[14:]
    else:
        (o_ref,) = refs[14:]
    uh = S5_D // LANES
    for sb in range(ROW_TILE // TOK_TILE):
        rows = slice(sb * TOK_TILE, (sb + 1) * TOK_TILE)
        y5_in = jnp.concatenate([y5_ref[hf, rows, :] for hf in range(uh)], axis=1)
        u = jnp.concatenate([u_ref[hf, rows, :] for hf in range(uh)], axis=1)
        g = _gelu_tanh(y5_in + d5_ref[...] * u)
        y5 = g * jax.nn.sigmoid(_dot(g.astype(BF16), wglu_ref[...]) + bglu_ref[...])
        mix = (_dot(yssd_ref[rows, :].astype(BF16), wout_ref[0:SSD_D, :])
               + _dot(y5.astype(BF16), wout_ref[SSD_D:SSD_D + S5_D, :])
               + _dot(ynat_ref[rows, :].astype(BF16), wout_ref[SSD_D + S5_D:, :]))
        x = x_ref[rows, :] + g1_ref[...] * mix
        h2 = _rms(x, nm_ref[...]) * (1.0 + m2_ref[:, D_MODEL:2 * D_MODEL]) + m2_ref[:, 0:D_MODEL]
        f = jnp.maximum(_dot(h2.astype(BF16), w1_ref[...]), 0.0)
        f = (f * f).astype(BF16)
        x = x + m2_ref[:, 2 * D_MODEL:3 * D_MODEL] * _dot(f, w2_ref[...])
        if final:
            x = _rms(x, nf_ref[...])
        o_ref[rows, :] = x


def _out_call(x, y_ssd, y_nat, u, y5, y5_tile_off, cond_idx, mods4, lw, norm_f, layer, final):
    rows = x.shape[0]
    uh = S5_D // LANES

    def tile(w):
        return pl.BlockSpec((ROW_TILE, w), lambda i: (i, 0))

    def par(shape):
        return _const_spec(shape, layer)

    in_specs = [
        tile(D_MODEL), tile(SSD_D), tile(NAT_D),
        pl.BlockSpec((uh, ROW_TILE, LANES), lambda i: (0, i, 0)),
        pl.BlockSpec((uh, ROW_TILE, LANES), lambda i: (0, i + y5_tile_off, 0)),
        pl.BlockSpec((None, None, 1, D_MODEL), lambda i: (layer, cond_idx(i), 0, 2)),
        pl.BlockSpec((None, None, 1, 3 * D_MODEL), lambda i: (layer, cond_idx(i), 0, 1)),
        par((1, S5_D)), par((S5_D, S5_D)), par((1, S5_D)),
        par((D_MODEL, D_MODEL)), par((1, D_MODEL)),
        par((D_MODEL, D_FF)), par((D_FF, D_MODEL))]
    args = [x, y_ssd, y_nat, u, y5, mods4, mods4, lw['s5_d'], lw['w_glu'], lw['b_glu'],
            lw['w_out'], lw['norm_mlp'], lw['w_ff1'], lw['w_ff2']]
    if final:
        in_specs.append(pl.BlockSpec((1, D_MODEL), lambda i: (0, 0)))
        args.append(norm_f.reshape(1, D_MODEL))
    return pl.pallas_call(
        functools.partial(_out_body, final=final),
        out_shape=jax.ShapeDtypeStruct((rows, D_MODEL), F32),
        grid=(rows // ROW_TILE,),
        in_specs=in_specs,
        out_specs=tile(D_MODEL),
        compiler_params=_cparams("parallel"),
        name="out_mlp",
    )(*args)


def _pack_w_in(w_in):
    o_dt = SSD_D + SSD_CONV_DIM
    o_u = o_dt + 2 * SSD_HEADS
    pad = IN_PACKED - w_in.shape[-1]
    return jnp.concatenate(
        [w_in[..., :o_dt], w_in[..., o_u:], w_in[..., o_dt:o_u],
         jnp.zeros(w_in.shape[:-1] + (pad,), w_in.dtype)], axis=-1).astype(BF16)


def _lane_pad(x):
    return jnp.pad(x, [(0, 0)] * (x.ndim - 1) + [(0, LANES - x.shape[-1])])


def kernel(x_prompt, x_sample, cache_nat_k, cache_nat_v, state_ssd, state_s5_re, state_s5_im,
           c, c_ctx, w_mod, b_mod, norm_mix, norm_mlp, w_in, ssd_conv_w, ssd_conv_b,
           ssd_dt_bias, ssd_a_log, ssd_d, ssd_norm, s5_a_re, s5_a_im, s5_log_dt,
           s5_b_re, s5_b_im, s5_c_re, s5_c_im, s5_d, s5_w_glu, s5_b_glu, nat_rpb,
           w_out, w_ff1, w_ff2, norm_f):
    bp, seq_p, _ = x_prompt.shape
    bs, seq_s, _ = x_sample.shape
    assert seq_p == TOK_TILE and seq_s == 4 * TOK_TILE
    rows_p = bp * seq_p
    rows_s = bs * seq_s

    ncp = -(-(1 + bs) // 8) * 8
    cond = jnp.concatenate([c_ctx[None, :], c, jnp.zeros((ncp - 1 - bs, D_MODEL), F32)], axis=0)
    mods = _mods_call(cond, w_mod, b_mod)
    mods4 = mods.reshape(DEPTH, ncp, 1, N_MOD * D_MODEL)

    w_in_p = _pack_w_in(w_in)
    ssd_w = {
        'conv_w': jnp.pad(ssd_conv_w.astype(F32), [(0, 0), (0, 8 - SSD_CONV), (0, 0)]),
        'conv_b': ssd_conv_b.astype(F32).reshape(DEPTH, 1, SSD_CONV_DIM),
        'dt_bias': _lane_pad(ssd_dt_bias.astype(F32).reshape(DEPTH, 1, 2 * SSD_HEADS)),
        'a': _lane_pad(-jnp.exp(ssd_a_log.astype(F32)).reshape(DEPTH, 1, 2 * SSD_HEADS)),
        'd': jnp.repeat(ssd_d.astype(F32), SSD_HEAD_DIM, axis=-1).reshape(DEPTH, 1, SSD_D),
        'norm': ssd_norm.astype(F32).reshape(DEPTH, 1, SSD_D),
    }
    out_w = {
        's5_d': s5_d.astype(F32).reshape(DEPTH, 1, S5_D),
        'w_glu': s5_w_glu.astype(BF16),
        'b_glu': s5_b_glu.astype(F32).reshape(DEPTH, 1, S5_D),
        'w_out': w_out.astype(BF16),
        'norm_mlp': norm_mlp.astype(F32).reshape(DEPTH, 1, D_MODEL),
        'w_ff1': w_ff1.astype(BF16),
        'w_ff2': w_ff2.astype(BF16),
    }
    s5_tabs = _s5_tables(s5_a_re, s5_a_im, s5_log_dt, s5_b_re, s5_b_im, s5_c_re, s5_c_im)
    nat_bias = _nat_pair_tiles(nat_rpb)

    cache_k = cache_nat_k.reshape(bs, DEPTH, -1, NAT_D)
    cache_v = cache_nat_v.reshape(bs, DEPTH, -1, NAT_D)
    st_ssd = jnp.swapaxes(state_ssd, -1, -2).reshape(bs, DEPTH, 2 * SSD_HEADS, SSD_STATE, SSD_HEAD_DIM)
    s5_tok = S5_ROWS * S5_T
    assert rows_p % s5_tok == 0 and rows_s % s5_tok == 0 and s5_tok % seq_s == 0
    nseq_s = s5_tok // seq_s
    st5 = jnp.stack([state_s5_re[:, :, 0], state_s5_im[:, :, 0],
                     state_s5_re[:, :, 1], state_s5_im[:, :, 1]], axis=2)
    st5 = jnp.transpose(st5.astype(F32).reshape(bs // nseq_s, nseq_s, DEPTH, S5_SW), (2, 0, 1, 3))
    st5 = jnp.pad(st5, [(0, 0), (rows_p // s5_tok, 0), (0, S5_HB - nseq_s), (0, 0)])

    assert rows_p % ROW_TILE == 0 and seq_s % ROW_TILE == 0
    tiles_per_seq = seq_s // ROW_TILE
    cond_p = lambda i: 0
    cond_s = lambda i: 1 + i // tiles_per_seq
    x_p = x_prompt.reshape(rows_p, D_MODEL)
    x_s = x_sample.reshape(rows_s, D_MODEL)
    new_k, new_v, new_ssd, new_s5 = [], [], [], []
    for l in range(DEPTH):
        final = l == DEPTH - 1
        zx_p, dt_p, u_p, q_p, k_p, v_p = _inproj_call(x_p, cond_p, mods4, norm_mix, w_in_p, l)
        zx_s, dt_s, u_s, q_s, k_s, v_s = _inproj_call(x_s, cond_s, mods4, norm_mix, w_in_p, l)
        new_k.append(k_p.reshape(bp, seq_p, NAT_D))
        new_v.append(v_p.reshape(bp, seq_p, NAT_D))

        y_nat_p = _ctx_attn_call(q_p, k_p, v_p)
        y_nat_s = _nat_attn_call(q_s, k_s, v_s, cache_k, cache_v, nat_bias, l, bs, seq_s)

        y_ssd_p, ssd_l = _ssd_call(zx_p, dt_p, ssd_w, l, None, seq_p)
        y_ssd_s = _ssd_call(zx_s, dt_s, ssd_w, l, st_ssd, seq_s)
        new_ssd.append(ssd_l)

        y5, s5_l = _s5_call(u_p, u_s, s5_tabs, l, st5, seq_p, seq_s)
        new_s5.append(s5_l[:rows_p // s5_tok])

        x_p = _out_call(x_p, y_ssd_p, y_nat_p, u_p, y5, 0, cond_p, mods4, out_w, norm_f, l, final)
        x_s = _out_call(x_s, y_ssd_s, y_nat_s, u_s, y5, rows_p // ROW_TILE, cond_s,
                        mods4, out_w, norm_f, l, final)

    y_prompt = x_p.reshape(bp, seq_p, D_MODEL)
    y_sample = x_s.reshape(bs, seq_s, D_MODEL)
    out_k = jnp.stack(new_k, axis=1).reshape(bp, DEPTH, seq_p, NAT_HEADS, NAT_HEAD_DIM)
    out_v = jnp.stack(new_v, axis=1).reshape(bp, DEPTH, seq_p, NAT_HEADS, NAT_HEAD_DIM)
    out_ssd = jnp.swapaxes(jnp.stack(new_ssd, axis=1), -1, -2)
    out_ssd = out_ssd.reshape(bp, DEPTH, 2, SSD_HEADS, SSD_HEAD_DIM, SSD_STATE)
    s5 = jnp.stack(new_s5, axis=0)[:, :, :s5_tok // seq_p]
    s5 = s5.reshape(DEPTH, bp, 4, S5_GROUPS, S5_STATE)
    s5 = jnp.transpose(s5, (1, 0, 2, 3, 4))
    out_re = s5[:, :, 0::2]
    out_im = s5[:, :, 1::2]
    return y_prompt, y_sample, out_k, out_v, out_ssd, out_re, out_im
```

```python
import functools
import math

import numpy as np
import jax
import jax.numpy as jnp
from jax import lax
from jax.experimental import pallas as pl
from jax.experimental.pallas import tpu as pltpu

F32 = jnp.float32
BF16 = jnp.bfloat16
HIGHEST = lax.Precision.HIGHEST

D_MODEL = 1024
DEPTH = 4
GRID_W = 64
SSD_HEADS = 6
SSD_HEAD_DIM = 64
SSD_D = SSD_HEADS * SSD_HEAD_DIM
SSD_GROUPS = 2
SSD_STATE = 64
SSD_CONV = 5
SSD_CHUNK = 128
SSD_CONV_DIM = SSD_D + 2 * SSD_GROUPS * SSD_STATE
SSD_ILV = 4
S5_GROUPS = 16
S5_GROUP_CH = 16
S5_D = S5_GROUPS * S5_GROUP_CH
S5_STATE = 64
NAT_HEADS = 6
NAT_HEAD_DIM = 64
NAT_D = NAT_HEADS * NAT_HEAD_DIM
NAT_KH = 8
NAT_KW = 16
D_FF = 4 * D_MODEL
N_MOD = 6
EPS = 1e-6

LANES = 128
TOK_TILE = 256
ROW_TILE = 512
IN_PACKED = 2560
S5_T = 8
S5_HGROUPS = LANES // S5_GROUP_CH
S5_KW = S5_T * LANES
S5_CW = S5_GROUPS * S5_STATE
S5_SW = 4 * S5_CW
S5_SH = S5_SW // 2
S5_NC = 32
S5_HB = 8
S5_ROWS = S5_HB * S5_NC
S5_LAGS = 2 * S5_T - 1
VMEM_LIMIT = 56 * 1024 * 1024


def _cparams(*sem):
    return pltpu.CompilerParams(dimension_semantics=sem, vmem_limit_bytes=VMEM_LIMIT)


def _dot(a, b, precision=None):
    return jnp.dot(a, b, preferred_element_type=F32, precision=precision)


def _dot_nt(a, b, precision=None):
    return lax.dot_general(a, b, (((1,), (1,)), ((), ())), preferred_element_type=F32,
                           precision=precision)


def _dot_tn(a, b, precision=None):
    return lax.dot_general(a, b, (((0,), (0,)), ((), ())), preferred_element_type=F32,
                           precision=precision)


def _silu(x):
    hx = 0.5 * x
    return hx + hx * jnp.tanh(hx)


def _rms(x, g):
    return x * lax.rsqrt(jnp.mean(x * x, axis=-1, keepdims=True) + EPS) * g


def _const_spec(shape, layer, single_buffer=True):
    return pl.BlockSpec((None,) + tuple(shape), lambda *_: (layer,) + (0,) * len(shape),
                        pipeline_mode=pl.Buffered(1) if single_buffer else None)


def _mods_body(cond_ref, w_ref, b_ref, o_ref):
    s = _silu(cond_ref[...])
    o_ref[...] = _dot(s.astype(BF16), w_ref[...].astype(BF16)) + b_ref[...]


def _mods_call(cond, w_mod, b_mod):
    ncp = cond.shape[0]
    blk = D_MODEL
    return pl.pallas_call(
        _mods_body,
        out_shape=jax.ShapeDtypeStruct((DEPTH, ncp, N_MOD * D_MODEL), F32),
        grid=(DEPTH, N_MOD),
        in_specs=[
            pl.BlockSpec((ncp, D_MODEL), lambda l, j: (0, 0)),
            pl.BlockSpec((None, D_MODEL, blk), lambda l, j: (l, 0, j)),
            pl.BlockSpec((None, 1, blk), lambda l, j: (l, 0, j)),
        ],
        out_specs=pl.BlockSpec((None, ncp, blk), lambda l, j: (l, 0, j)),
        compiler_params=_cparams("arbitrary", "arbitrary"),
        name="adaln_mods",
    )(cond, w_mod, b_mod.reshape(DEPTH, 1, N_MOD * D_MODEL))


def _inproj_body(x_ref, mod_ref, g_ref, w_ref, zx_ref, dt_ref, u_ref, q_ref, k_ref, v_ref):
    x = x_ref[...]
    h = _rms(x, g_ref[...]) * (1.0 + mod_ref[:, D_MODEL:2 * D_MODEL]) + mod_ref[:, 0:D_MODEL]
    p = _dot(h.astype(BF16), w_ref[...])
    zx_ref[...] = p[:, 0:1024]
    for hf in range(S5_D // LANES):
        u_ref[hf] = p[:, 1024 + hf * LANES:1024 + (hf + 1) * LANES]
    q_ref[...] = p[:, 1280:1664]
    k_ref[...] = p[:, 1664:2048]
    v_ref[...] = p[:, 2048:2432]
    dt_ref[...] = p[:, 2432:2560]


def _inproj_call(x, cond_idx, mods4, norm_mix, w_in_p, layer):
    rows = x.shape[0]

    def tile(w):
        return pl.BlockSpec((ROW_TILE, w), lambda i: (i, 0))

    def out(w):
        return jax.ShapeDtypeStruct((rows, w), F32)

    uh = S5_D // LANES
    return pl.pallas_call(
        _inproj_body,
        out_shape=[out(1024), out(LANES), jax.ShapeDtypeStruct((uh, rows, LANES), F32),
                   out(NAT_D), out(NAT_D), out(NAT_D)],
        grid=(rows // ROW_TILE,),
        in_specs=[
            tile(D_MODEL),
            pl.BlockSpec((None, None, 1, 2 * D_MODEL), lambda i: (layer, cond_idx(i), 0, 0)),
            _const_spec((1, D_MODEL), layer, single_buffer=False),
            _const_spec((D_MODEL, IN_PACKED), layer),
        ],
        out_specs=[tile(1024), tile(LANES), pl.BlockSpec((uh, ROW_TILE, LANES), lambda i: (0, i, 0)),
                   tile(NAT_D), tile(NAT_D), tile(NAT_D)],
        compiler_params=_cparams("parallel"),
        name="in_proj",
    )(x, mods4, norm_mix.reshape(DEPTH, 1, D_MODEL), w_in_p)


NAT_SCALE = NAT_HEAD_DIM ** -0.5


def _values_with_ones(v):
    return jnp.concatenate([v, jnp.ones(v.shape, BF16)], axis=1)


def _pair_attention(q, keys, v_ext, bias, lo):
    outs = []
    for hh in range(2):
        qm = jnp.where(lo if hh == 0 else jnp.logical_not(lo), q, 0.0).astype(BF16)
        s = _dot_nt(qm, keys)
        if bias is not None:
            b = bias(hh)
            nb = b.shape[1]
            s = jnp.concatenate([s[:, 0:nb] + b, s[:, nb:]], axis=1)
        p = jnp.exp(s - jnp.max(s, axis=-1, keepdims=True)).astype(BF16)
        o = _dot(p, v_ext)
        outs.append(o[:, 0:LANES] / o[:, LANES:2 * LANES])
    return jnp.where(lo, outs[0], outs[1])


def _ctx_attn_body(q_ref, k_ref, v_ref, o_ref):
    lo = lax.broadcasted_iota(jnp.int32, (1, LANES), 1) < NAT_HEAD_DIM
    for p in range(NAT_D // LANES):
        cs = slice(p * LANES, (p + 1) * LANES)
        o_ref[:, cs] = _pair_attention(q_ref[:, cs] * NAT_SCALE, k_ref[:, cs].astype(BF16),
                                       _values_with_ones(v_ref[:, cs].astype(BF16)), None, lo)


def _ctx_attn_call(q, k, v):
    rows = q.shape[0]
    spec = pl.BlockSpec((TOK_TILE, NAT_D), lambda b: (b, 0))
    return pl.pallas_call(
        _ctx_attn_body,
        out_shape=jax.ShapeDtypeStruct((rows, NAT_D), F32),
        grid=(rows // TOK_TILE,),
        in_specs=[spec, spec, spec],
        out_specs=spec,
        compiler_params=_cparams("parallel"),
        name="ctx_attn",
    )(q, k, v)


NAT_QROWS = 4
NAT_HEAD_PAIRS = NAT_HEADS // 2
NAT_NPAIR = 2 * NAT_KH


def _nat_window_start(r, rows):
    kh = min(NAT_KH, rows)
    return int(np.clip(r - kh // 2, 0, rows - kh))


def _nat_blocks(rows):
    kh = min(NAT_KH, rows)
    out = []
    for j in range(rows // NAT_QROWS):
        rs = [_nat_window_start(r, rows) for r in range(j * NAT_QROWS, (j + 1) * NAT_QROWS)]
        first = min(rs)
        n = -(-(max(rs) + kh - first) // NAT_QROWS) * NAT_QROWS
        first = min(first, rows - n)
        out.append((first, n))
    return out


def _nat_pair_tiles(rpb):
    cols = np.arange(GRID_W)
    c_start = np.clip(cols - NAT_KW // 2, 0, GRID_W - NAT_KW)
    col_mask = (cols[None, :] >= c_start[:, None]) & (cols[None, :] < c_start[:, None] + NAT_KW)
    idx = cols[None, :] - cols[:, None] + NAT_KW - 1
    sel = (idx[None] == np.arange(2 * NAT_KW - 1)[:, None, None]).astype(np.float32)
    tiles = jnp.einsum('dhab,bqk->dhaqk', rpb.astype(F32), sel, precision=HIGHEST)
    tiles = jnp.where(col_mask, tiles, -jnp.inf)
    neg = jnp.full(tiles.shape[:2] + (1, GRID_W, GRID_W), -jnp.inf, F32)
    ext = jnp.concatenate([neg, tiles, neg], axis=2)
    pairs = jnp.concatenate([ext[:, :, 0:NAT_NPAIR], ext[:, :, 1:NAT_NPAIR + 1]], axis=-1)
    return pairs.reshape(DEPTH, NAT_HEAD_PAIRS, 2, NAT_NPAIR, GRID_W, 2 * GRID_W)


def _nat_block_bias(bias_ref, hh, j, first, n, rows, lo):
    kh = min(NAT_KH, rows)
    neg = jnp.full((GRID_W, 2 * GRID_W), -jnp.inf, F32)
    row_blocks = []
    for ql in range(NAT_QROWS):
        qr = j * NAT_QROWS + ql
        rs = _nat_window_start(qr, rows)
        pieces = []
        for m in range(n // 2):
            k0 = first + 2 * m
            ok0 = rs <= k0 < rs + kh
            ok1 = rs <= k0 + 1 < rs + kh
            if not (ok0 or ok1):
                pieces.append(neg)
                continue
            t = bias_ref[hh, k0 - qr + NAT_KH]
            if ok0 and ok1:
                pieces.append(t)
            elif ok0:
                pieces.append(jnp.where(lo, t, -jnp.inf))
            else:
                pieces.append(jnp.where(lo, -jnp.inf, t))
        row_blocks.append(jnp.concatenate(pieces, axis=1))
    return jnp.concatenate(row_blocks, axis=0)


def _nat_attn_body(q_ref, k_ref, v_ref, kc_ref, vc_ref, bias_ref, o_ref, *, rows):
    lo = lax.broadcasted_iota(jnp.int32, (1, LANES), 1) < NAT_HEAD_DIM
    kc = kc_ref[...].astype(BF16)
    vc = vc_ref[...].astype(BF16)
    nq = NAT_QROWS * GRID_W
    for j, (first, n) in enumerate(_nat_blocks(rows)):
        q = q_ref[j * nq:(j + 1) * nq, :] * NAT_SCALE
        ks = slice(first * GRID_W, (first + n) * GRID_W)
        keys = jnp.concatenate([k_ref[ks, :].astype(BF16), kc], axis=0)
        v_ext = _values_with_ones(jnp.concatenate([v_ref[ks, :].astype(BF16), vc], axis=0))

        def bias(hh, j=j, first=first, n=n):
            return _nat_block_bias(bias_ref, hh, j, first, n, rows, lo)

        o_ref[j * nq:(j + 1) * nq, :] = _pair_attention(q, keys, v_ext, bias, lo)


def _nat_attn_call(q, k, v, cache_k, cache_v, bias, layer, bs, seq):
    rows = seq // GRID_W
    assert rows % NAT_QROWS == 0 and 2 * GRID_W == LANES
    spec = pl.BlockSpec((seq, LANES), lambda p, b: (b, p))
    cspec = pl.BlockSpec((None, None, cache_k.shape[2], LANES), lambda p, b: (b, layer, 0, p))
    return pl.pallas_call(
        functools.partial(_nat_attn_body, rows=rows),
        out_shape=jax.ShapeDtypeStruct(q.shape, F32),
        grid=(NAT_HEAD_PAIRS, bs),
        in_specs=[spec, spec, spec, cspec, cspec,
                  pl.BlockSpec((None, None) + bias.shape[2:], lambda p, b: (layer, p, 0, 0, 0, 0))],
        out_specs=spec,
        compiler_params=_cparams("parallel", "parallel"),
        name="nat_attn",
    )(q, k, v, cache_k, cache_v, bias)


def _ssd_body(*refs, seq, has_h0):
    zx_ref, dt_ref, cw_ref, cb_ref, dtb_ref, a_ref, d_ref, nw_ref = refs[:8]
    refs = refs[8:]
    if has_h0:
        h0_ref, y_ref = refs[:2]
        refs = refs[2:]
        hout_ref = None
    else:
        y_ref, hout_ref = refs[:2]
        refs = refs[2:]
        h0_ref = None
    xbc_s, y_s, eb_s, dtv_s, st_s, h_s = refs

    q = SSD_CHUNK
    nc = seq // q
    nh = SSD_HEADS
    hd = SSD_HEAD_DIM
    hpg = nh // SSD_GROUPS
    b_off = SSD_D
    c_off = SSD_D + SSD_GROUPS * SSD_STATE
    half = SSD_CONV // 2

    zeros8 = jnp.zeros((8, LANES), F32)
    for cblk in range(SSD_CONV_DIM // LANES):
        cs = slice(cblk * LANES, (cblk + 1) * LANES)
        xcol = slice(SSD_D + cblk * LANES, SSD_D + (cblk + 1) * LANES)
        xin = zx_ref[:, xcol]
        top = jnp.concatenate([zeros8, zx_ref[0:16, xcol]], axis=0)
        bot = jnp.concatenate([zx_ref[seq - 16:seq, xcol], zeros8], axis=0)
        acc = cb_ref[:, cs] + cw_ref[half:half + 1, cs] * xin
        acc_t = cb_ref[:, cs] + cw_ref[half:half + 1, cs] * top[8:16]
        acc_b = cb_ref[:, cs] + cw_ref[half:half + 1, cs] * bot[8:16]
        for kk in range(SSD_CONV):
            d = kk - half
            if d == 0:
                continue
            w = cw_ref[kk:kk + 1, cs]
            acc = acc + w * pltpu.roll(xin, (-d) % seq, axis=0)
            acc_t = acc_t + w * pltpu.roll(top, (-d) % 24, axis=0)[8:16]
            acc_b = acc_b + w * pltpu.roll(bot, (-d) % 24, axis=0)[8:16]
        xbc_s[:, cs] = _silu(acc)
        xbc_s[0:8, cs] = _silu(acc_t)
        xbc_s[seq - 8:seq, cs] = _silu(acc_b)

    lane = lax.broadcasted_iota(jnp.int32, (1, LANES), 1)
    xdt = dt_ref[...] + dtb_ref[...]
    dtv = jnp.maximum(xdt, 0.0) + jnp.log1p(jnp.exp(-jnp.abs(xdt)))
    dtv_s[...] = jnp.where(lane < 2 * nh, dtv, 0.0)

    ii = lax.broadcasted_iota(jnp.int32, (q, q), 0)
    jj = lax.broadcasted_iota(jnp.int32, (q, q), 1)
    lower = jj <= ii
    upper = jj >= ii
    is_fwd = lane < nh
    lo64 = lane < hd
    tri = jnp.concatenate([lower, upper], axis=1).astype(BF16)
    wide = 2 * SSD_D
    ek = lax.broadcasted_iota(jnp.int32, (2 * LANES, wide), 0) & (LANES - 1)
    el = lax.broadcasted_iota(jnp.int32, (2 * LANES, wide), 1) >> (hd.bit_length() - 1)
    expand = (ek == el).astype(BF16)
    grow = lax.broadcasted_iota(jnp.int32, (LANES, SSD_D), 0) >= SSD_STATE
    glane = lax.broadcasted_iota(jnp.int32, (LANES, SSD_D), 1) >= hpg * hd
    own = grow == glane

    def split2(x):
        hi = x.astype(BF16)
        return hi, (x - hi.astype(F32)).astype(BF16)

    def lane_expand(x):
        hi, mid = split2(x)
        return _dot(jnp.concatenate([hi, mid], axis=1), expand)

    def stage_cumsum(c):
        r0 = pl.multiple_of(c * q, q)
        k = dict(c=c, rs=pl.ds(r0, q))
        k['dt'] = dtv_s[k['rs'], :]
        da_c = k['dt'] * a_ref[...]
        rhs = jnp.concatenate([jnp.where(is_fwd, da_c, 0.0), jnp.where(is_fwd, 0.0, da_c)], axis=0)
        p0, p1 = split2(rhs)
        p2 = (rhs - p0.astype(F32) - p1.astype(F32)).astype(BF16)
        k['ac'] = _dot(tri, p0) + _dot(tri, p1) + _dot(tri, p2)
        return k

    def stage_expand(k):
        ac, dt_c = k['ac'], k['dt']
        k['ac_t'] = ac.T
        k['dt_t'] = dt_c.T
        last = jnp.where(is_fwd, ac[q - 1:q, :], ac[0:1, :])
        eb_s[k['rs'], :] = lane_expand(jnp.exp(ac))
        k['wb'] = lane_expand(dt_c * jnp.exp(last - ac))

    def stage_states(k):
        rs = k['rs']
        k['xs'] = xbc_s[rs, 0:SSD_D]
        ball = xbc_s[rs, b_off:c_off].astype(BF16)
        call = xbc_s[rs, c_off:c_off + SSD_GROUPS * SSD_STATE]
        xw = (jnp.concatenate([k['xs'], k['xs']], axis=1) * k['wb']).astype(BF16)
        st_s[k['c']] = _dot_tn(ball, xw)
        k['cb'] = [_dot_nt(jnp.where(lo64 if g == 0 else jnp.logical_not(lo64), call, 0.0).astype(BF16),
                           ball) for g in range(SSD_GROUPS)]

    def stage_intra(k, pr):
        ac, ac_t, dt_t = k['ac'], k['ac_t'], k['dt_t']
        ws = []
        for h in (2 * pr, 2 * pr + 1):
            seg_f = ac[:, h:h + 1] - ac_t[h:h + 1, :]
            seg_b = ac[:, nh + h:nh + h + 1] - ac_t[nh + h:nh + h + 1, :]
            w = k['cb'][h // hpg] * (
                jnp.exp(jnp.where(lower, seg_f, -jnp.inf)) * dt_t[h:h + 1, :]
                + jnp.exp(jnp.where(upper, seg_b, -jnp.inf)) * dt_t[nh + h:nh + h + 1, :])
            ws.append(w.astype(BF16))
        xp = k['xs'][:, pr * LANES:(pr + 1) * LANES]
        xbd = jnp.concatenate([jnp.where(lo64, xp, 0.0), jnp.where(lo64, 0.0, xp)],
                              axis=0).astype(BF16)
        y_s[k['rs'], pr * LANES:(pr + 1) * LANES] = _dot(jnp.concatenate(ws, axis=1), xbd)

    ilv = min(SSD_ILV, nc)

    def chunks(cg, carry):
        ks = [stage_cumsum(cg * ilv + j) for j in range(ilv)]
        for k in ks:
            stage_expand(k)
        for k in ks:
            stage_states(k)
        for pr in range(nh // 2):
            for k in ks:
                stage_intra(k, pr)
        return carry

    lax.fori_loop(0, nc // ilv, chunks, 0)

    h_s[...] = jnp.zeros(h_s.shape, F32)
    if has_h0:
        for hl in range(2 * nh):
            direction, h = divmod(hl, nh)
            g = h // hpg
            h_s[direction, g * SSD_STATE:(g + 1) * SSD_STATE, h * hd:(h + 1) * hd] = h0_ref[hl]

    def carry_states(kstep, carry):
        for direction in range(2):
            c = kstep if direction == 0 else nc - 1 - kstep
            r0 = pl.multiple_of(c * q, q)
            rs = pl.ds(r0, q)
            ds_ = slice(direction * SSD_D, (direction + 1) * SSD_D)
            edge = r0 + q - 1 if direction == 0 else r0
            call = xbc_s[rs, c_off:c_off + SSD_GROUPS * SSD_STATE].astype(BF16)
            h_in = h_s[direction]
            y_s[rs, :] += _dot(call, jnp.where(own, h_in, 0.0).astype(BF16)) * eb_s[rs, ds_]
            h_s[direction] = eb_s[pl.ds(edge, 1), ds_] * h_in + st_s[c][:, ds_]
        return carry

    lax.fori_loop(0, nc, carry_states, 0)

    y = y_s[...] + d_ref[...] * xbc_s[:, 0:SSD_D]
    y = y * _silu(zx_ref[:, 0:SSD_D])
    y_ref[...] = _rms(y, nw_ref[...])
    if hout_ref is not None:
        for hl in range(2 * nh):
            direction, h = divmod(hl, nh)
            g = h // hpg
            hout_ref[hl] = h_s[direction, g * SSD_STATE:(g + 1) * SSD_STATE, h * hd:(h + 1) * hd]


def _ssd_call(zx, dt, lw, layer, state, seq):
    has_h0 = state is not None
    rows = zx.shape[0]
    nseq = rows // seq
    nc = seq // SSD_CHUNK
    nst = 2 * SSD_HEADS

    def seqspec(w):
        return pl.BlockSpec((seq, w), lambda b: (b, 0))

    def par(shape):
        return _const_spec(shape, layer, single_buffer=False)

    in_specs = [seqspec(1024), seqspec(LANES), par((8, SSD_CONV_DIM)), par((1, SSD_CONV_DIM)),
                par((1, LANES)), par((1, LANES)), par((1, SSD_D)), par((1, SSD_D))]
    args = [zx, dt, lw['conv_w'], lw['conv_b'], lw['dt_bias'], lw['a'], lw['d'], lw['norm']]
    y_shape = jax.ShapeDtypeStruct((rows, SSD_D), F32)
    if has_h0:
        in_specs.append(pl.BlockSpec((None, None, nst, SSD_HEAD_DIM, SSD_STATE),
                                     lambda b: (b, layer, 0, 0, 0)))
        args.append(state)
        out_shape = y_shape
        out_specs = seqspec(SSD_D)
    else:
        out_shape = [y_shape, jax.ShapeDtypeStruct((nseq, nst, SSD_HEAD_DIM, SSD_STATE), F32)]
        out_specs = [seqspec(SSD_D),
                     pl.BlockSpec((None, nst, SSD_HEAD_DIM, SSD_STATE), lambda b: (b, 0, 0, 0))]
    return pl.pallas_call(
        functools.partial(_ssd_body, seq=seq, has_h0=has_h0),
        out_shape=out_shape,
        grid=(nseq,),
        in_specs=in_specs,
        out_specs=out_specs,
        scratch_shapes=[
            pltpu.VMEM((seq, SSD_CONV_DIM), F32),
            pltpu.VMEM((seq, SSD_D), F32),
            pltpu.VMEM((seq, 2 * SSD_D), F32),
            pltpu.VMEM((seq, LANES), F32),
            pltpu.VMEM((nc, SSD_GROUPS * SSD_STATE, 2 * SSD_D), F32),
            pltpu.VMEM((2, SSD_GROUPS * SSD_STATE, SSD_D), F32),
        ],
        compiler_params=_cparams("parallel"),
        name="ssd_mixer",
    )(*args)


def _s5_tables(a_re, a_im, log_dt, b_re, b_im, c_re, c_im):
    t = S5_T
    gh = S5_HGROUPS
    f32 = lambda x: x.astype(F32)
    lam_r, lam_i = f32(a_re), f32(a_im)
    step = jnp.exp(f32(log_dt))[..., None]
    xr, xi = lam_r * step, lam_i * step

    def powers(ks):
        ks = jnp.asarray(ks, F32)[None, None, :, None, None]
        mag = jnp.exp(xr[:, :, None] * ks)
        return mag * jnp.cos(xi[:, :, None] * ks), mag * jnp.sin(xi[:, :, None] * ks)

    pw_r, pw_i = powers(np.arange(t + 1))
    nr, ni = pw_r[:, :, 1] - 1.0, pw_i[:, :, 1]
    den = lam_r * lam_r + lam_i * lam_i
    fr = (nr * lam_r + ni * lam_i) / den
    fi = (ni * lam_r - nr * lam_i) / den
    br, bi = f32(b_re)[:, None], f32(b_im)[:, None]
    bb_r = fr[..., None] * br - fi[..., None] * bi
    bb_i = fr[..., None] * bi + fi[..., None] * br
    cr, ci = f32(c_re), f32(c_im)

    kern = []
    for direction in range(2):
        pr = pw_r[:, direction, :t][:, :, :, None, :]
        pi = pw_i[:, direction, :t][:, :, :, None, :]
        cpr, cpi = cr[:, None] * pr - ci[:, None] * pi, cr[:, None] * pi + ci[:, None] * pr
        kern.append(jnp.einsum('dtgcn,dgnk->dtgck', cpr, bb_r[:, direction], precision=HIGHEST)
                    - jnp.einsum('dtgcn,dgnk->dtgck', cpi, bb_i[:, direction], precision=HIGHEST))
    kf, kb = kern
    lag = [kb[:, -d] if d < 0 else (kf[:, 0] + kb[:, 0] if d == 0 else kf[:, d])
           for d in range(-(t - 1), t)]
    kt = jnp.stack(lag, axis=1).reshape(DEPTH, S5_LAGS, 2, gh, S5_GROUP_CH, S5_GROUP_CH)
    kt = jnp.transpose(kt, (0, 2, 1, 5, 3, 4)).reshape(DEPTH, 2, S5_LAGS, S5_GROUP_CH, LANES)

    def by_half(x):
        return jnp.transpose(x.reshape(DEPTH, -1, 2, gh, S5_STATE), (0, 2, 1, 3, 4))

    def dir_powers(direction, ks):
        pr, pi = powers(np.asarray(ks))
        return pr[:, direction], pi[:, direction]

    bt_r = jnp.transpose(bb_r.reshape(DEPTH, 2, 2, gh, S5_STATE, S5_GROUP_CH), (0, 1, 2, 5, 3, 4))
    bt_i = jnp.transpose(bb_i.reshape(DEPTH, 2, 2, gh, S5_STATE, S5_GROUP_CH), (0, 1, 2, 5, 3, 4))
    sw = []
    for direction, ks in ((0, [t - 1 - s for s in range(t)]), (1, list(range(t)))):
        pr, pi = dir_powers(direction, ks)
        pr = by_half(pr)[:, :, :, None]
        pi = by_half(pi)[:, :, :, None]
        wr, wi = bt_r[:, direction][:, :, None], bt_i[:, direction][:, :, None]
        sw += [pr * wr - pi * wi, pr * wi + pi * wr]
    sw = jnp.stack(sw, axis=3).reshape(DEPTH, 2, t, 4, S5_GROUP_CH, gh * S5_STATE)
    ct_r = jnp.transpose(cr.reshape(DEPTH, 2, gh, S5_GROUP_CH, S5_STATE), (0, 1, 4, 2, 3))
    ct_i = jnp.transpose(ci.reshape(DEPTH, 2, gh, S5_GROUP_CH, S5_STATE), (0, 1, 4, 2, 3))
    aw = []
    for direction, ks in ((0, list(range(1, t + 1))), (1, [t - k for k in range(t)])):
        pr, pi = dir_powers(direction, ks)
        pr = jnp.swapaxes(by_half(pr), 3, 4)[..., None]
        pi = jnp.swapaxes(by_half(pi), 3, 4)[..., None]
        wr, wi = ct_r[:, :, None], ct_i[:, :, None]
        aw += [wr * pr - wi * pi, -(wr * pi + wi * pr)]
    aw = jnp.stack(aw, axis=2).reshape(DEPTH, 2, 4, t, S5_STATE, LANES)

    apow = jnp.stack([pw_r[:, 0, t], pw_i[:, 0, t], pw_r[:, 1, t], pw_i[:, 1, t]], axis=1)
    seg_r, seg_i = powers(t * np.arange(S5_NC + 1))
    back_r, back_i = dir_powers(1, t * (S5_NC - 1 - np.arange(S5_NC)))
    pseg = jnp.stack([seg_r[:, 0, :S5_NC], seg_i[:, 0, :S5_NC], back_r, back_i], axis=1)
    nkb = S5_CW // LANES
    pseg = jnp.transpose(pseg.reshape(DEPTH, 4, S5_NC, nkb, LANES), (0, 1, 3, 2, 4))
    pseg = pseg.reshape(DEPTH, 4 * nkb, S5_NC, LANES)
    aseg = jnp.stack([seg_r[:, 0, S5_NC], seg_i[:, 0, S5_NC],
                      seg_r[:, 1, S5_NC], seg_i[:, 1, S5_NC]], axis=1)
    return (kt, sw, aw, apow.reshape(DEPTH, 1, S5_SW), pseg, aseg.reshape(DEPTH, 1, S5_SW))


def _s5_expand_operators(kt_ref, sw_ref, aw_ref, toep_s, sop_s, aop_s):
    gh = S5_HGROUPS
    sh_ch = S5_GROUP_CH.bit_length() - 1
    sh_st = S5_STATE.bit_length() - 1

    def group_mask(shape, row_shift, lane_shift):
        r = lax.broadcasted_iota(jnp.int32, shape, 0) >> row_shift
        c = lax.broadcasted_iota(jnp.int32, shape, 1) >> lane_shift
        return r == c

    def blockdiag(x, mask):
        return jnp.where(mask, jnp.concatenate([x] * gh, axis=0), 0.0).astype(BF16)

    m_kk = group_mask((LANES, LANES), sh_ch, sh_ch)
    m_ks = group_mask((LANES, gh * S5_STATE), sh_ch, sh_st)
    m_sk = group_mask((gh * S5_STATE, LANES), sh_st, sh_ch)
    cw = gh * S5_STATE
    for hf in range(2):
        lags = [blockdiag(kt_ref[hf, d], m_kk) for d in range(S5_LAGS)]
        for s in range(S5_T):
            for t in range(S5_T):
                toep_s[hf, s * LANES:(s + 1) * LANES, t * LANES:(t + 1) * LANES] = lags[t - s + S5_T - 1]
            for comp in range(4):
                sop_s[hf, s * LANES:(s + 1) * LANES, comp * cw:(comp + 1) * cw] = blockdiag(
                    sw_ref[hf, s, comp], m_ks)
        for comp in range(4):
            for t in range(S5_T):
                aop_s[hf, comp * cw:(comp + 1) * cw, t * LANES:(t + 1) * LANES] = blockdiag(
                    aw_ref[hf, comp, t], m_sk)


def _s5_body(up_ref, us_ref, kt_ref, sw_ref, aw_ref, at_ref, pseg_ref, aseg_ref, h0_ref,
             y_ref, hout_ref, toep_s, sop_s, aop_s, st_s, hin_s, yacc_s, *, nblk_p, nseg_s):
    i = pl.program_id(0)

    @pl.when(i == 0)
    def _():
        _s5_expand_operators(kt_ref, sw_ref, aw_ref, toep_s, sop_s, aop_s)

    cw = S5_CW
    nkb = cw // LANES
    is_p = i < nblk_p

    def time_rows(s):
        return pl.ds(s, S5_ROWS, stride=S5_T)

    for hf in range(2):
        xh = jnp.concatenate(
            [jnp.where(is_p, up_ref[hf, time_rows(s), :], us_ref[hf, time_rows(s), :])
             for s in range(S5_T)], axis=1).astype(BF16)
        yacc_s[hf] = _dot(xh, toep_s[hf])
        sh = _dot(xh, sop_s[hf])
        for comp in range(4):
            for k in range(nkb // 2):
                col = (comp * (nkb // 2) + k) * LANES
                st_s[comp * nkb + hf * (nkb // 2) + k] = sh[:, col:col + LANES]

    def load(ref, comp, rows):
        return jnp.concatenate([ref[comp * nkb + k, rows, :] for k in range(nkb)], axis=1)

    def store(ref, comp, rows, val):
        for k in range(nkb):
            ref[comp * nkb + k, rows, :] = val[:, k * LANES:(k + 1) * LANES]

    def cmul(ar, ai, hr, hi):
        return ar * hr - ai * hi, ar * hi + ai * hr

    at = at_ref[...]
    ar_f, ai_f, ar_b, ai_b = [at[:, k * cw:(k + 1) * cw] for k in range(4)]

    def step(c, carry):
        hfr, hfi, hbr, hbi = carry
        rf = pl.ds(c, S5_HB, stride=S5_NC)
        rb = pl.ds(S5_NC - 1 - c, S5_HB, stride=S5_NC)
        for comp, val, rows in ((0, hfr, rf), (1, hfi, rf), (2, hbr, rb), (3, hbi, rb)):
            store(hin_s, comp, rows, val)
        fr, fi = cmul(ar_f, ai_f, hfr, hfi)
        br, bi = cmul(ar_b, ai_b, hbr, hbi)
        return (fr + load(st_s, 0, rf), fi + load(st_s, 1, rf),
                br + load(st_s, 2, rb), bi + load(st_s, 3, rb))

    zero = jnp.zeros((S5_HB, cw), F32)
    fin = lax.fori_loop(0, S5_NC, step, (zero, zero, zero, zero))
    hout_ref[...] = jnp.concatenate(fin, axis=-1)

    @pl.when(i >= nblk_p)
    def _():
        h0 = h0_ref[...]
        aseg = aseg_ref[...]
        sr_f, si_f, sr_b, si_b = [aseg[:, k * cw:(k + 1) * cw] for k in range(4)]
        ent = [[None] * S5_HB for _ in range(4)]
        for s in range(S5_HB // nseg_s):
            hr, hi = h0[s:s + 1, 0:cw], h0[s:s + 1, cw:2 * cw]
            for j in range(nseg_s):
                v = s * nseg_s + j
                ent[0][v], ent[1][v] = hr, hi
                hr, hi = cmul(sr_f, si_f, hr, hi)
                hr, hi = hr + fin[0][v:v + 1], hi + fin[1][v:v + 1]
            hr, hi = h0[s:s + 1, 2 * cw:3 * cw], h0[s:s + 1, 3 * cw:4 * cw]
            for j in range(nseg_s - 1, -1, -1):
                v = s * nseg_s + j
                ent[2][v], ent[3][v] = hr, hi
                hr, hi = cmul(sr_b, si_b, hr, hi)
                hr, hi = hr + fin[2][v:v + 1], hi + fin[3][v:v + 1]
        for direction in range(2):
            for k in range(nkb):
                kr = (2 * direction) * nkb + k
                ki = (2 * direction + 1) * nkb + k
                pr, pi = pseg_ref[kr], pseg_ref[ki]
                for v in range(S5_HB):
                    rows = slice(v * S5_NC, (v + 1) * S5_NC)
                    er = ent[2 * direction][v][:, k * LANES:(k + 1) * LANES]
                    ei = ent[2 * direction + 1][v][:, k * LANES:(k + 1) * LANES]
                    dr, di = cmul(pr, pi, er, ei)
                    hin_s[kr, rows, :] += dr
                    hin_s[ki, rows, :] += di

    for hf in range(2):
        hh = jnp.concatenate(
            [hin_s[comp * nkb + hf * (nkb // 2) + k] for comp in range(4) for k in range(nkb // 2)],
            axis=1).astype(BF16)
        y = yacc_s[hf] + _dot(hh, aop_s[hf])
        for t in range(S5_T):
            y_ref[hf, time_rows(t), :] = y[:, t * LANES:(t + 1) * LANES]


def _s5_call(u_p, u_s, tables, layer, h0, seq_p, seq_s):
    kt, sw, aw, apow, pseg, aseg = tables
    seg_tok = S5_T * S5_NC
    assert seq_p == seg_tok and seq_s % seg_tok == 0 and S5_HB % (seq_s // seg_tok) == 0
    blk_tok = S5_ROWS * S5_T
    nblk_p = u_p.shape[1] // blk_tok
    nblk = nblk_p + u_s.shape[1] // blk_tok
    uh = S5_D // LANES

    def par(a):
        return _const_spec(a.shape[1:], layer)

    nlb = S5_SW // LANES
    return pl.pallas_call(
        functools.partial(_s5_body, nblk_p=nblk_p, nseg_s=seq_s // seg_tok),
        out_shape=[jax.ShapeDtypeStruct((uh, nblk * blk_tok, LANES), F32),
                   jax.ShapeDtypeStruct((nblk, S5_HB, S5_SW), F32)],
        grid=(nblk,),
        in_specs=[pl.BlockSpec((uh, blk_tok, LANES), lambda i: (0, jnp.minimum(i, nblk_p - 1), 0)),
                  pl.BlockSpec((uh, blk_tok, LANES), lambda i: (0, jnp.maximum(i - nblk_p, 0), 0)),
                  par(kt), par(sw), par(aw), par(apow), par(pseg), par(aseg),
                  pl.BlockSpec((None, None, S5_HB, S5_SW), lambda i: (layer, i, 0, 0))],
        out_specs=[pl.BlockSpec((uh, blk_tok, LANES), lambda i: (0, i, 0)),
                   pl.BlockSpec((None, S5_HB, S5_SW), lambda i: (i, 0, 0))],
        scratch_shapes=[pltpu.VMEM((2, S5_KW, S5_KW), BF16), pltpu.VMEM((2, S5_KW, S5_SH), BF16),
                        pltpu.VMEM((2, S5_SH, S5_KW), BF16),
                        pltpu.VMEM((nlb, S5_ROWS, LANES), F32), pltpu.VMEM((nlb, S5_ROWS, LANES), F32),
                        pltpu.VMEM((2, S5_ROWS, S5_KW), F32)],
        compiler_params=_cparams("arbitrary"),
        name="s5_mixer",
    )(u_p, u_s, kt, sw, aw, apow, pseg, aseg, h0)


def _gelu_tanh(x):
    return 0.5 * x * (1.0 + jnp.tanh(math.sqrt(2.0 / math.pi) * (x + 0.044715 * (x * x * x))))


def _out_body(*refs, final):
    (x_ref, yssd_ref, ynat_ref, u_ref, y5_ref, g1_ref, m2_ref, d5_ref,
     wglu_ref, bglu_ref, wout_ref, nm_ref, w1_ref, w2_ref) = refs[:14]
    if final:
        nf_ref, o_ref = refs[14:]
    else:
        (o_ref,) = refs[14:]
    uh = S5_D // LANES
    for sb in range(ROW_TILE // TOK_TILE):
        rows = slice(sb * TOK_TILE, (sb + 1) * TOK_TILE)
        y5_in = jnp.concatenate([y5_ref[hf, rows, :] for hf in range(uh)], axis=1)
        u = jnp.concatenate([u_ref[hf, rows, :] for hf in range(uh)], axis=1)
        g = _gelu_tanh(y5_in + d5_ref[...] * u)
        y5 = g * jax.nn.sigmoid(_dot(g.astype(BF16), wglu_ref[...]) + bglu_ref[...])
        mix = (_dot(yssd_ref[rows, :].astype(BF16), wout_ref[0:SSD_D, :])
               + _dot(y5.astype(BF16), wout_ref[SSD_D:SSD_D + S5_D, :])
               + _dot(ynat_ref[rows, :].astype(BF16), wout_ref[SSD_D + S5_D:, :]))
        x = x_ref[rows, :] + g1_ref[...] * mix
        h2 = _rms(x, nm_ref[...]) * (1.0 + m2_ref[:, D_MODEL:2 * D_MODEL]) + m2_ref[:, 0:D_MODEL]
        f = jnp.maximum(_dot(h2.astype(BF16), w1_ref[...]), 0.0)
        f = (f * f).astype(BF16)
        x = x + m2_ref[:, 2 * D_MODEL:3 * D_MODEL] * _dot(f, w2_ref[...])
        if final:
            x = _rms(x, nf_ref[...])
        o_ref[rows, :] = x


def _out_call(x, y_ssd, y_nat, u, y5, y5_tile_off, cond_idx, mods4, lw, norm_f, layer, final):
    rows = x.shape[0]
    uh = S5_D // LANES

    def tile(w):
        return pl.BlockSpec((ROW_TILE, w), lambda i: (i, 0))

    def par(shape):
        return _const_spec(shape, layer)

    in_specs = [
        tile(D_MODEL), tile(SSD_D), tile(NAT_D),
        pl.BlockSpec((uh, ROW_TILE, LANES), lambda i: (0, i, 0)),
        pl.BlockSpec((uh, ROW_TILE, LANES), lambda i: (0, i + y5_tile_off, 0)),
        pl.BlockSpec((None, None, 1, D_MODEL), lambda i: (layer, cond_idx(i), 0, 2)),
        pl.BlockSpec((None, None, 1, 3 * D_MODEL), lambda i: (layer, cond_idx(i), 0, 1)),
        par((1, S5_D)), par((S5_D, S5_D)), par((1, S5_D)),
        par((D_MODEL, D_MODEL)), par((1, D_MODEL)),
        par((D_MODEL, D_FF)), par((D_FF, D_MODEL))]
    args = [x, y_ssd, y_nat, u, y5, mods4, mods4, lw['s5_d'], lw['w_glu'], lw['b_glu'],
            lw['w_out'], lw['norm_mlp'], lw['w_ff1'], lw['w_ff2']]
    if final:
        in_specs.append(pl.BlockSpec((1, D_MODEL), lambda i: (0, 0)))
        args.append(norm_f.reshape(1, D_MODEL))
    return pl.pallas_call(
        functools.partial(_out_body, final=final),
        out_shape=jax.ShapeDtypeStruct((rows, D_MODEL), F32),
        grid=(rows // ROW_TILE,),
        in_specs=in_specs,
        out_specs=tile(D_MODEL),
        compiler_params=_cparams("parallel"),
        name="out_mlp",
    )(*args)


def _pack_w_in(w_in):
    o_dt = SSD_D + SSD_CONV_DIM
    o_u = o_dt + 2 * SSD_HEADS
    pad = IN_PACKED - w_in.shape[-1]
    return jnp.concatenate(
        [w_in[..., :o_dt], w_in[..., o_u:], w_in[..., o_dt:o_u],
         jnp.zeros(w_in.shape[:-1] + (pad,), w_in.dtype)], axis=-1).astype(BF16)


def _lane_pad(x):
    return jnp.pad(x, [(0, 0)] * (x.ndim - 1) + [(0, LANES - x.shape[-1])])


def kernel(x_prompt, x_sample, cache_nat_k, cache_nat_v, state_ssd, state_s5_re, state_s5_im,
           c, c_ctx, w_mod, b_mod, norm_mix, norm_mlp, w_in, ssd_conv_w, ssd_conv_b,
           ssd_dt_bias, ssd_a_log, ssd_d, ssd_norm, s5_a_re, s5_a_im, s5_log_dt,
           s5_b_re, s5_b_im, s5_c_re, s5_c_im, s5_d, s5_w_glu, s5_b_glu, nat_rpb,
           w_out, w_ff1, w_ff2, norm_f):
    bp, seq_p, _ = x_prompt.shape
    bs, seq_s, _ = x_sample.shape
    assert seq_p == TOK_TILE and seq_s == 4 * TOK_TILE
    rows_p = bp * seq_p
    rows_s = bs * seq_s

    ncp = -(-(1 + bs) // 8) * 8
    cond = jnp.concatenate([c_ctx[None, :], c, jnp.zeros((ncp - 1 - bs, D_MODEL), F32)], axis=0)
    mods = _mods_call(cond, w_mod, b_mod)
    mods4 = mods.reshape(DEPTH, ncp, 1, N_MOD * D_MODEL)

    w_in_p = _pack_w_in(w_in)
    ssd_w = {
        'conv_w': jnp.pad(ssd_conv_w.astype(F32), [(0, 0), (0, 8 - SSD_CONV), (0, 0)]),
        'conv_b': ssd_conv_b.astype(F32).reshape(DEPTH, 1, SSD_CONV_DIM),
        'dt_bias': _lane_pad(ssd_dt_bias.astype(F32).reshape(DEPTH, 1, 2 * SSD_HEADS)),
        'a': _lane_pad(-jnp.exp(ssd_a_log.astype(F32)).reshape(DEPTH, 1, 2 * SSD_HEADS)),
        'd': jnp.repeat(ssd_d.astype(F32), SSD_HEAD_DIM, axis=-1).reshape(DEPTH, 1, SSD_D),
        'norm': ssd_norm.astype(F32).reshape(DEPTH, 1, SSD_D),
    }
    out_w = {
        's5_d': s5_d.astype(F32).reshape(DEPTH, 1, S5_D),
        'w_glu': s5_w_glu.astype(BF16),
        'b_glu': s5_b_glu.astype(F32).reshape(DEPTH, 1, S5_D),
        'w_out': w_out.astype(BF16),
        'norm_mlp': norm_mlp.astype(F32).reshape(DEPTH, 1, D_MODEL),
        'w_ff1': w_ff1.astype(BF16),
        'w_ff2': w_ff2.astype(BF16),
    }
    s5_tabs = _s5_tables(s5_a_re, s5_a_im, s5_log_dt, s5_b_re, s5_b_im, s5_c_re, s5_c_im)
    nat_bias = _nat_pair_tiles(nat_rpb)

    cache_k = cache_nat_k.reshape(bs, DEPTH, -1, NAT_D)
    cache_v = cache_nat_v.reshape(bs, DEPTH, -1, NAT_D)
    st_ssd = jnp.swapaxes(state_ssd, -1, -2).reshape(bs, DEPTH, 2 * SSD_HEADS, SSD_STATE, SSD_HEAD_DIM)
    s5_tok = S5_ROWS * S5_T
    assert rows_p % s5_tok == 0 and rows_s % s5_tok == 0 and s5_tok % seq_s == 0
    nseq_s = s5_tok // seq_s
    st5 = jnp.stack([state_s5_re[:, :, 0], state_s5_im[:, :, 0],
                     state_s5_re[:, :, 1], state_s5_im[:, :, 1]], axis=2)
    st5 = jnp.transpose(st5.astype(F32).reshape(bs // nseq_s, nseq_s, DEPTH, S5_SW), (2, 0, 1, 3))
    st5 = jnp.pad(st5, [(0, 0), (rows_p // s5_tok, 0), (0, S5_HB - nseq_s), (0, 0)])

    assert rows_p % ROW_TILE == 0 and seq_s % ROW_TILE == 0
    tiles_per_seq = seq_s // ROW_TILE
    cond_p = lambda i: 0
    cond_s = lambda i: 1 + i // tiles_per_seq
    x_p = x_prompt.reshape(rows_p, D_MODEL)
    x_s = x_sample.reshape(rows_s, D_MODEL)
    new_k, new_v, new_ssd, new_s5 = [], [], [], []
    for l in range(DEPTH):
        final = l == DEPTH - 1
        zx_p, dt_p, u_p, q_p, k_p, v_p = _inproj_call(x_p, cond_p, mods4, norm_mix, w_in_p, l)
        zx_s, dt_s, u_s, q_s, k_s, v_s = _inproj_call(x_s, cond_s, mods4, norm_mix, w_in_p, l)
        new_k.append(k_p.reshape(bp, seq_p, NAT_D))
        new_v.append(v_p.reshape(bp, seq_p, NAT_D))

        y_nat_p = _ctx_attn_call(q_p, k_p, v_p)
        y_nat_s = _nat_attn_call(q_s, k_s, v_s, cache_k, cache_v, nat_bias, l, bs, seq_s)

        y_ssd_p, ssd_l = _ssd_call(zx_p, dt_p, ssd_w, l, None, seq_p)
        y_ssd_s = _ssd_call(zx_s, dt_s, ssd_w, l, st_ssd, seq_s)
        new_ssd.append(ssd_l)

        y5, s5_l = _s5_call(u_p, u_s, s5_tabs, l, st5, seq_p, seq_s)
        new_s5.append(s5_l[:rows_p // s5_tok])

        x_p = _out_call(x_p, y_ssd_p, y_nat_p, u_p, y5, 0, cond_p, mods4, out_w, norm_f, l, final)
        x_s = _out_call(x_s, y_ssd_s, y_nat_s, u_s, y5, rows_p // ROW_TILE, cond_s,
                        mods4, out_w, norm_f, l, final)

    y_prompt = x_p.reshape(bp, seq_p, D_MODEL)
    y_sample = x_s.reshape(bs, seq_s, D_MODEL)
    out_k = jnp.stack(new_k, axis=1).reshape(bp, DEPTH, seq_p, NAT_HEADS, NAT_HEAD_DIM)
    out_v = jnp.stack(new_v, axis=1).reshape(bp, DEPTH, seq_p, NAT_HEADS, NAT_HEAD_DIM)
    out_ssd = jnp.swapaxes(jnp.stack(new_ssd, axis=1), -1, -2)
    out_ssd = out_ssd.reshape(bp, DEPTH, 2, SSD_HEADS, SSD_HEAD_DIM, SSD_STATE)
    s5 = jnp.stack(new_s5, axis=0)[:, :, :s5_tok // seq_p]
    s5 = s5.reshape(DEPTH, bp, 4, S5_GROUPS, S5_STATE)
    s5 = jnp.transpose(s5, (1, 0, 2, 3, 4))
    out_re = s5[:, :, 0::2]
    out_im = s5[:, :, 1::2]
    return y_prompt, y_sample, out_k, out_v, out_ssd, out_re, out_im
```

```python
import functools
import math

import numpy as np
import jax
import jax.numpy as jnp
from jax import lax
from jax.experimental import pallas as pl
from jax.experimental.pallas import tpu as pltpu

F32 = jnp.float32
BF16 = jnp.bfloat16
HIGHEST = lax.Precision.HIGHEST

D_MODEL = 1024
DEPTH = 4
GRID_W = 64
SSD_HEADS = 6
SSD_HEAD_DIM = 64
SSD_D = SSD_HEADS * SSD_HEAD_DIM
SSD_GROUPS = 2
SSD_STATE = 64
SSD_CONV = 5
SSD_CHUNK = 128
SSD_CONV_DIM = SSD_D + 2 * SSD_GROUPS * SSD_STATE
SSD_ILV = 4
S5_GROUPS = 16
S5_GROUP_CH = 16
S5_D = S5_GROUPS * S5_GROUP_CH
S5_STATE = 64
NAT_HEADS = 6
NAT_HEAD_DIM = 64
NAT_D = NAT_HEADS * NAT_HEAD_DIM
NAT_KH = 8
NAT_KW = 16
D_FF = 4 * D_MODEL
N_MOD = 6
EPS = 1e-6

LANES = 128
TOK_TILE = 256
ROW_TILE = 512
IN_COLS = SSD_D + SSD_CONV_DIM + 2 * SSD_HEADS + S5_D + 3 * NAT_D
S5_T = 8
S5_HGROUPS = LANES // S5_GROUP_CH
S5_KW = S5_T * LANES
S5_CW = S5_GROUPS * S5_STATE
S5_SW = 4 * S5_CW
S5_SH = S5_SW // 2
S5_NC = 32
S5_HB = 8
S5_ROWS = S5_HB * S5_NC
S5_LAGS = 2 * S5_T - 1
VMEM_LIMIT = 56 * 1024 * 1024


def _cparams(*sem):
    return pltpu.CompilerParams(dimension_semantics=sem, vmem_limit_bytes=VMEM_LIMIT)


def _dot(a, b, precision=None):
    return jnp.dot(a, b, preferred_element_type=F32, precision=precision)


def _dot_nt(a, b, precision=None):
    return lax.dot_general(a, b, (((1,), (1,)), ((), ())), preferred_element_type=F32,
                           precision=precision)


def _dot_tn(a, b, precision=None):
    return lax.dot_general(a, b, (((0,), (0,)), ((), ())), preferred_element_type=F32,
                           precision=precision)


def _silu(x):
    hx = 0.5 * x
    return hx + hx * jnp.tanh(hx)


def _rms(x, g):
    return x * lax.rsqrt(jnp.mean(x * x, axis=-1, keepdims=True) + EPS) * g


def _const_spec(shape, layer, single_buffer=True):
    return pl.BlockSpec((None,) + tuple(shape), lambda *_: (layer,) + (0,) * len(shape),
                        pipeline_mode=pl.Buffered(1) if single_buffer else None)


def _mods_body(cond_ref, w_ref, b_ref, o_ref):
    s = _silu(cond_ref[...])
    o_ref[...] = _dot(s.astype(BF16), w_ref[...].astype(BF16)) + b_ref[...]


def _mods_call(cond, w_mod, b_mod):
    ncp = cond.shape[0]
    blk = D_MODEL
    return pl.pallas_call(
        _mods_body,
        out_shape=jax.ShapeDtypeStruct((DEPTH, ncp, N_MOD * D_MODEL), F32),
        grid=(DEPTH, N_MOD),
        in_specs=[
            pl.BlockSpec((ncp, D_MODEL), lambda l, j: (0, 0)),
            pl.BlockSpec((None, D_MODEL, blk), lambda l, j: (l, 0, j)),
            pl.BlockSpec((None, 1, blk), lambda l, j: (l, 0, j)),
        ],
        out_specs=pl.BlockSpec((None, ncp, blk), lambda l, j: (l, 0, j)),
        compiler_params=_cparams("arbitrary", "arbitrary"),
        name="adaln_mods",
    )(cond, w_mod, b_mod.reshape(DEPTH, 1, N_MOD * D_MODEL))


def _inproj_body(x_ref, mod_ref, g_ref, w_ref, zx_ref, dt_ref, u_ref, q_ref, k_ref, v_ref):
    x = x_ref[...]
    h = _rms(x, g_ref[...]) * (1.0 + mod_ref[:, D_MODEL:2 * D_MODEL]) + mod_ref[:, 0:D_MODEL]
    p = _dot(h.astype(BF16), w_ref[...])
    o_dt = SSD_D + SSD_CONV_DIM
    zx_ref[...] = p[:, 0:o_dt]
    dt_ref[...] = p[:, o_dt:o_dt + LANES]
    rest = p[:, o_dt + 2 * SSD_HEADS:]
    for hf in range(S5_D // LANES):
        u_ref[hf] = rest[:, hf * LANES:(hf + 1) * LANES]
    q_ref[...] = rest[:, S5_D:S5_D + NAT_D]
    k_ref[...] = rest[:, S5_D + NAT_D:S5_D + 2 * NAT_D]
    v_ref[...] = rest[:, S5_D + 2 * NAT_D:S5_D + 3 * NAT_D]


def _inproj_call(x, cond_idx, mods4, norm_mix, w_in_p, layer):
    rows = x.shape[0]

    def tile(w):
        return pl.BlockSpec((ROW_TILE, w), lambda i: (i, 0))

    def out(w):
        return jax.ShapeDtypeStruct((rows, w), F32)

    uh = S5_D // LANES
    return pl.pallas_call(
        _inproj_body,
        out_shape=[out(1024), out(LANES), jax.ShapeDtypeStruct((uh, rows, LANES), F32),
                   out(NAT_D), out(NAT_D), out(NAT_D)],
        grid=(rows // ROW_TILE,),
        in_specs=[
            tile(D_MODEL),
            pl.BlockSpec((None, None, 1, 2 * D_MODEL), lambda i: (layer, cond_idx(i), 0, 0)),
            _const_spec((1, D_MODEL), layer, single_buffer=False),
            _const_spec((D_MODEL, IN_COLS), layer),
        ],
        out_specs=[tile(1024), tile(LANES), pl.BlockSpec((uh, ROW_TILE, LANES), lambda i: (0, i, 0)),
                   tile(NAT_D), tile(NAT_D), tile(NAT_D)],
        compiler_params=_cparams("parallel"),
        name="in_proj",
    )(x, mods4, norm_mix.reshape(DEPTH, 1, D_MODEL), w_in_p)


NAT_SCALE = NAT_HEAD_DIM ** -0.5


def _values_with_ones(v):
    return jnp.concatenate([v, jnp.ones(v.shape, BF16)], axis=1)


def _pair_attention(q, keys, v_ext, bias, lo):
    outs = []
    for hh in range(2):
        qm = jnp.where(lo if hh == 0 else jnp.logical_not(lo), q, 0.0).astype(BF16)
        s = _dot_nt(qm, keys)
        if bias is not None:
            b = bias(hh)
            nb = b.shape[1]
            s = jnp.concatenate([s[:, 0:nb] + b, s[:, nb:]], axis=1)
        p = jnp.exp(s - jnp.max(s, axis=-1, keepdims=True)).astype(BF16)
        o = _dot(p, v_ext)
        outs.append(o[:, 0:LANES] / o[:, LANES:2 * LANES])
    return jnp.where(lo, outs[0], outs[1])


def _ctx_attn_body(q_ref, k_ref, v_ref, o_ref):
    lo = lax.broadcasted_iota(jnp.int32, (1, LANES), 1) < NAT_HEAD_DIM
    for p in range(NAT_D // LANES):
        cs = slice(p * LANES, (p + 1) * LANES)
        o_ref[:, cs] = _pair_attention(q_ref[:, cs] * NAT_SCALE, k_ref[:, cs].astype(BF16),
                                       _values_with_ones(v_ref[:, cs].astype(BF16)), None, lo)


def _ctx_attn_call(q, k, v):
    rows = q.shape[0]
    spec = pl.BlockSpec((TOK_TILE, NAT_D), lambda b: (b, 0))
    return pl.pallas_call(
        _ctx_attn_body,
        out_shape=jax.ShapeDtypeStruct((rows, NAT_D), F32),
        grid=(rows // TOK_TILE,),
        in_specs=[spec, spec, spec],
        out_specs=spec,
        compiler_params=_cparams("parallel"),
        name="ctx_attn",
    )(q, k, v)


NAT_QROWS = 4
NAT_HEAD_PAIRS = NAT_HEADS // 2
NAT_NPAIR = 2 * NAT_KH


def _nat_window_start(r, rows):
    kh = min(NAT_KH, rows)
    return int(np.clip(r - kh // 2, 0, rows - kh))


def _nat_blocks(rows):
    kh = min(NAT_KH, rows)
    out = []
    for j in range(rows // NAT_QROWS):
        rs = [_nat_window_start(r, rows) for r in range(j * NAT_QROWS, (j + 1) * NAT_QROWS)]
        first = min(rs)
        n = -(-(max(rs) + kh - first) // NAT_QROWS) * NAT_QROWS
        first = min(first, rows - n)
        out.append((first, n))
    return out


def _nat_pair_tiles(rpb):
    cols = np.arange(GRID_W)
    c_start = np.clip(cols - NAT_KW // 2, 0, GRID_W - NAT_KW)
    col_mask = (cols[None, :] >= c_start[:, None]) & (cols[None, :] < c_start[:, None] + NAT_KW)
    idx = cols[None, :] - cols[:, None] + NAT_KW - 1
    sel = (idx[None] == np.arange(2 * NAT_KW - 1)[:, None, None]).astype(np.float32)
    tiles = jnp.einsum('dhab,bqk->dhaqk', rpb.astype(F32), sel, precision=HIGHEST)
    tiles = jnp.where(col_mask, tiles, -jnp.inf)
    neg = jnp.full(tiles.shape[:2] + (1, GRID_W, GRID_W), -jnp.inf, F32)
    ext = jnp.concatenate([neg, tiles, neg], axis=2)
    pairs = jnp.concatenate([ext[:, :, 0:NAT_NPAIR], ext[:, :, 1:NAT_NPAIR + 1]], axis=-1)
    return pairs.reshape(DEPTH, NAT_HEAD_PAIRS, 2, NAT_NPAIR, GRID_W, 2 * GRID_W)


def _nat_block_bias(bias_ref, hh, j, first, n, rows, lo):
    kh = min(NAT_KH, rows)
    neg = jnp.full((GRID_W, 2 * GRID_W), -jnp.inf, F32)
    row_blocks = []
    for ql in range(NAT_QROWS):
        qr = j * NAT_QROWS + ql
        rs = _nat_window_start(qr, rows)
        pieces = []
        for m in range(n // 2):
            k0 = first + 2 * m
            ok0 = rs <= k0 < rs + kh
            ok1 = rs <= k0 + 1 < rs + kh
            if not (ok0 or ok1):
                pieces.append(neg)
                continue
            t = bias_ref[hh, k0 - qr + NAT_KH]
            if ok0 and ok1:
                pieces.append(t)
            elif ok0:
                pieces.append(jnp.where(lo, t, -jnp.inf))
            else:
                pieces.append(jnp.where(lo, -jnp.inf, t))
        row_blocks.append(jnp.concatenate(pieces, axis=1))
    return jnp.concatenate(row_blocks, axis=0)


def _nat_attn_body(q_ref, k_ref, v_ref, kc_ref, vc_ref, bias_ref, o_ref, *, rows):
    lo = lax.broadcasted_iota(jnp.int32, (1, LANES), 1) < NAT_HEAD_DIM
    kc = kc_ref[...].astype(BF16)
    vc = vc_ref[...].astype(BF16)
    nq = NAT_QROWS * GRID_W
    for j, (first, n) in enumerate(_nat_blocks(rows)):
        q = q_ref[j * nq:(j + 1) * nq, :] * NAT_SCALE
        ks = slice(first * GRID_W, (first + n) * GRID_W)
        keys = jnp.concatenate([k_ref[ks, :].astype(BF16), kc], axis=0)
        v_ext = _values_with_ones(jnp.concatenate([v_ref[ks, :].astype(BF16), vc], axis=0))

        def bias(hh, j=j, first=first, n=n):
            return _nat_block_bias(bias_ref, hh, j, first, n, rows, lo)

        o_ref[j * nq:(j + 1) * nq, :] = _pair_attention(q, keys, v_ext, bias, lo)


def _nat_attn_call(q, k, v, cache_k, cache_v, bias, layer, bs, seq):
    rows = seq // GRID_W
    assert rows % NAT_QROWS == 0 and 2 * GRID_W == LANES
    spec = pl.BlockSpec((seq, LANES), lambda p, b: (b, p))
    cspec = pl.BlockSpec((None, None, cache_k.shape[2], LANES), lambda p, b: (b, layer, 0, p))
    return pl.pallas_call(
        functools.partial(_nat_attn_body, rows=rows),
        out_shape=jax.ShapeDtypeStruct(q.shape, F32),
        grid=(NAT_HEAD_PAIRS, bs),
        in_specs=[spec, spec, spec, cspec, cspec,
                  pl.BlockSpec((None, None) + bias.shape[2:], lambda p, b: (layer, p, 0, 0, 0, 0))],
        out_specs=spec,
        compiler_params=_cparams("parallel", "parallel"),
        name="nat_attn",
    )(q, k, v, cache_k, cache_v, bias)


def _ssd_body(*refs, seq, has_h0):
    zx_ref, dt_ref, cw_ref, cb_ref, dtb_ref, a_ref, d_ref, nw_ref = refs[:8]
    refs = refs[8:]
    if has_h0:
        h0_ref, y_ref = refs[:2]
        refs = refs[2:]
        hout_ref = None
    else:
        y_ref, hout_ref = refs[:2]
        refs = refs[2:]
        h0_ref = None
    xbc_s, y_s, eb_s, dtv_s, st_s, h_s, tr_s = refs

    q = SSD_CHUNK
    nc = seq // q
    nh = SSD_HEADS
    hd = SSD_HEAD_DIM
    hpg = nh // SSD_GROUPS
    b_off = SSD_D
    c_off = SSD_D + SSD_GROUPS * SSD_STATE
    half = SSD_CONV // 2

    zeros8 = jnp.zeros((8, LANES), F32)
    for cblk in range(SSD_CONV_DIM // LANES):
        cs = slice(cblk * LANES, (cblk + 1) * LANES)
        xcol = slice(SSD_D + cblk * LANES, SSD_D + (cblk + 1) * LANES)
        xin = zx_ref[:, xcol]
        top = jnp.concatenate([zeros8, zx_ref[0:16, xcol]], axis=0)
        bot = jnp.concatenate([zx_ref[seq - 16:seq, xcol], zeros8], axis=0)
        acc = cb_ref[:, cs] + cw_ref[half:half + 1, cs] * xin
        acc_t = cb_ref[:, cs] + cw_ref[half:half + 1, cs] * top[8:16]
        acc_b = cb_ref[:, cs] + cw_ref[half:half + 1, cs] * bot[8:16]
        for kk in range(SSD_CONV):
            d = kk - half
            if d == 0:
                continue
            w = cw_ref[kk:kk + 1, cs]
            acc = acc + w * pltpu.roll(xin, (-d) % seq, axis=0)
            acc_t = acc_t + w * pltpu.roll(top, (-d) % 24, axis=0)[8:16]
            acc_b = acc_b + w * pltpu.roll(bot, (-d) % 24, axis=0)[8:16]
        xbc_s[:, cs] = _silu(acc)
        xbc_s[0:8, cs] = _silu(acc_t)
        xbc_s[seq - 8:seq, cs] = _silu(acc_b)

    lane = lax.broadcasted_iota(jnp.int32, (1, LANES), 1)
    xdt = dt_ref[...] + dtb_ref[...]
    dtv = jnp.maximum(xdt, 0.0) + jnp.log1p(jnp.exp(-jnp.abs(xdt)))
    dtv_s[...] = jnp.where(lane < 2 * nh, dtv, 0.0)

    ii = lax.broadcasted_iota(jnp.int32, (q, q), 0)
    jj = lax.broadcasted_iota(jnp.int32, (q, q), 1)
    lower = jj <= ii
    upper = jj >= ii
    is_fwd = lane < nh
    lo64 = lane < hd
    tri = jnp.concatenate([lower, upper], axis=1).astype(BF16)
    wide = 2 * SSD_D
    ek = lax.broadcasted_iota(jnp.int32, (2 * LANES, wide), 0) & (LANES - 1)
    el = lax.broadcasted_iota(jnp.int32, (2 * LANES, wide), 1) >> (hd.bit_length() - 1)
    expand = (ek == el).astype(BF16)
    grow = lax.broadcasted_iota(jnp.int32, (LANES, SSD_D), 0) >= SSD_STATE
    glane = lax.broadcasted_iota(jnp.int32, (LANES, SSD_D), 1) >= hpg * hd
    own = grow == glane

    def split2(x):
        hi = x.astype(BF16)
        return hi, (x - hi.astype(F32)).astype(BF16)

    def lane_expand(x):
        hi, mid = split2(x)
        return _dot(jnp.concatenate([hi, mid], axis=1), expand)

    def stage_cumsum(c):
        r0 = pl.multiple_of(c * q, q)
        k = dict(c=c, rs=pl.ds(r0, q))
        k['dt'] = dtv_s[k['rs'], :]
        da_c = k['dt'] * a_ref[...]
        rhs = jnp.concatenate([jnp.where(is_fwd, da_c, 0.0), jnp.where(is_fwd, 0.0, da_c)], axis=0)
        p0, p1 = split2(rhs)
        p2 = (rhs - p0.astype(F32) - p1.astype(F32)).astype(BF16)
        k['ac'] = _dot(tri, p0) + _dot(tri, p1) + _dot(tri, p2)
        return k

    def stage_expand(k):
        ac, dt_c = k['ac'], k['dt']
        k['ac_t'] = ac.T
        k['dt_t'] = dt_c.T
        last = jnp.where(is_fwd, ac[q - 1:q, :], ac[0:1, :])
        eb_s[k['rs'], :] = lane_expand(jnp.exp(ac))
        k['wb'] = lane_expand(dt_c * jnp.exp(last - ac))

    def stage_states(k):
        rs = k['rs']
        k['xs'] = xbc_s[rs, 0:SSD_D]
        ball = xbc_s[rs, b_off:c_off].astype(BF16)
        call = xbc_s[rs, c_off:c_off + SSD_GROUPS * SSD_STATE]
        xw = (jnp.concatenate([k['xs'], k['xs']], axis=1) * k['wb']).astype(BF16)
        st_s[k['c']] = _dot_tn(ball, xw)
        k['cb'] = [_dot_nt(jnp.where(lo64 if g == 0 else jnp.logical_not(lo64), call, 0.0).astype(BF16),
                           ball) for g in range(SSD_GROUPS)]

    def stage_intra(k, pr):
        ac, ac_t, dt_t = k['ac'], k['ac_t'], k['dt_t']
        ws = []
        for h in (2 * pr, 2 * pr + 1):
            seg_f = ac[:, h:h + 1] - ac_t[h:h + 1, :]
            seg_b = ac[:, nh + h:nh + h + 1] - ac_t[nh + h:nh + h + 1, :]
            w = k['cb'][h // hpg] * (
                jnp.exp(jnp.where(lower, seg_f, -jnp.inf)) * dt_t[h:h + 1, :]
                + jnp.exp(jnp.where(upper, seg_b, -jnp.inf)) * dt_t[nh + h:nh + h + 1, :])
            ws.append(w.astype(BF16))
        xp = k['xs'][:, pr * LANES:(pr + 1) * LANES]
        xbd = jnp.concatenate([jnp.where(lo64, xp, 0.0), jnp.where(lo64, 0.0, xp)],
                              axis=0).astype(BF16)
        y_s[k['rs'], pr * LANES:(pr + 1) * LANES] = _dot(jnp.concatenate(ws, axis=1), xbd)

    ilv = min(SSD_ILV, nc)

    def chunks(cg, carry):
        ks = [stage_cumsum(cg * ilv + j) for j in range(ilv)]
        for k in ks:
            stage_expand(k)
        for k in ks:
            stage_states(k)
        for pr in range(nh // 2):
            for k in ks:
                stage_intra(k, pr)
        return carry

    lax.fori_loop(0, nc // ilv, chunks, 0)

    if has_h0:
        for direction in range(2):
            tr_s[...] = jnp.zeros(tr_s.shape, F32)
            for h in range(nh):
                g = h // hpg
                tr_s[h * hd:(h + 1) * hd, g * SSD_STATE:(g + 1) * SSD_STATE] = h0_ref[direction * nh + h]
            for k in range(SSD_D // LANES):
                h_s[direction, :, k * LANES:(k + 1) * LANES] = tr_s[k * LANES:(k + 1) * LANES, :].T
    else:
        h_s[...] = jnp.zeros(h_s.shape, F32)

    def carry_states(kstep, carry):
        for direction in range(2):
            c = kstep if direction == 0 else nc - 1 - kstep
            r0 = pl.multiple_of(c * q, q)
            rs = pl.ds(r0, q)
            ds_ = slice(direction * SSD_D, (direction + 1) * SSD_D)
            edge = r0 + q - 1 if direction == 0 else r0
            call = xbc_s[rs, c_off:c_off + SSD_GROUPS * SSD_STATE].astype(BF16)
            h_in = h_s[direction]
            y_s[rs, :] += _dot(call, jnp.where(own, h_in, 0.0).astype(BF16)) * eb_s[rs, ds_]
            h_s[direction] = eb_s[pl.ds(edge, 1), ds_] * h_in + st_s[c][:, ds_]
        return carry

    lax.fori_loop(0, nc, carry_states, 0)

    y = y_s[...] + d_ref[...] * xbc_s[:, 0:SSD_D]
    y = y * _silu(zx_ref[:, 0:SSD_D])
    y_ref[...] = _rms(y, nw_ref[...])
    if hout_ref is not None:
        for direction in range(2):
            for k in range(SSD_D // LANES):
                tr_s[k * LANES:(k + 1) * LANES, :] = h_s[direction, :, k * LANES:(k + 1) * LANES].T
            for h in range(nh):
                g = h // hpg
                hout_ref[direction * nh + h] = tr_s[h * hd:(h + 1) * hd,
                                                    g * SSD_STATE:(g + 1) * SSD_STATE]


def _ssd_call(zx, dt, lw, layer, state, seq):
    has_h0 = state is not None
    rows = zx.shape[0]
    nseq = rows // seq
    nc = seq // SSD_CHUNK
    nst = 2 * SSD_HEADS

    def seqspec(w):
        return pl.BlockSpec((seq, w), lambda b: (b, 0))

    def par(shape):
        return _const_spec(shape, layer, single_buffer=False)

    in_specs = [seqspec(1024), seqspec(LANES), par((8, SSD_CONV_DIM)), par((1, SSD_CONV_DIM)),
                par((1, LANES)), par((1, LANES)), par((1, SSD_D)), par((1, SSD_D))]
    args = [zx, dt, lw['conv_w'], lw['conv_b'], lw['dt_bias'], lw['a'], lw['d'], lw['norm']]
    y_shape = jax.ShapeDtypeStruct((rows, SSD_D), F32)
    if has_h0:
        in_specs.append(pl.BlockSpec((None, None, nst, SSD_HEAD_DIM, SSD_STATE),
                                     lambda b: (b, layer, 0, 0, 0)))
        args.append(state)
        out_shape = y_shape
        out_specs = seqspec(SSD_D)
    else:
        out_shape = [y_shape, jax.ShapeDtypeStruct((nseq, nst, SSD_HEAD_DIM, SSD_STATE), F32)]
        out_specs = [seqspec(SSD_D),
                     pl.BlockSpec((None, nst, SSD_HEAD_DIM, SSD_STATE), lambda b: (b, 0, 0, 0))]
    return pl.pallas_call(
        functools.partial(_ssd_body, seq=seq, has_h0=has_h0),
        out_shape=out_shape,
        grid=(nseq,),
        in_specs=in_specs,
        out_specs=out_specs,
        scratch_shapes=[
            pltpu.VMEM((seq, SSD_CONV_DIM), F32),
            pltpu.VMEM((seq, SSD_D), F32),
            pltpu.VMEM((seq, 2 * SSD_D), F32),
            pltpu.VMEM((seq, LANES), F32),
            pltpu.VMEM((nc, SSD_GROUPS * SSD_STATE, 2 * SSD_D), F32),
            pltpu.VMEM((2, SSD_GROUPS * SSD_STATE, SSD_D), F32),
            pltpu.VMEM((SSD_D, SSD_GROUPS * SSD_STATE), F32),
        ],
        compiler_params=_cparams("parallel"),
        name="ssd_mixer",
    )(*args)


def _s5_tables(a_re, a_im, log_dt, b_re, b_im, c_re, c_im):
    t = S5_T
    gh = S5_HGROUPS
    f32 = lambda x: x.astype(F32)
    lam_r, lam_i = f32(a_re), f32(a_im)
    step = jnp.exp(f32(log_dt))[..., None]
    xr, xi = lam_r * step, lam_i * step

    def powers(ks):
        ks = jnp.asarray(ks, F32)[None, None, :, None, None]
        mag = jnp.exp(xr[:, :, None] * ks)
        return mag * jnp.cos(xi[:, :, None] * ks), mag * jnp.sin(xi[:, :, None] * ks)

    pw_r, pw_i = powers(np.arange(t + 1))
    nr, ni = pw_r[:, :, 1] - 1.0, pw_i[:, :, 1]
    den = lam_r * lam_r + lam_i * lam_i
    fr = (nr * lam_r + ni * lam_i) / den
    fi = (ni * lam_r - nr * lam_i) / den
    br, bi = f32(b_re)[:, None], f32(b_im)[:, None]
    bb_r = fr[..., None] * br - fi[..., None] * bi
    bb_i = fr[..., None] * bi + fi[..., None] * br
    cr, ci = f32(c_re), f32(c_im)

    kern = []
    for direction in range(2):
        pr = jnp.swapaxes(pw_r[:, direction, :t], 1, 2)[:, :, :, None, :]
        pi = jnp.swapaxes(pw_i[:, direction, :t], 1, 2)[:, :, :, None, :]
        cpr = cr[:, :, None] * pr - ci[:, :, None] * pi
        cpi = cr[:, :, None] * pi + ci[:, :, None] * pr
        lhs = jnp.concatenate([cpr, -cpi], axis=-1).reshape(DEPTH, S5_GROUPS, t * S5_GROUP_CH, 2 * S5_STATE)
        rhs = jnp.concatenate([bb_r[:, direction], bb_i[:, direction]], axis=2)
        k = jnp.einsum('dgxn,dgnk->dgxk', lhs, rhs, precision=HIGHEST)
        kern.append(jnp.swapaxes(k.reshape(DEPTH, S5_GROUPS, t, S5_GROUP_CH, S5_GROUP_CH), 1, 2))
    kf, kb = kern
    lag = [kb[:, -d] if d < 0 else (kf[:, 0] + kb[:, 0] if d == 0 else kf[:, d])
           for d in range(-(t - 1), t)]
    kt = jnp.stack(lag, axis=1).reshape(DEPTH, S5_LAGS, 2, gh, S5_GROUP_CH, S5_GROUP_CH)
    kt = jnp.transpose(kt, (0, 2, 1, 5, 3, 4)).reshape(DEPTH, 2, S5_LAGS, S5_GROUP_CH, LANES)

    def by_half(x):
        return jnp.transpose(x.reshape(DEPTH, -1, 2, gh, S5_STATE), (0, 2, 1, 3, 4))

    def dir_powers(direction, ks):
        pr, pi = powers(np.asarray(ks))
        return pr[:, direction], pi[:, direction]

    bt_r = jnp.transpose(bb_r.reshape(DEPTH, 2, 2, gh, S5_STATE, S5_GROUP_CH), (0, 1, 2, 5, 3, 4))
    bt_i = jnp.transpose(bb_i.reshape(DEPTH, 2, 2, gh, S5_STATE, S5_GROUP_CH), (0, 1, 2, 5, 3, 4))
    sw = []
    for direction, ks in ((0, [t - 1 - s for s in range(t)]), (1, list(range(t)))):
        pr, pi = dir_powers(direction, ks)
        pr = by_half(pr)[:, :, :, None]
        pi = by_half(pi)[:, :, :, None]
        wr, wi = bt_r[:, direction][:, :, None], bt_i[:, direction][:, :, None]
        sw += [pr * wr - pi * wi, pr * wi + pi * wr]
    sw = jnp.stack(sw, axis=3).reshape(DEPTH, 2, t, 4, S5_GROUP_CH, gh * S5_STATE)
    ct_r = jnp.transpose(cr.reshape(DEPTH, 2, gh, S5_GROUP_CH, S5_STATE), (0, 1, 4, 2, 3))
    ct_i = jnp.transpose(ci.reshape(DEPTH, 2, gh, S5_GROUP_CH, S5_STATE), (0, 1, 4, 2, 3))
    aw = []
    for direction, ks in ((0, list(range(1, t + 1))), (1, [t - k for k in range(t)])):
        pr, pi = dir_powers(direction, ks)
        pr = jnp.swapaxes(by_half(pr), 3, 4)[..., None]
        pi = jnp.swapaxes(by_half(pi), 3, 4)[..., None]
        wr, wi = ct_r[:, :, None], ct_i[:, :, None]
        aw += [wr * pr - wi * pi, -(wr * pi + wi * pr)]
    aw = jnp.stack(aw, axis=2).reshape(DEPTH, 2, 4, t, S5_STATE, LANES)

    apow = jnp.stack([pw_r[:, 0, t], pw_i[:, 0, t], pw_r[:, 1, t], pw_i[:, 1, t]], axis=1)
    seg_r, seg_i = powers(t * np.arange(S5_NC + 1))
    back_r, back_i = dir_powers(1, t * (S5_NC - 1 - np.arange(S5_NC)))
    pseg = jnp.stack([seg_r[:, 0, :S5_NC], seg_i[:, 0, :S5_NC], back_r, back_i], axis=1)
    nkb = S5_CW // LANES
    pseg = jnp.transpose(pseg.reshape(DEPTH, 4, S5_NC, nkb, LANES), (0, 1, 3, 2, 4))
    pseg = pseg.reshape(DEPTH, 4 * nkb, S5_NC, LANES)
    aseg = jnp.stack([seg_r[:, 0, S5_NC], seg_i[:, 0, S5_NC],
                      seg_r[:, 1, S5_NC], seg_i[:, 1, S5_NC]], axis=1)
    return (kt, sw, aw, apow.reshape(DEPTH, 1, S5_SW), pseg, aseg.reshape(DEPTH, 1, S5_SW))


def _s5_expand_operators(kt_ref, sw_ref, aw_ref, toep_s, sop_s, aop_s):
    gh = S5_HGROUPS
    sh_ch = S5_GROUP_CH.bit_length() - 1
    sh_st = S5_STATE.bit_length() - 1

    def group_mask(shape, row_shift, lane_shift):
        r = lax.broadcasted_iota(jnp.int32, shape, 0) >> row_shift
        c = lax.broadcasted_iota(jnp.int32, shape, 1) >> lane_shift
        return r == c

    def blockdiag(x, mask):
        return jnp.where(mask, jnp.concatenate([x] * gh, axis=0), 0.0).astype(BF16)

    m_kk = group_mask((LANES, LANES), sh_ch, sh_ch)
    m_ks = group_mask((LANES, gh * S5_STATE), sh_ch, sh_st)
    m_sk = group_mask((gh * S5_STATE, LANES), sh_st, sh_ch)
    cw = gh * S5_STATE
    for hf in range(2):
        lags = [blockdiag(kt_ref[hf, d], m_kk) for d in range(S5_LAGS)]
        for s in range(S5_T):
            for t in range(S5_T):
                toep_s[hf, s * LANES:(s + 1) * LANES, t * LANES:(t + 1) * LANES] = lags[t - s + S5_T - 1]
            for comp in range(4):
                sop_s[hf, s * LANES:(s + 1) * LANES, comp * cw:(comp + 1) * cw] = blockdiag(
                    sw_ref[hf, s, comp], m_ks)
        for comp in range(4):
            for t in range(S5_T):
                aop_s[hf, comp * cw:(comp + 1) * cw, t * LANES:(t + 1) * LANES] = blockdiag(
                    aw_ref[hf, comp, t], m_sk)


def _s5_body(up_ref, us_ref, kt_ref, sw_ref, aw_ref, at_ref, pseg_ref, aseg_ref, h0_ref,
             y_ref, hout_ref, toep_s, sop_s, aop_s, st_s, hin_s, yacc_s, *, nblk_p, nseg_s):
    i = pl.program_id(0)

    @pl.when(i == 0)
    def _():
        _s5_expand_operators(kt_ref, sw_ref, aw_ref, toep_s, sop_s, aop_s)

    cw = S5_CW
    nkb = cw // LANES
    is_p = i < nblk_p

    def time_rows(s):
        return pl.ds(s, S5_ROWS, stride=S5_T)

    for hf in range(2):
        xh = jnp.concatenate(
            [jnp.where(is_p, up_ref[hf, time_rows(s), :], us_ref[hf, time_rows(s), :])
             for s in range(S5_T)], axis=1).astype(BF16)
        yacc_s[hf] = _dot(xh, toep_s[hf])
        sh = _dot(xh, sop_s[hf])
        for comp in range(4):
            for k in range(nkb // 2):
                col = (comp * (nkb // 2) + k) * LANES
                st_s[comp * nkb + hf * (nkb // 2) + k] = sh[:, col:col + LANES]

    def load(ref, comp, rows):
        return jnp.concatenate([ref[comp * nkb + k, rows, :] for k in range(nkb)], axis=1)

    def store(ref, comp, rows, val):
        for k in range(nkb):
            ref[comp * nkb + k, rows, :] = val[:, k * LANES:(k + 1) * LANES]

    def cmul(ar, ai, hr, hi):
        return ar * hr - ai * hi, ar * hi + ai * hr

    at = at_ref[...]
    ar_f, ai_f, ar_b, ai_b = [at[:, k * cw:(k + 1) * cw] for k in range(4)]

    def step(c, carry):
        hfr, hfi, hbr, hbi = carry
        rf = pl.ds(c, S5_HB, stride=S5_NC)
        rb = pl.ds(S5_NC - 1 - c, S5_HB, stride=S5_NC)
        for comp, val, rows in ((0, hfr, rf), (1, hfi, rf), (2, hbr, rb), (3, hbi, rb)):
            store(hin_s, comp, rows, val)
        fr, fi = cmul(ar_f, ai_f, hfr, hfi)
        br, bi = cmul(ar_b, ai_b, hbr, hbi)
        return (fr + load(st_s, 0, rf), fi + load(st_s, 1, rf),
                br + load(st_s, 2, rb), bi + load(st_s, 3, rb))

    zero = jnp.zeros((S5_HB, cw), F32)
    fin = lax.fori_loop(0, S5_NC, step, (zero, zero, zero, zero))
    hout_ref[...] = jnp.concatenate(fin, axis=-1)

    @pl.when(i >= nblk_p)
    def _():
        h0 = h0_ref[...]
        aseg = aseg_ref[...]
        sr_f, si_f, sr_b, si_b = [aseg[:, k * cw:(k + 1) * cw] for k in range(4)]
        ent = [[None] * S5_HB for _ in range(4)]
        for s in range(S5_HB // nseg_s):
            hr, hi = h0[s:s + 1, 0:cw], h0[s:s + 1, cw:2 * cw]
            for j in range(nseg_s):
                v = s * nseg_s + j
                ent[0][v], ent[1][v] = hr, hi
                hr, hi = cmul(sr_f, si_f, hr, hi)
                hr, hi = hr + fin[0][v:v + 1], hi + fin[1][v:v + 1]
            hr, hi = h0[s:s + 1, 2 * cw:3 * cw], h0[s:s + 1, 3 * cw:4 * cw]
            for j in range(nseg_s - 1, -1, -1):
                v = s * nseg_s + j
                ent[2][v], ent[3][v] = hr, hi
                hr, hi = cmul(sr_b, si_b, hr, hi)
                hr, hi = hr + fin[2][v:v + 1], hi + fin[3][v:v + 1]
        for direction in range(2):
            for k in range(nkb):
                kr = (2 * direction) * nkb + k
                ki = (2 * direction + 1) * nkb + k
                pr, pi = pseg_ref[kr], pseg_ref[ki]
                for v in range(S5_HB):
                    rows = slice(v * S5_NC, (v + 1) * S5_NC)
                    er = ent[2 * direction][v][:, k * LANES:(k + 1) * LANES]
                    ei = ent[2 * direction + 1][v][:, k * LANES:(k + 1) * LANES]
                    dr, di = cmul(pr, pi, er, ei)
                    hin_s[kr, rows, :] += dr
                    hin_s[ki, rows, :] += di

    for hf in range(2):
        hh = jnp.concatenate(
            [hin_s[comp * nkb + hf * (nkb // 2) + k] for comp in range(4) for k in range(nkb // 2)],
            axis=1).astype(BF16)
        y = yacc_s[hf] + _dot(hh, aop_s[hf])
        for t in range(S5_T):
            y_ref[hf, time_rows(t), :] = y[:, t * LANES:(t + 1) * LANES]


def _s5_call(u_p, u_s, tables, layer, h0, seq_p, seq_s):
    kt, sw, aw, apow, pseg, aseg = tables
    seg_tok = S5_T * S5_NC
    assert seq_p == seg_tok and seq_s % seg_tok == 0 and S5_HB % (seq_s // seg_tok) == 0
    blk_tok = S5_ROWS * S5_T
    nblk_p = u_p.shape[1] // blk_tok
    nblk = nblk_p + u_s.shape[1] // blk_tok
    uh = S5_D // LANES

    def par(a):
        return _const_spec(a.shape[1:], layer)

    nlb = S5_SW // LANES
    return pl.pallas_call(
        functools.partial(_s5_body, nblk_p=nblk_p, nseg_s=seq_s // seg_tok),
        out_shape=[jax.ShapeDtypeStruct((uh, nblk * blk_tok, LANES), F32),
                   jax.ShapeDtypeStruct((nblk, S5_HB, S5_SW), F32)],
        grid=(nblk,),
        in_specs=[pl.BlockSpec((uh, blk_tok, LANES), lambda i: (0, jnp.minimum(i, nblk_p - 1), 0)),
                  pl.BlockSpec((uh, blk_tok, LANES), lambda i: (0, jnp.maximum(i - nblk_p, 0), 0)),
                  par(kt), par(sw), par(aw), par(apow), par(pseg), par(aseg),
                  pl.BlockSpec((None, None, S5_HB, S5_SW), lambda i: (layer, i, 0, 0))],
        out_specs=[pl.BlockSpec((uh, blk_tok, LANES), lambda i: (0, i, 0)),
                   pl.BlockSpec((None, S5_HB, S5_SW), lambda i: (i, 0, 0))],
        scratch_shapes=[pltpu.VMEM((2, S5_KW, S5_KW), BF16), pltpu.VMEM((2, S5_KW, S5_SH), BF16),
                        pltpu.VMEM((2, S5_SH, S5_KW), BF16),
                        pltpu.VMEM((nlb, S5_ROWS, LANES), F32), pltpu.VMEM((nlb, S5_ROWS, LANES), F32),
                        pltpu.VMEM((2, S5_ROWS, S5_KW), F32)],
        compiler_params=_cparams("arbitrary"),
        name="s5_mixer",
    )(u_p, u_s, kt, sw, aw, apow, pseg, aseg, h0)


def _gelu_tanh(x):
    return 0.5 * x * (1.0 + jnp.tanh(math.sqrt(2.0 / math.pi) * (x + 0.044715 * (x * x * x))))


def _out_body(*refs, final):
    (x_ref, yssd_ref, ynat_ref, u_ref, y5_ref, g1_ref, m2_ref, d5_ref,
     wglu_ref, bglu_ref, wout_ref, nm_ref, w1_ref, w2_ref) = refs[:14]
    if final:
        nf_ref, o_ref = refs[14:]
    else:
        (o_ref,) = refs[14:]
    uh = S5_D // LANES
    for sb in range(ROW_TILE // TOK_TILE):
        rows = slice(sb * TOK_TILE, (sb + 1) * TOK_TILE)
        y5_in = jnp.concatenate([y5_ref[hf, rows, :] for hf in range(uh)], axis=1)
        u = jnp.concatenate([u_ref[hf, rows, :] for hf in range(uh)], axis=1)
        g = _gelu_tanh(y5_in + d5_ref[...] * u)
        y5 = g * jax.nn.sigmoid(_dot(g.astype(BF16), wglu_ref[...]) + bglu_ref[...])
        mix = (_dot(yssd_ref[rows, :].astype(BF16), wout_ref[0:SSD_D, :])
               + _dot(y5.astype(BF16), wout_ref[SSD_D:SSD_D + S5_D, :])
               + _dot(ynat_ref[rows, :].astype(BF16), wout_ref[SSD_D + S5_D:, :]))
        x = x_ref[rows, :] + g1_ref[...] * mix
        h2 = _rms(x, nm_ref[...]) * (1.0 + m2_ref[:, D_MODEL:2 * D_MODEL]) + m2_ref[:, 0:D_MODEL]
        f = jnp.maximum(_dot(h2.astype(BF16), w1_ref[...]), 0.0)
        f = (f * f).astype(BF16)
        x = x + m2_ref[:, 2 * D_MODEL:3 * D_MODEL] * _dot(f, w2_ref[...])
        if final:
            x = _rms(x, nf_ref[...])
        o_ref[rows, :] = x


def _out_call(x, y_ssd, y_nat, u, y5, y5_tile_off, cond_idx, mods4, lw, norm_f, layer, final):
    rows = x.shape[0]
    uh = S5_D // LANES

    def tile(w):
        return pl.BlockSpec((ROW_TILE, w), lambda i: (i, 0))

    def par(shape):
        return _const_spec(shape, layer)

    in_specs = [
        tile(D_MODEL), tile(SSD_D), tile(NAT_D),
        pl.BlockSpec((uh, ROW_TILE, LANES), lambda i: (0, i, 0)),
        pl.BlockSpec((uh, ROW_TILE, LANES), lambda i: (0, i + y5_tile_off, 0)),
        pl.BlockSpec((None, None, 1, D_MODEL), lambda i: (layer, cond_idx(i), 0, 2)),
        pl.BlockSpec((None, None, 1, 3 * D_MODEL), lambda i: (layer, cond_idx(i), 0, 1)),
        par((1, S5_D)), par((S5_D, S5_D)), par((1, S5_D)),
        par((D_MODEL, D_MODEL)), par((1, D_MODEL)),
        par((D_MODEL, D_FF)), par((D_FF, D_MODEL))]
    args = [x, y_ssd, y_nat, u, y5, mods4, mods4, lw['s5_d'], lw['w_glu'], lw['b_glu'],
            lw['w_out'], lw['norm_mlp'], lw['w_ff1'], lw['w_ff2']]
    if final:
        in_specs.append(pl.BlockSpec((1, D_MODEL), lambda i: (0, 0)))
        args.append(norm_f.reshape(1, D_MODEL))
    return pl.pallas_call(
        functools.partial(_out_body, final=final),
        out_shape=jax.ShapeDtypeStruct((rows, D_MODEL), F32),
        grid=(rows // ROW_TILE,),
        in_specs=in_specs,
        out_specs=tile(D_MODEL),
        compiler_params=_cparams("parallel"),
        name="out_mlp",
    )(*args)


def _lane_pad(x):
    return jnp.pad(x, [(0, 0)] * (x.ndim - 1) + [(0, LANES - x.shape[-1])])


def kernel(x_prompt, x_sample, cache_nat_k, cache_nat_v, state_ssd, state_s5_re, state_s5_im,
           c, c_ctx, w_mod, b_mod, norm_mix, norm_mlp, w_in, ssd_conv_w, ssd_conv_b,
           ssd_dt_bias, ssd_a_log, ssd_d, ssd_norm, s5_a_re, s5_a_im, s5_log_dt,
           s5_b_re, s5_b_im, s5_c_re, s5_c_im, s5_d, s5_w_glu, s5_b_glu, nat_rpb,
           w_out, w_ff1, w_ff2, norm_f):
    bp, seq_p, _ = x_prompt.shape
    bs, seq_s, _ = x_sample.shape
    assert seq_p == TOK_TILE and seq_s == 4 * TOK_TILE
    rows_p = bp * seq_p
    rows_s = bs * seq_s

    ncp = -(-(1 + bs) // 8) * 8
    cond = jnp.concatenate([c_ctx[None, :], c, jnp.zeros((ncp - 1 - bs, D_MODEL), F32)], axis=0)
    mods = _mods_call(cond, w_mod, b_mod)
    mods4 = mods.reshape(DEPTH, ncp, 1, N_MOD * D_MODEL)

    assert w_in.shape[-1] == IN_COLS
    w_in_p = w_in.astype(BF16)
    ssd_w = {
        'conv_w': jnp.pad(ssd_conv_w.astype(F32), [(0, 0), (0, 8 - SSD_CONV), (0, 0)]),
        'conv_b': ssd_conv_b.astype(F32).reshape(DEPTH, 1, SSD_CONV_DIM),
        'dt_bias': _lane_pad(ssd_dt_bias.astype(F32).reshape(DEPTH, 1, 2 * SSD_HEADS)),
        'a': _lane_pad(-jnp.exp(ssd_a_log.astype(F32)).reshape(DEPTH, 1, 2 * SSD_HEADS)),
        'd': jnp.repeat(ssd_d.astype(F32), SSD_HEAD_DIM, axis=-1).reshape(DEPTH, 1, SSD_D),
        'norm': ssd_norm.astype(F32).reshape(DEPTH, 1, SSD_D),
    }
    out_w = {
        's5_d': s5_d.astype(F32).reshape(DEPTH, 1, S5_D),
        'w_glu': s5_w_glu.astype(BF16),
        'b_glu': s5_b_glu.astype(F32).reshape(DEPTH, 1, S5_D),
        'w_out': w_out.astype(BF16),
        'norm_mlp': norm_mlp.astype(F32).reshape(DEPTH, 1, D_MODEL),
        'w_ff1': w_ff1.astype(BF16),
        'w_ff2': w_ff2.astype(BF16),
    }
    s5_tabs = _s5_tables(s5_a_re, s5_a_im, s5_log_dt, s5_b_re, s5_b_im, s5_c_re, s5_c_im)
    nat_bias = _nat_pair_tiles(nat_rpb)

    cache_k = cache_nat_k.reshape(bs, DEPTH, -1, NAT_D)
    cache_v = cache_nat_v.reshape(bs, DEPTH, -1, NAT_D)
    st_ssd = state_ssd.reshape(bs, DEPTH, 2 * SSD_HEADS, SSD_HEAD_DIM, SSD_STATE)
    s5_tok = S5_ROWS * S5_T
    assert rows_p % s5_tok == 0 and rows_s % s5_tok == 0 and s5_tok % seq_s == 0
    nseq_s = s5_tok // seq_s
    st5 = jnp.stack([state_s5_re[:, :, 0], state_s5_im[:, :, 0],
                     state_s5_re[:, :, 1], state_s5_im[:, :, 1]], axis=2)
    st5 = jnp.transpose(st5.astype(F32).reshape(bs // nseq_s, nseq_s, DEPTH, S5_SW), (2, 0, 1, 3))
    st5 = jnp.pad(st5, [(0, 0), (rows_p // s5_tok, 0), (0, S5_HB - nseq_s), (0, 0)])

    assert rows_p % ROW_TILE == 0 and seq_s % ROW_TILE == 0
    tiles_per_seq = seq_s // ROW_TILE
    cond_p = lambda i: 0
    cond_s = lambda i: 1 + i // tiles_per_seq
    x_p = x_prompt.reshape(rows_p, D_MODEL)
    x_s = x_sample.reshape(rows_s, D_MODEL)
    new_k, new_v, new_ssd, new_s5 = [], [], [], []
    for l in range(DEPTH):
        final = l == DEPTH - 1
        zx_p, dt_p, u_p, q_p, k_p, v_p = _inproj_call(x_p, cond_p, mods4, norm_mix, w_in_p, l)
        zx_s, dt_s, u_s, q_s, k_s, v_s = _inproj_call(x_s, cond_s, mods4, norm_mix, w_in_p, l)
        new_k.append(k_p.reshape(bp, seq_p, NAT_D))
        new_v.append(v_p.reshape(bp, seq_p, NAT_D))

        y_nat_p = _ctx_attn_call(q_p, k_p, v_p)
        y_nat_s = _nat_attn_call(q_s, k_s, v_s, cache_k, cache_v, nat_bias, l, bs, seq_s)

        y_ssd_p, ssd_l = _ssd_call(zx_p, dt_p, ssd_w, l, None, seq_p)
        y_ssd_s = _ssd_call(zx_s, dt_s, ssd_w, l, st_ssd, seq_s)
        new_ssd.append(ssd_l)

        y5, s5_l = _s5_call(u_p, u_s, s5_tabs, l, st5, seq_p, seq_s)
        new_s5.append(s5_l[:rows_p // s5_tok])

        x_p = _out_call(x_p, y_ssd_p, y_nat_p, u_p, y5, 0, cond_p, mods4, out_w, norm_f, l, final)
        x_s = _out_call(x_s, y_ssd_s, y_nat_s, u_s, y5, rows_p // ROW_TILE, cond_s,
                        mods4, out_w, norm_f, l, final)

    y_prompt = x_p.reshape(bp, seq_p, D_MODEL)
    y_sample = x_s.reshape(bs, seq_s, D_MODEL)
    out_k = jnp.stack(new_k, axis=1).reshape(bp, DEPTH, seq_p, NAT_HEADS, NAT_HEAD_DIM)
    out_v = jnp.stack(new_v, axis=1).reshape(bp, DEPTH, seq_p, NAT_HEADS, NAT_HEAD_DIM)
    out_ssd = jnp.stack(new_ssd, axis=1).reshape(bp, DEPTH, 2, SSD_HEADS, SSD_HEAD_DIM, SSD_STATE)
    s5 = jnp.stack(new_s5, axis=0)[:, :, :s5_tok // seq_p]
    s5 = s5.reshape(DEPTH, bp, 4, S5_GROUPS, S5_STATE)
    s5 = jnp.transpose(s5, (1, 0, 2, 3, 4))
    out_re = s5[:, :, 0::2]
    out_im = s5[:, :, 1::2]
    return y_prompt, y_sample, out_k, out_v, out_ssd, out_re, out_im
```

```python
import functools
import math

import numpy as np
import jax
import jax.numpy as jnp
from jax import lax
from jax.experimental import pallas as pl
from jax.experimental.pallas import tpu as pltpu

F32 = jnp.float32
BF16 = jnp.bfloat16
HIGHEST = lax.Precision.HIGHEST

D_MODEL = 1024
DEPTH = 4
GRID_W = 64
SSD_HEADS = 6
SSD_HEAD_DIM = 64
SSD_D = SSD_HEADS * SSD_HEAD_DIM
SSD_GROUPS = 2
SSD_STATE = 64
SSD_CONV = 5
SSD_CHUNK = 128
SSD_CONV_DIM = SSD_D + 2 * SSD_GROUPS * SSD_STATE
SSD_ILV = 4
S5_GROUPS = 16
S5_GROUP_CH = 16
S5_D = S5_GROUPS * S5_GROUP_CH
S5_STATE = 64
NAT_HEADS = 6
NAT_HEAD_DIM = 64
NAT_D = NAT_HEADS * NAT_HEAD_DIM
NAT_KH = 8
NAT_KW = 16
D_FF = 4 * D_MODEL
N_MOD = 6
EPS = 1e-6

LANES = 128
TOK_TILE = 256
ROW_TILE = 512
IN_COLS = SSD_D + SSD_CONV_DIM + 2 * SSD_HEADS + S5_D + 3 * NAT_D
S5_T = 8
S5_HGROUPS = LANES // S5_GROUP_CH
S5_KW = S5_T * LANES
S5_CW = S5_GROUPS * S5_STATE
S5_SW = 4 * S5_CW
S5_SH = S5_SW // 2
S5_NC = 32
S5_HB = 8
S5_ROWS = S5_HB * S5_NC
S5_LAGS = 2 * S5_T - 1
VMEM_LIMIT = 56 * 1024 * 1024


def _cparams(*sem):
    return pltpu.CompilerParams(dimension_semantics=sem, vmem_limit_bytes=VMEM_LIMIT)


def _dot(a, b, precision=None):
    return jnp.dot(a, b, preferred_element_type=F32, precision=precision)


def _dot_nt(a, b, precision=None):
    return lax.dot_general(a, b, (((1,), (1,)), ((), ())), preferred_element_type=F32,
                           precision=precision)


def _dot_tn(a, b, precision=None):
    return lax.dot_general(a, b, (((0,), (0,)), ((), ())), preferred_element_type=F32,
                           precision=precision)


def _silu(x):
    hx = 0.5 * x
    return hx + hx * jnp.tanh(hx)


def _rms(x, g):
    return x * lax.rsqrt(jnp.mean(x * x, axis=-1, keepdims=True) + EPS) * g


def _const_spec(shape, layer, single_buffer=True):
    return pl.BlockSpec((None,) + tuple(shape), lambda *_: (layer,) + (0,) * len(shape),
                        pipeline_mode=pl.Buffered(1) if single_buffer else None)


def _mods_body(cond_ref, w_ref, b_ref, o_ref):
    s = _silu(cond_ref[...])
    o_ref[...] = _dot(s.astype(BF16), w_ref[...].astype(BF16)) + b_ref[...]


def _mods_call(cond, w_mod, b_mod):
    ncp = cond.shape[0]
    blk = D_MODEL
    return pl.pallas_call(
        _mods_body,
        out_shape=jax.ShapeDtypeStruct((DEPTH, ncp, N_MOD * D_MODEL), F32),
        grid=(DEPTH, N_MOD),
        in_specs=[
            pl.BlockSpec((ncp, D_MODEL), lambda l, j: (0, 0)),
            pl.BlockSpec((None, D_MODEL, blk), lambda l, j: (l, 0, j)),
            pl.BlockSpec((None, 1, blk), lambda l, j: (l, 0, j)),
        ],
        out_specs=pl.BlockSpec((None, ncp, blk), lambda l, j: (l, 0, j)),
        compiler_params=_cparams("arbitrary", "arbitrary"),
        name="adaln_mods",
    )(cond, w_mod, b_mod.reshape(DEPTH, 1, N_MOD * D_MODEL))


def _inproj_body(x_ref, mod_ref, g_ref, w_ref, zx_ref, dt_ref, u_ref, q_ref, k_ref, v_ref):
    x = x_ref[...]
    h = _rms(x, g_ref[...]) * (1.0 + mod_ref[:, D_MODEL:2 * D_MODEL]) + mod_ref[:, 0:D_MODEL]
    p = _dot(h.astype(BF16), w_ref[...])
    o_dt = SSD_D + SSD_CONV_DIM
    zx_ref[...] = p[:, 0:o_dt]
    dt_ref[...] = p[:, o_dt:o_dt + LANES]
    rest = p[:, o_dt + 2 * SSD_HEADS:]
    for hf in range(S5_D // LANES):
        u_ref[hf] = rest[:, hf * LANES:(hf + 1) * LANES]
    q_ref[...] = rest[:, S5_D:S5_D + NAT_D]
    k_ref[...] = rest[:, S5_D + NAT_D:S5_D + 2 * NAT_D]
    v_ref[...] = rest[:, S5_D + 2 * NAT_D:S5_D + 3 * NAT_D]


def _inproj_call(x, cond_idx, mods4, norm_mix, w_in_p, layer):
    rows = x.shape[0]

    def tile(w):
        return pl.BlockSpec((ROW_TILE, w), lambda i: (i, 0))

    def out(w):
        return jax.ShapeDtypeStruct((rows, w), F32)

    uh = S5_D // LANES
    return pl.pallas_call(
        _inproj_body,
        out_shape=[out(1024), out(LANES), jax.ShapeDtypeStruct((uh, rows, LANES), F32),
                   out(NAT_D), out(NAT_D), out(NAT_D)],
        grid=(rows // ROW_TILE,),
        in_specs=[
            tile(D_MODEL),
            pl.BlockSpec((None, None, 1, 2 * D_MODEL), lambda i: (layer, cond_idx(i), 0, 0)),
            _const_spec((1, D_MODEL), layer, single_buffer=False),
            _const_spec((D_MODEL, IN_COLS), layer),
        ],
        out_specs=[tile(1024), tile(LANES), pl.BlockSpec((uh, ROW_TILE, LANES), lambda i: (0, i, 0)),
                   tile(NAT_D), tile(NAT_D), tile(NAT_D)],
        compiler_params=_cparams("parallel"),
        name="in_proj",
    )(x, mods4, norm_mix.reshape(DEPTH, 1, D_MODEL), w_in_p)


NAT_SCALE = NAT_HEAD_DIM ** -0.5


def _values_with_ones(v):
    return jnp.concatenate([v, jnp.ones(v.shape, BF16)], axis=1)


def _pair_attention(q, keys, v_ext, bias, lo):
    outs = []
    for hh in range(2):
        qm = jnp.where(lo if hh == 0 else jnp.logical_not(lo), q, 0.0).astype(BF16)
        s = _dot_nt(qm, keys)
        if bias is not None:
            b = bias(hh)
            nb = b.shape[1]
            s = jnp.concatenate([s[:, 0:nb] + b, s[:, nb:]], axis=1)
        p = jnp.exp(s - jnp.max(s, axis=-1, keepdims=True)).astype(BF16)
        o = _dot(p, v_ext)
        outs.append(o[:, 0:LANES] / o[:, LANES:2 * LANES])
    return jnp.where(lo, outs[0], outs[1])


def _ctx_attn_body(q_ref, k_ref, v_ref, o_ref):
    lo = lax.broadcasted_iota(jnp.int32, (1, LANES), 1) < NAT_HEAD_DIM
    for p in range(NAT_D // LANES):
        cs = slice(p * LANES, (p + 1) * LANES)
        o_ref[:, cs] = _pair_attention(q_ref[:, cs] * NAT_SCALE, k_ref[:, cs].astype(BF16),
                                       _values_with_ones(v_ref[:, cs].astype(BF16)), None, lo)


def _ctx_attn_call(q, k, v):
    rows = q.shape[0]
    spec = pl.BlockSpec((TOK_TILE, NAT_D), lambda b: (b, 0))
    return pl.pallas_call(
        _ctx_attn_body,
        out_shape=jax.ShapeDtypeStruct((rows, NAT_D), F32),
        grid=(rows // TOK_TILE,),
        in_specs=[spec, spec, spec],
        out_specs=spec,
        compiler_params=_cparams("parallel"),
        name="ctx_attn",
    )(q, k, v)


NAT_QROWS = 4
NAT_HEAD_PAIRS = NAT_HEADS // 2
NAT_NPAIR = 2 * NAT_KH


def _nat_window_start(r, rows):
    kh = min(NAT_KH, rows)
    return int(np.clip(r - kh // 2, 0, rows - kh))


def _nat_blocks(rows):
    kh = min(NAT_KH, rows)
    out = []
    for j in range(rows // NAT_QROWS):
        rs = [_nat_window_start(r, rows) for r in range(j * NAT_QROWS, (j + 1) * NAT_QROWS)]
        first = min(rs)
        n = -(-(max(rs) + kh - first) // NAT_QROWS) * NAT_QROWS
        first = min(first, rows - n)
        out.append((first, n))
    return out


def _nat_pair_tiles(rpb):
    cols = np.arange(GRID_W)
    c_start = np.clip(cols - NAT_KW // 2, 0, GRID_W - NAT_KW)
    col_mask = (cols[None, :] >= c_start[:, None]) & (cols[None, :] < c_start[:, None] + NAT_KW)
    idx = cols[None, :] - cols[:, None] + NAT_KW - 1
    sel = (idx[None] == np.arange(2 * NAT_KW - 1)[:, None, None]).astype(np.float32)
    tiles = jnp.einsum('dhab,bqk->dhaqk', rpb.astype(F32), sel, precision=HIGHEST)
    tiles = jnp.where(col_mask, tiles, -jnp.inf)
    neg = jnp.full(tiles.shape[:2] + (1, GRID_W, GRID_W), -jnp.inf, F32)
    ext = jnp.concatenate([neg, tiles, neg], axis=2)
    pairs = jnp.concatenate([ext[:, :, 0:NAT_NPAIR], ext[:, :, 1:NAT_NPAIR + 1]], axis=-1)
    return pairs.reshape(DEPTH, NAT_HEAD_PAIRS, 2, NAT_NPAIR, GRID_W, 2 * GRID_W)


def _nat_block_bias(bias_ref, hh, j, first, n, rows, lo):
    kh = min(NAT_KH, rows)
    neg = jnp.full((GRID_W, 2 * GRID_W), -jnp.inf, F32)
    row_blocks = []
    for ql in range(NAT_QROWS):
        qr = j * NAT_QROWS + ql
        rs = _nat_window_start(qr, rows)
        pieces = []
        for m in range(n // 2):
            k0 = first + 2 * m
            ok0 = rs <= k0 < rs + kh
            ok1 = rs <= k0 + 1 < rs + kh
            if not (ok0 or ok1):
                pieces.append(neg)
                continue
            t = bias_ref[hh, k0 - qr + NAT_KH]
            if ok0 and ok1:
                pieces.append(t)
            elif ok0:
                pieces.append(jnp.where(lo, t, -jnp.inf))
            else:
                pieces.append(jnp.where(lo, -jnp.inf, t))
        row_blocks.append(jnp.concatenate(pieces, axis=1))
    return jnp.concatenate(row_blocks, axis=0)


def _nat_attn_body(q_ref, k_ref, v_ref, kc_ref, vc_ref, bias_ref, o_ref, *, rows):
    lo = lax.broadcasted_iota(jnp.int32, (1, LANES), 1) < NAT_HEAD_DIM
    kc = kc_ref[...].astype(BF16)
    vc = vc_ref[...].astype(BF16)
    nq = NAT_QROWS * GRID_W
    for j, (first, n) in enumerate(_nat_blocks(rows)):
        q = q_ref[j * nq:(j + 1) * nq, :] * NAT_SCALE
        ks = slice(first * GRID_W, (first + n) * GRID_W)
        keys = jnp.concatenate([k_ref[ks, :].astype(BF16), kc], axis=0)
        v_ext = _values_with_ones(jnp.concatenate([v_ref[ks, :].astype(BF16), vc], axis=0))

        def bias(hh, j=j, first=first, n=n):
            return _nat_block_bias(bias_ref, hh, j, first, n, rows, lo)

        o_ref[j * nq:(j + 1) * nq, :] = _pair_attention(q, keys, v_ext, bias, lo)


def _nat_attn_call(q, k, v, cache_k, cache_v, bias, layer, bs, seq):
    rows = seq // GRID_W
    assert rows % NAT_QROWS == 0 and 2 * GRID_W == LANES
    spec = pl.BlockSpec((seq, LANES), lambda p, b: (b, p))
    cspec = pl.BlockSpec((None, None, cache_k.shape[2], LANES), lambda p, b: (b, layer, 0, p))
    return pl.pallas_call(
        functools.partial(_nat_attn_body, rows=rows),
        out_shape=jax.ShapeDtypeStruct(q.shape, F32),
        grid=(NAT_HEAD_PAIRS, bs),
        in_specs=[spec, spec, spec, cspec, cspec,
                  pl.BlockSpec((None, None) + bias.shape[2:], lambda p, b: (layer, p, 0, 0, 0, 0))],
        out_specs=spec,
        compiler_params=_cparams("parallel", "parallel"),
        name="nat_attn",
    )(q, k, v, cache_k, cache_v, bias)


def _ssd_body(*refs, seq, has_h0):
    zx_ref, dt_ref, cw_ref, cb_ref, dtb_ref, a_ref, d_ref, nw_ref = refs[:8]
    refs = refs[8:]
    if has_h0:
        h0_ref, y_ref = refs[:2]
        refs = refs[2:]
        hout_ref = None
    else:
        y_ref, hout_ref = refs[:2]
        refs = refs[2:]
        h0_ref = None
    xbc_s, y_s, eb_s, dtv_s, st_s, h_s, tr_s = refs

    q = SSD_CHUNK
    nc = seq // q
    nh = SSD_HEADS
    hd = SSD_HEAD_DIM
    hpg = nh // SSD_GROUPS
    b_off = SSD_D
    c_off = SSD_D + SSD_GROUPS * SSD_STATE
    half = SSD_CONV // 2

    zeros8 = jnp.zeros((8, LANES), F32)
    for cblk in range(SSD_CONV_DIM // LANES):
        cs = slice(cblk * LANES, (cblk + 1) * LANES)
        xcol = slice(SSD_D + cblk * LANES, SSD_D + (cblk + 1) * LANES)
        xin = zx_ref[:, xcol]
        top = jnp.concatenate([zeros8, zx_ref[0:16, xcol]], axis=0)
        bot = jnp.concatenate([zx_ref[seq - 16:seq, xcol], zeros8], axis=0)
        acc = cb_ref[:, cs] + cw_ref[half:half + 1, cs] * xin
        acc_t = cb_ref[:, cs] + cw_ref[half:half + 1, cs] * top[8:16]
        acc_b = cb_ref[:, cs] + cw_ref[half:half + 1, cs] * bot[8:16]
        for kk in range(SSD_CONV):
            d = kk - half
            if d == 0:
                continue
            w = cw_ref[kk:kk + 1, cs]
            acc = acc + w * pltpu.roll(xin, (-d) % seq, axis=0)
            acc_t = acc_t + w * pltpu.roll(top, (-d) % 24, axis=0)[8:16]
            acc_b = acc_b + w * pltpu.roll(bot, (-d) % 24, axis=0)[8:16]
        xbc_s[:, cs] = _silu(acc)
        xbc_s[0:8, cs] = _silu(acc_t)
        xbc_s[seq - 8:seq, cs] = _silu(acc_b)

    lane = lax.broadcasted_iota(jnp.int32, (1, LANES), 1)
    xdt = dt_ref[...] + dtb_ref[...]
    dtv = jnp.maximum(xdt, 0.0) + jnp.log1p(jnp.exp(-jnp.abs(xdt)))
    dtv_s[...] = jnp.where(lane < 2 * nh, dtv, 0.0)

    ii = lax.broadcasted_iota(jnp.int32, (q, q), 0)
    jj = lax.broadcasted_iota(jnp.int32, (q, q), 1)
    lower = jj <= ii
    upper = jj >= ii
    is_fwd = lane < nh
    lo64 = lane < hd
    tri = jnp.concatenate([lower, upper], axis=1).astype(BF16)
    wide = 2 * SSD_D
    ek = lax.broadcasted_iota(jnp.int32, (2 * LANES, wide), 0) & (LANES - 1)
    el = lax.broadcasted_iota(jnp.int32, (2 * LANES, wide), 1) >> (hd.bit_length() - 1)
    expand = (ek == el).astype(BF16)
    grow = lax.broadcasted_iota(jnp.int32, (LANES, SSD_D), 0) >= SSD_STATE
    glane = lax.broadcasted_iota(jnp.int32, (LANES, SSD_D), 1) >= hpg * hd
    own = grow == glane

    def split2(x):
        hi = x.astype(BF16)
        return hi, (x - hi.astype(F32)).astype(BF16)

    def lane_expand(x):
        hi, mid = split2(x)
        return _dot(jnp.concatenate([hi, mid], axis=1), expand)

    def stage_cumsum(c):
        r0 = pl.multiple_of(c * q, q)
        k = dict(c=c, rs=pl.ds(r0, q))
        k['dt'] = dtv_s[k['rs'], :]
        da_c = k['dt'] * a_ref[...]
        rhs = jnp.concatenate([jnp.where(is_fwd, da_c, 0.0), jnp.where(is_fwd, 0.0, da_c)], axis=0)
        p0, p1 = split2(rhs)
        p2 = (rhs - p0.astype(F32) - p1.astype(F32)).astype(BF16)
        k['ac'] = _dot(tri, p0) + _dot(tri, p1) + _dot(tri, p2)
        return k

    def stage_expand(k):
        ac, dt_c = k['ac'], k['dt']
        k['ac_t'] = ac.T
        k['dt_t'] = dt_c.T
        last = jnp.where(is_fwd, ac[q - 1:q, :], ac[0:1, :])
        eb_s[k['rs'], :] = lane_expand(jnp.exp(ac))
        k['wb'] = lane_expand(dt_c * jnp.exp(last - ac))

    def stage_states(k):
        rs = k['rs']
        k['xs'] = xbc_s[rs, 0:SSD_D]
        ball = xbc_s[rs, b_off:c_off].astype(BF16)
        call = xbc_s[rs, c_off:c_off + SSD_GROUPS * SSD_STATE]
        xw = (jnp.concatenate([k['xs'], k['xs']], axis=1) * k['wb']).astype(BF16)
        st_s[k['c']] = _dot_tn(ball, xw)
        k['cb'] = [_dot_nt(jnp.where(lo64 if g == 0 else jnp.logical_not(lo64), call, 0.0).astype(BF16),
                           ball) for g in range(SSD_GROUPS)]

    def stage_intra(k, pr):
        ac, ac_t, dt_t = k['ac'], k['ac_t'], k['dt_t']
        ws = []
        for h in (2 * pr, 2 * pr + 1):
            seg_f = ac[:, h:h + 1] - ac_t[h:h + 1, :]
            seg_b = ac[:, nh + h:nh + h + 1] - ac_t[nh + h:nh + h + 1, :]
            w = k['cb'][h // hpg] * (
                jnp.exp(jnp.where(lower, seg_f, -jnp.inf)) * dt_t[h:h + 1, :]
                + jnp.exp(jnp.where(upper, seg_b, -jnp.inf)) * dt_t[nh + h:nh + h + 1, :])
            ws.append(w.astype(BF16))
        xp = k['xs'][:, pr * LANES:(pr + 1) * LANES]
        xbd = jnp.concatenate([jnp.where(lo64, xp, 0.0), jnp.where(lo64, 0.0, xp)],
                              axis=0).astype(BF16)
        y_s[k['rs'], pr * LANES:(pr + 1) * LANES] = _dot(jnp.concatenate(ws, axis=1), xbd)

    ilv = min(SSD_ILV, nc)

    def chunks(cg, carry):
        ks = [stage_cumsum(cg * ilv + j) for j in range(ilv)]
        for k in ks:
            stage_expand(k)
        for k in ks:
            stage_states(k)
        for pr in range(nh // 2):
            for k in ks:
                stage_intra(k, pr)
        return carry

    lax.fori_loop(0, nc // ilv, chunks, 0)

    if has_h0:
        for direction in range(2):
            tr_s[...] = jnp.zeros(tr_s.shape, F32)
            for h in range(nh):
                g = h // hpg
                tr_s[h * hd:(h + 1) * hd, g * SSD_STATE:(g + 1) * SSD_STATE] = h0_ref[direction * nh + h]
            for k in range(SSD_D // LANES):
                h_s[direction, :, k * LANES:(k + 1) * LANES] = tr_s[k * LANES:(k + 1) * LANES, :].T
    else:
        h_s[...] = jnp.zeros(h_s.shape, F32)

    def carry_states(kstep, carry):
        for direction in range(2):
            c = kstep if direction == 0 else nc - 1 - kstep
            r0 = pl.multiple_of(c * q, q)
            rs = pl.ds(r0, q)
            ds_ = slice(direction * SSD_D, (direction + 1) * SSD_D)
            edge = r0 + q - 1 if direction == 0 else r0
            call = xbc_s[rs, c_off:c_off + SSD_GROUPS * SSD_STATE].astype(BF16)
            h_in = h_s[direction]
            y_s[rs, :] += _dot(call, jnp.where(own, h_in, 0.0).astype(BF16)) * eb_s[rs, ds_]
            h_s[direction] = eb_s[pl.ds(edge, 1), ds_] * h_in + st_s[c][:, ds_]
        return carry

    lax.fori_loop(0, nc, carry_states, 0)

    y = y_s[...] + d_ref[...] * xbc_s[:, 0:SSD_D]
    y = y * _silu(zx_ref[:, 0:SSD_D])
    y_ref[...] = _rms(y, nw_ref[...])
    if hout_ref is not None:
        for direction in range(2):
            for k in range(SSD_D // LANES):
                tr_s[k * LANES:(k + 1) * LANES, :] = h_s[direction, :, k * LANES:(k + 1) * LANES].T
            for h in range(nh):
                g = h // hpg
                hout_ref[direction * nh + h] = tr_s[h * hd:(h + 1) * hd,
                                                    g * SSD_STATE:(g + 1) * SSD_STATE]


def _ssd_call(zx, dt, lw, layer, state, seq):
    has_h0 = state is not None
    rows = zx.shape[0]
    nseq = rows // seq
    nc = seq // SSD_CHUNK
    nst = 2 * SSD_HEADS

    def seqspec(w):
        return pl.BlockSpec((seq, w), lambda b: (b, 0))

    def par(shape):
        return _const_spec(shape, layer, single_buffer=False)

    in_specs = [seqspec(1024), seqspec(LANES), par((8, SSD_CONV_DIM)), par((1, SSD_CONV_DIM)),
                par((1, LANES)), par((1, LANES)), par((1, SSD_D)), par((1, SSD_D))]
    args = [zx, dt, lw['conv_w'], lw['conv_b'], lw['dt_bias'], lw['a'], lw['d'], lw['norm']]
    y_shape = jax.ShapeDtypeStruct((rows, SSD_D), F32)
    if has_h0:
        in_specs.append(pl.BlockSpec((None, None, nst, SSD_HEAD_DIM, SSD_STATE),
                                     lambda b: (b, layer, 0, 0, 0)))
        args.append(state)
        out_shape = y_shape
        out_specs = seqspec(SSD_D)
    else:
        out_shape = [y_shape, jax.ShapeDtypeStruct((nseq, nst, SSD_HEAD_DIM, SSD_STATE), F32)]
        out_specs = [seqspec(SSD_D),
                     pl.BlockSpec((None, nst, SSD_HEAD_DIM, SSD_STATE), lambda b: (b, 0, 0, 0))]
    return pl.pallas_call(
        functools.partial(_ssd_body, seq=seq, has_h0=has_h0),
        out_shape=out_shape,
        grid=(nseq,),
        in_specs=in_specs,
        out_specs=out_specs,
        scratch_shapes=[
            pltpu.VMEM((seq, SSD_CONV_DIM), F32),
            pltpu.VMEM((seq, SSD_D), F32),
            pltpu.VMEM((seq, 2 * SSD_D), F32),
            pltpu.VMEM((seq, LANES), F32),
            pltpu.VMEM((nc, SSD_GROUPS * SSD_STATE, 2 * SSD_D), F32),
            pltpu.VMEM((2, SSD_GROUPS * SSD_STATE, SSD_D), F32),
            pltpu.VMEM((SSD_D, SSD_GROUPS * SSD_STATE), F32),
        ],
        compiler_params=_cparams("parallel"),
        name="ssd_mixer",
    )(*args)


def _s5_tables(a_re, a_im, log_dt, b_re, b_im, c_re, c_im):
    t = S5_T
    gh = S5_HGROUPS
    f32 = lambda x: x.astype(F32)
    lam_r, lam_i = f32(a_re), f32(a_im)
    step = jnp.exp(f32(log_dt))[..., None]
    xr, xi = lam_r * step, lam_i * step

    def powers(ks):
        ks = jnp.asarray(ks, F32)[None, None, :, None, None]
        mag = jnp.exp(xr[:, :, None] * ks)
        return mag * jnp.cos(xi[:, :, None] * ks), mag * jnp.sin(xi[:, :, None] * ks)

    pw_r, pw_i = powers(np.arange(t + 1))
    nr, ni = pw_r[:, :, 1] - 1.0, pw_i[:, :, 1]
    den = lam_r * lam_r + lam_i * lam_i
    fr = (nr * lam_r + ni * lam_i) / den
    fi = (ni * lam_r - nr * lam_i) / den
    br, bi = f32(b_re)[:, None], f32(b_im)[:, None]
    bb_r = fr[..., None] * br - fi[..., None] * bi
    bb_i = fr[..., None] * bi + fi[..., None] * br
    cr, ci = f32(c_re), f32(c_im)

    kern = []
    for direction in range(2):
        pr = jnp.swapaxes(pw_r[:, direction, :t], 1, 2)[:, :, :, None, :]
        pi = jnp.swapaxes(pw_i[:, direction, :t], 1, 2)[:, :, :, None, :]
        cpr = cr[:, :, None] * pr - ci[:, :, None] * pi
        cpi = cr[:, :, None] * pi + ci[:, :, None] * pr
        lhs = jnp.concatenate([cpr, -cpi], axis=-1).reshape(DEPTH, S5_GROUPS, t * S5_GROUP_CH, 2 * S5_STATE)
        rhs = jnp.concatenate([bb_r[:, direction], bb_i[:, direction]], axis=2)
        k = jnp.einsum('dgxn,dgnk->dgxk', lhs, rhs, precision=HIGHEST)
        kern.append(jnp.swapaxes(k.reshape(DEPTH, S5_GROUPS, t, S5_GROUP_CH, S5_GROUP_CH), 1, 2))
    kf, kb = kern
    lag = [kb[:, -d] if d < 0 else (kf[:, 0] + kb[:, 0] if d == 0 else kf[:, d])
           for d in range(-(t - 1), t)]
    kt = jnp.stack(lag, axis=1).reshape(DEPTH, S5_LAGS, 2, gh, S5_GROUP_CH, S5_GROUP_CH)
    kt = jnp.transpose(kt, (0, 2, 1, 5, 3, 4)).reshape(DEPTH, 2, S5_LAGS, S5_GROUP_CH, LANES)

    def by_half(x):
        return jnp.transpose(x.reshape(DEPTH, -1, 2, gh, S5_STATE), (0, 2, 1, 3, 4))

    def dir_powers(direction, ks):
        pr, pi = powers(np.asarray(ks))
        return pr[:, direction], pi[:, direction]

    bt_r = jnp.transpose(bb_r.reshape(DEPTH, 2, 2, gh, S5_STATE, S5_GROUP_CH), (0, 1, 2, 5, 3, 4))
    bt_i = jnp.transpose(bb_i.reshape(DEPTH, 2, 2, gh, S5_STATE, S5_GROUP_CH), (0, 1, 2, 5, 3, 4))
    sw = []
    for direction, ks in ((0, [t - 1 - s for s in range(t)]), (1, list(range(t)))):
        pr, pi = dir_powers(direction, ks)
        pr = by_half(pr)[:, :, :, None]
        pi = by_half(pi)[:, :, :, None]
        wr, wi = bt_r[:, direction][:, :, None], bt_i[:, direction][:, :, None]
        sw += [pr * wr - pi * wi, pr * wi + pi * wr]
    sw = jnp.stack(sw, axis=3).reshape(DEPTH, 2, t, 4, S5_GROUP_CH, gh * S5_STATE)
    ct_r = jnp.transpose(cr.reshape(DEPTH, 2, gh, S5_GROUP_CH, S5_STATE), (0, 1, 4, 2, 3))
    ct_i = jnp.transpose(ci.reshape(DEPTH, 2, gh, S5_GROUP_CH, S5_STATE), (0, 1, 4, 2, 3))
    aw = []
    for direction, ks in ((0, list(range(1, t + 1))), (1, [t - k for k in range(t)])):
        pr, pi = dir_powers(direction, ks)
        pr = jnp.swapaxes(by_half(pr), 3, 4)[..., None]
        pi = jnp.swapaxes(by_half(pi), 3, 4)[..., None]
        wr, wi = ct_r[:, :, None], ct_i[:, :, None]
        aw += [wr * pr - wi * pi, -(wr * pi + wi * pr)]
    aw = jnp.stack(aw, axis=2).reshape(DEPTH, 2, 4, t, S5_STATE, LANES)

    apow = jnp.stack([pw_r[:, 0, t], pw_i[:, 0, t], pw_r[:, 1, t], pw_i[:, 1, t]], axis=1)
    seg_r, seg_i = powers(t * np.arange(S5_NC + 1))
    back_r, back_i = dir_powers(1, t * (S5_NC - 1 - np.arange(S5_NC)))
    pseg = jnp.stack([seg_r[:, 0, :S5_NC], seg_i[:, 0, :S5_NC], back_r, back_i], axis=1)
    nkb = S5_CW // LANES
    pseg = jnp.transpose(pseg.reshape(DEPTH, 4, S5_NC, nkb, LANES), (0, 1, 3, 2, 4))
    pseg = pseg.reshape(DEPTH, 4 * nkb, S5_NC, LANES)
    aseg = jnp.stack([seg_r[:, 0, S5_NC], seg_i[:, 0, S5_NC],
                      seg_r[:, 1, S5_NC], seg_i[:, 1, S5_NC]], axis=1)
    return (kt, sw, aw, apow.reshape(DEPTH, 1, S5_SW), pseg, aseg.reshape(DEPTH, 1, S5_SW))


def _s5_expand_operators(kt_ref, sw_ref, aw_ref, toep_s, sop_s, aop_s):
    gh = S5_HGROUPS
    sh_ch = S5_GROUP_CH.bit_length() - 1
    sh_st = S5_STATE.bit_length() - 1

    def group_mask(shape, row_shift, lane_shift):
        r = lax.broadcasted_iota(jnp.int32, shape, 0) >> row_shift
        c = lax.broadcasted_iota(jnp.int32, shape, 1) >> lane_shift
        return r == c

    def blockdiag(x, mask):
        return jnp.where(mask, jnp.concatenate([x] * gh, axis=0), 0.0).astype(BF16)

    m_kk = group_mask((LANES, LANES), sh_ch, sh_ch)
    m_ks = group_mask((LANES, gh * S5_STATE), sh_ch, sh_st)
    m_sk = group_mask((gh * S5_STATE, LANES), sh_st, sh_ch)
    cw = gh * S5_STATE
    for hf in range(2):
        lags = [blockdiag(kt_ref[hf, d], m_kk) for d in range(S5_LAGS)]
        for s in range(S5_T):
            for t in range(S5_T):
                toep_s[hf, s * LANES:(s + 1) * LANES, t * LANES:(t + 1) * LANES] = lags[t - s + S5_T - 1]
            for comp in range(4):
                sop_s[hf, s * LANES:(s + 1) * LANES, comp * cw:(comp + 1) * cw] = blockdiag(
                    sw_ref[hf, s, comp], m_ks)
        for comp in range(4):
            for t in range(S5_T):
                aop_s[hf, comp * cw:(comp + 1) * cw, t * LANES:(t + 1) * LANES] = blockdiag(
                    aw_ref[hf, comp, t], m_sk)


def _s5_body(up_ref, us_ref, kt_ref, sw_ref, aw_ref, at_ref, pseg_ref, aseg_ref, h0_ref,
             y_ref, hout_ref, toep_s, sop_s, aop_s, st_s, hin_s, yacc_s, xh_s, *, nblk_p, nseg_s):
    i = pl.program_id(0)

    @pl.when(i == 0)
    def _():
        _s5_expand_operators(kt_ref, sw_ref, aw_ref, toep_s, sop_s, aop_s)

    cw = S5_CW
    nkb = cw // LANES

    def seg_rows(c, s):
        return pl.ds(c * S5_T + s, S5_HB, stride=S5_T * S5_NC)

    def chunk_rows(c):
        return slice(c * S5_HB, (c + 1) * S5_HB)

    def gather(src_ref):
        for hf in range(2):
            for c in range(0, S5_NC, 2):
                xh_s[hf, c * S5_HB:(c + 2) * S5_HB, :] = jnp.concatenate(
                    [jnp.concatenate([src_ref[hf, seg_rows(cc, s), :] for s in range(S5_T)], axis=1)
                     for cc in (c, c + 1)], axis=0).astype(BF16)

    @pl.when(i < nblk_p)
    def _():
        gather(up_ref)

    @pl.when(i >= nblk_p)
    def _():
        gather(us_ref)

    for hf in range(2):
        xh = xh_s[hf]
        yacc_s[hf] = _dot(xh, toep_s[hf])
        sh = _dot(xh, sop_s[hf])
        for comp in range(4):
            for k in range(nkb // 2):
                col = (comp * (nkb // 2) + k) * LANES
                st_s[comp * nkb + hf * (nkb // 2) + k] = sh[:, col:col + LANES]

    def load(ref, comp, rows):
        return jnp.concatenate([ref[comp * nkb + k, rows, :] for k in range(nkb)], axis=1)

    def store(ref, comp, rows, val):
        for k in range(nkb):
            ref[comp * nkb + k, rows, :] = val[:, k * LANES:(k + 1) * LANES]

    def cmul(ar, ai, hr, hi):
        return ar * hr - ai * hi, ar * hi + ai * hr

    at = at_ref[...]
    ar_f, ai_f, ar_b, ai_b = [at[:, k * cw:(k + 1) * cw] for k in range(4)]

    def step(c, carry):
        hfr, hfi, hbr, hbi = carry
        rf = chunk_rows(c)
        rb = chunk_rows(S5_NC - 1 - c)
        for comp, val, rows in ((0, hfr, rf), (1, hfi, rf), (2, hbr, rb), (3, hbi, rb)):
            store(hin_s, comp, rows, val)
        fr, fi = cmul(ar_f, ai_f, hfr, hfi)
        br, bi = cmul(ar_b, ai_b, hbr, hbi)
        return (fr + load(st_s, 0, rf), fi + load(st_s, 1, rf),
                br + load(st_s, 2, rb), bi + load(st_s, 3, rb))

    zero = jnp.zeros((S5_HB, cw), F32)
    fin = (zero, zero, zero, zero)
    for c in range(S5_NC):
        fin = step(c, fin)
    hout_ref[...] = jnp.concatenate(fin, axis=-1)

    @pl.when(i >= nblk_p)
    def _():
        h0 = h0_ref[...]
        aseg = aseg_ref[...]
        sr_f, si_f, sr_b, si_b = [aseg[:, k * cw:(k + 1) * cw] for k in range(4)]
        ent = [[None] * S5_HB for _ in range(4)]
        for s in range(S5_HB // nseg_s):
            hr, hi = h0[s:s + 1, 0:cw], h0[s:s + 1, cw:2 * cw]
            for j in range(nseg_s):
                v = s * nseg_s + j
                ent[0][v], ent[1][v] = hr, hi
                hr, hi = cmul(sr_f, si_f, hr, hi)
                hr, hi = hr + fin[0][v:v + 1], hi + fin[1][v:v + 1]
            hr, hi = h0[s:s + 1, 2 * cw:3 * cw], h0[s:s + 1, 3 * cw:4 * cw]
            for j in range(nseg_s - 1, -1, -1):
                v = s * nseg_s + j
                ent[2][v], ent[3][v] = hr, hi
                hr, hi = cmul(sr_b, si_b, hr, hi)
                hr, hi = hr + fin[2][v:v + 1], hi + fin[3][v:v + 1]
        ent = [jnp.concatenate(rows, axis=0) for rows in ent]
        for direction in range(2):
            for k in range(nkb):
                kr = (2 * direction) * nkb + k
                ki = (2 * direction + 1) * nkb + k
                er = ent[2 * direction][:, k * LANES:(k + 1) * LANES]
                ei = ent[2 * direction + 1][:, k * LANES:(k + 1) * LANES]
                for c in range(S5_NC):
                    dr, di = cmul(pseg_ref[kr, c:c + 1, :], pseg_ref[ki, c:c + 1, :], er, ei)
                    hin_s[kr, chunk_rows(c), :] += dr
                    hin_s[ki, chunk_rows(c), :] += di

    for hf in range(2):
        hh = jnp.concatenate(
            [hin_s[comp * nkb + hf * (nkb // 2) + k] for comp in range(4) for k in range(nkb // 2)],
            axis=1).astype(BF16)
        y = yacc_s[hf] + _dot(hh, aop_s[hf])
        for c in range(S5_NC):
            for t in range(S5_T):
                y_ref[hf, seg_rows(c, t), :] = y[c * S5_HB:(c + 1) * S5_HB, t * LANES:(t + 1) * LANES]


def _s5_call(u_p, u_s, tables, layer, h0, seq_p, seq_s):
    kt, sw, aw, apow, pseg, aseg = tables
    seg_tok = S5_T * S5_NC
    assert seq_p == seg_tok and seq_s % seg_tok == 0 and S5_HB % (seq_s // seg_tok) == 0
    blk_tok = S5_ROWS * S5_T
    nblk_p = u_p.shape[1] // blk_tok
    nblk = nblk_p + u_s.shape[1] // blk_tok
    uh = S5_D // LANES

    def par(a):
        return _const_spec(a.shape[1:], layer)

    nlb = S5_SW // LANES
    return pl.pallas_call(
        functools.partial(_s5_body, nblk_p=nblk_p, nseg_s=seq_s // seg_tok),
        out_shape=[jax.ShapeDtypeStruct((uh, nblk * blk_tok, LANES), F32),
                   jax.ShapeDtypeStruct((nblk, S5_HB, S5_SW), F32)],
        grid=(nblk,),
        in_specs=[pl.BlockSpec((uh, blk_tok, LANES), lambda i: (0, jnp.minimum(i, nblk_p - 1), 0)),
                  pl.BlockSpec((uh, blk_tok, LANES), lambda i: (0, jnp.maximum(i - nblk_p, 0), 0)),
                  par(kt), par(sw), par(aw), par(apow), par(pseg), par(aseg),
                  pl.BlockSpec((None, None, S5_HB, S5_SW), lambda i: (layer, i, 0, 0))],
        out_specs=[pl.BlockSpec((uh, blk_tok, LANES), lambda i: (0, i, 0)),
                   pl.BlockSpec((None, S5_HB, S5_SW), lambda i: (i, 0, 0))],
        scratch_shapes=[pltpu.VMEM((2, S5_KW, S5_KW), BF16), pltpu.VMEM((2, S5_KW, S5_SH), BF16),
                        pltpu.VMEM((2, S5_SH, S5_KW), BF16),
                        pltpu.VMEM((nlb, S5_ROWS, LANES), F32), pltpu.VMEM((nlb, S5_ROWS, LANES), F32),
                        pltpu.VMEM((2, S5_ROWS, S5_KW), F32), pltpu.VMEM((2, S5_ROWS, S5_KW), BF16)],
        compiler_params=_cparams("arbitrary"),
        name="s5_mixer",
    )(u_p, u_s, kt, sw, aw, apow, pseg, aseg, h0)


def _gelu_tanh(x):
    return 0.5 * x * (1.0 + jnp.tanh(math.sqrt(2.0 / math.pi) * (x + 0.044715 * (x * x * x))))


def _out_body(*refs, final):
    (x_ref, yssd_ref, ynat_ref, u_ref, y5_ref, g1_ref, m2_ref, d5_ref,
     wglu_ref, bglu_ref, wout_ref, nm_ref, w1_ref, w2_ref) = refs[:14]
    if final:
        nf_ref, o_ref = refs[14:]
    else:
        (o_ref,) = refs[14:]
    uh = S5_D // LANES
    for sb in range(ROW_TILE // TOK_TILE):
        rows = slice(sb * TOK_TILE, (sb + 1) * TOK_TILE)
        y5_in = jnp.concatenate([y5_ref[hf, rows, :] for hf in range(uh)], axis=1)
        u = jnp.concatenate([u_ref[hf, rows, :] for hf in range(uh)], axis=1)
        g = _gelu_tanh(y5_in + d5_ref[...] * u)
        y5 = g * jax.nn.sigmoid(_dot(g.astype(BF16), wglu_ref[...]) + bglu_ref[...])
        mix = (_dot(yssd_ref[rows, :].astype(BF16), wout_ref[0:SSD_D, :])
               + _dot(y5.astype(BF16), wout_ref[SSD_D:SSD_D + S5_D, :])
               + _dot(ynat_ref[rows, :].astype(BF16), wout_ref[SSD_D + S5_D:, :]))
        x = x_ref[rows, :] + g1_ref[...] * mix
        h2 = _rms(x, nm_ref[...]) * (1.0 + m2_ref[:, D_MODEL:2 * D_MODEL]) + m2_ref[:, 0:D_MODEL]
        f = jnp.maximum(_dot(h2.astype(BF16), w1_ref[...]), 0.0)
        f = (f * f).astype(BF16)
        x = x + m2_ref[:, 2 * D_MODEL:3 * D_MODEL] * _dot(f, w2_ref[...])
        if final:
            x = _rms(x, nf_ref[...])
        o_ref[rows, :] = x


def _out_call(x, y_ssd, y_nat, u, y5, y5_tile_off, cond_idx, mods4, lw, norm_f, layer, final):
    rows = x.shape[0]
    uh = S5_D // LANES

    def tile(w):
        return pl.BlockSpec((ROW_TILE, w), lambda i: (i, 0))

    def par(shape):
        return _const_spec(shape, layer)

    in_specs = [
        tile(D_MODEL), tile(SSD_D), tile(NAT_D),
        pl.BlockSpec((uh, ROW_TILE, LANES), lambda i: (0, i, 0)),
        pl.BlockSpec((uh, ROW_TILE, LANES), lambda i: (0, i + y5_tile_off, 0)),
        pl.BlockSpec((None, None, 1, D_MODEL), lambda i: (layer, cond_idx(i), 0, 2)),
        pl.BlockSpec((None, None, 1, 3 * D_MODEL), lambda i: (layer, cond_idx(i), 0, 1)),
        par((1, S5_D)), par((S5_D, S5_D)), par((1, S5_D)),
        par((D_MODEL, D_MODEL)), par((1, D_MODEL)),
        par((D_MODEL, D_FF)), par((D_FF, D_MODEL))]
    args = [x, y_ssd, y_nat, u, y5, mods4, mods4, lw['s5_d'], lw['w_glu'], lw['b_glu'],
            lw['w_out'], lw['norm_mlp'], lw['w_ff1'], lw['w_ff2']]
    if final:
        in_specs.append(pl.BlockSpec((1, D_MODEL), lambda i: (0, 0)))
        args.append(norm_f.reshape(1, D_MODEL))
    return pl.pallas_call(
        functools.partial(_out_body, final=final),
        out_shape=jax.ShapeDtypeStruct((rows, D_MODEL), F32),
        grid=(rows // ROW_TILE,),
        in_specs=in_specs,
        out_specs=tile(D_MODEL),
        compiler_params=_cparams("parallel"),
        name="out_mlp",
    )(*args)


def _lane_pad(x):
    return jnp.pad(x, [(0, 0)] * (x.ndim - 1) + [(0, LANES - x.shape[-1])])


def kernel(x_prompt, x_sample, cache_nat_k, cache_nat_v, state_ssd, state_s5_re, state_s5_im,
           c, c_ctx, w_mod, b_mod, norm_mix, norm_mlp, w_in, ssd_conv_w, ssd_conv_b,
           ssd_dt_bias, ssd_a_log, ssd_d, ssd_norm, s5_a_re, s5_a_im, s5_log_dt,
           s5_b_re, s5_b_im, s5_c_re, s5_c_im, s5_d, s5_w_glu, s5_b_glu, nat_rpb,
           w_out, w_ff1, w_ff2, norm_f):
    bp, seq_p, _ = x_prompt.shape
    bs, seq_s, _ = x_sample.shape
    assert seq_p == TOK_TILE and seq_s == 4 * TOK_TILE
    rows_p = bp * seq_p
    rows_s = bs * seq_s

    ncp = -(-(1 + bs) // 8) * 8
    cond = jnp.concatenate([c_ctx[None, :], c, jnp.zeros((ncp - 1 - bs, D_MODEL), F32)], axis=0)
    mods = _mods_call(cond, w_mod, b_mod)
    mods4 = mods.reshape(DEPTH, ncp, 1, N_MOD * D_MODEL)

    assert w_in.shape[-1] == IN_COLS
    w_in_p = w_in.astype(BF16)
    ssd_w = {
        'conv_w': jnp.pad(ssd_conv_w.astype(F32), [(0, 0), (0, 8 - SSD_CONV), (0, 0)]),
        'conv_b': ssd_conv_b.astype(F32).reshape(DEPTH, 1, SSD_CONV_DIM),
        'dt_bias': _lane_pad(ssd_dt_bias.astype(F32).reshape(DEPTH, 1, 2 * SSD_HEADS)),
        'a': _lane_pad(-jnp.exp(ssd_a_log.astype(F32)).reshape(DEPTH, 1, 2 * SSD_HEADS)),
        'd': jnp.repeat(ssd_d.astype(F32), SSD_HEAD_DIM, axis=-1).reshape(DEPTH, 1, SSD_D),
        'norm': ssd_norm.astype(F32).reshape(DEPTH, 1, SSD_D),
    }
    out_w = {
        's5_d': s5_d.astype(F32).reshape(DEPTH, 1, S5_D),
        'w_glu': s5_w_glu.astype(BF16),
        'b_glu': s5_b_glu.astype(F32).reshape(DEPTH, 1, S5_D),
        'w_out': w_out.astype(BF16),
        'norm_mlp': norm_mlp.astype(F32).reshape(DEPTH, 1, D_MODEL),
        'w_ff1': w_ff1.astype(BF16),
        'w_ff2': w_ff2.astype(BF16),
    }
    s5_tabs = _s5_tables(s5_a_re, s5_a_im, s5_log_dt, s5_b_re, s5_b_im, s5_c_re, s5_c_im)
    nat_bias = _nat_pair_tiles(nat_rpb)

    cache_k = cache_nat_k.reshape(bs, DEPTH, -1, NAT_D)
    cache_v = cache_nat_v.reshape(bs, DEPTH, -1, NAT_D)
    st_ssd = state_ssd.reshape(bs, DEPTH, 2 * SSD_HEADS, SSD_HEAD_DIM, SSD_STATE)
    s5_tok = S5_ROWS * S5_T
    assert rows_p % s5_tok == 0 and rows_s % s5_tok == 0 and s5_tok % seq_s == 0
    nseq_s = s5_tok // seq_s
    st5 = jnp.stack([state_s5_re[:, :, 0], state_s5_im[:, :, 0],
                     state_s5_re[:, :, 1], state_s5_im[:, :, 1]], axis=2)
    st5 = jnp.transpose(st5.astype(F32).reshape(bs // nseq_s, nseq_s, DEPTH, S5_SW), (2, 0, 1, 3))
    st5 = jnp.pad(st5, [(0, 0), (rows_p // s5_tok, 0), (0, S5_HB - nseq_s), (0, 0)])

    assert rows_p % ROW_TILE == 0 and seq_s % ROW_TILE == 0
    tiles_per_seq = seq_s // ROW_TILE
    cond_p = lambda i: 0
    cond_s = lambda i: 1 + i // tiles_per_seq
    x_p = x_prompt.reshape(rows_p, D_MODEL)
    x_s = x_sample.reshape(rows_s, D_MODEL)
    new_k, new_v, new_ssd, new_s5 = [], [], [], []
    for l in range(DEPTH):
        final = l == DEPTH - 1
        zx_p, dt_p, u_p, q_p, k_p, v_p = _inproj_call(x_p, cond_p, mods4, norm_mix, w_in_p, l)
        zx_s, dt_s, u_s, q_s, k_s, v_s = _inproj_call(x_s, cond_s, mods4, norm_mix, w_in_p, l)
        new_k.append(k_p.reshape(bp, seq_p, NAT_D))
        new_v.append(v_p.reshape(bp, seq_p, NAT_D))

        y_nat_p = _ctx_attn_call(q_p, k_p, v_p)
        y_nat_s = _nat_attn_call(q_s, k_s, v_s, cache_k, cache_v, nat_bias, l, bs, seq_s)

        y_ssd_p, ssd_l = _ssd_call(zx_p, dt_p, ssd_w, l, None, seq_p)
        y_ssd_s = _ssd_call(zx_s, dt_s, ssd_w, l, st_ssd, seq_s)
        new_ssd.append(ssd_l)

        y5, s5_l = _s5_call(u_p, u_s, s5_tabs, l, st5, seq_p, seq_s)
        new_s5.append(s5_l[:rows_p // s5_tok])

        x_p = _out_call(x_p, y_ssd_p, y_nat_p, u_p, y5, 0, cond_p, mods4, out_w, norm_f, l, final)
        x_s = _out_call(x_s, y_ssd_s, y_nat_s, u_s, y5, rows_p // ROW_TILE, cond_s,
                        mods4, out_w, norm_f, l, final)

    y_prompt = x_p.reshape(bp, seq_p, D_MODEL)
    y_sample = x_s.reshape(bs, seq_s, D_MODEL)
    out_k = jnp.stack(new_k, axis=1).reshape(bp, DEPTH, seq_p, NAT_HEADS, NAT_HEAD_DIM)
    out_v = jnp.stack(new_v, axis=1).reshape(bp, DEPTH, seq_p, NAT_HEADS, NAT_HEAD_DIM)
    out_ssd = jnp.stack(new_ssd, axis=1).reshape(bp, DEPTH, 2, SSD_HEADS, SSD_HEAD_DIM, SSD_STATE)
    s5 = jnp.stack(new_s5, axis=0)[:, :, :s5_tok // seq_p]
    s5 = s5.reshape(DEPTH, bp, 4, S5_GROUPS, S5_STATE)
    s5 = jnp.transpose(s5, (1, 0, 2, 3, 4))
    out_re = s5[:, :, 0::2]
    out_im = s5[:, :, 1::2]
    return y_prompt, y_sample, out_k, out_v, out_ssd, out_re, out_im
```

```python
import functools
import math

import numpy as np
import jax
import jax.numpy as jnp
from jax import lax
from jax.experimental import pallas as pl
from jax.experimental.pallas import tpu as pltpu

F32 = jnp.float32
BF16 = jnp.bfloat16
HIGHEST = lax.Precision.HIGHEST

D_MODEL = 1024
DEPTH = 4
GRID_W = 64
SSD_HEADS = 6
SSD_HEAD_DIM = 64
SSD_D = SSD_HEADS * SSD_HEAD_DIM
SSD_GROUPS = 2
SSD_STATE = 64
SSD_CONV = 5
SSD_CHUNK = 128
SSD_CONV_DIM = SSD_D + 2 * SSD_GROUPS * SSD_STATE
SSD_ILV = 4
S5_GROUPS = 16
S5_GROUP_CH = 16
S5_D = S5_GROUPS * S5_GROUP_CH
S5_STATE = 64
NAT_HEADS = 6
NAT_HEAD_DIM = 64
NAT_D = NAT_HEADS * NAT_HEAD_DIM
NAT_KH = 8
NAT_KW = 16
D_FF = 4 * D_MODEL
N_MOD = 6
EPS = 1e-6

LANES = 128
TOK_TILE = 256
ROW_TILE = 512
IN_COLS = SSD_D + SSD_CONV_DIM + 2 * SSD_HEADS + S5_D + 3 * NAT_D
S5_T = 8
S5_HGROUPS = LANES // S5_GROUP_CH
S5_KW = S5_T * LANES
S5_CW = S5_GROUPS * S5_STATE
S5_SW = 4 * S5_CW
S5_SH = S5_SW // 2
S5_NC = 32
S5_HB = 8
S5_ROWS = S5_HB * S5_NC
S5_LAGS = 2 * S5_T - 1
VMEM_LIMIT = 56 * 1024 * 1024


def _cparams(*sem):
    return pltpu.CompilerParams(dimension_semantics=sem, vmem_limit_bytes=VMEM_LIMIT)


def _dot(a, b, precision=None):
    return jnp.dot(a, b, preferred_element_type=F32, precision=precision)


def _dot_nt(a, b, precision=None):
    return lax.dot_general(a, b, (((1,), (1,)), ((), ())), preferred_element_type=F32,
                           precision=precision)


def _dot_tn(a, b, precision=None):
    return lax.dot_general(a, b, (((0,), (0,)), ((), ())), preferred_element_type=F32,
                           precision=precision)


def _silu(x):
    hx = 0.5 * x
    return hx + hx * jnp.tanh(hx)


def _rms(x, g):
    return x * lax.rsqrt(jnp.mean(x * x, axis=-1, keepdims=True) + EPS) * g


def _const_spec(shape, layer, single_buffer=True):
    return pl.BlockSpec((None,) + tuple(shape), lambda *_: (layer,) + (0,) * len(shape),
                        pipeline_mode=pl.Buffered(1) if single_buffer else None)


def _mods_body(cond_ref, w_ref, b_ref, o_ref):
    s = _silu(cond_ref[...])
    o_ref[...] = _dot(s.astype(BF16), w_ref[...].astype(BF16)) + b_ref[...]


def _mods_call(cond, w_mod, b_mod):
    ncp = cond.shape[0]
    blk = 2 * D_MODEL
    return pl.pallas_call(
        _mods_body,
        out_shape=jax.ShapeDtypeStruct((DEPTH, ncp, N_MOD * D_MODEL), F32),
        grid=(DEPTH, N_MOD * D_MODEL // blk),
        in_specs=[
            pl.BlockSpec((ncp, D_MODEL), lambda l, j: (0, 0)),
            pl.BlockSpec((None, D_MODEL, blk), lambda l, j: (l, 0, j)),
            pl.BlockSpec((None, 1, blk), lambda l, j: (l, 0, j)),
        ],
        out_specs=pl.BlockSpec((None, ncp, blk), lambda l, j: (l, 0, j)),
        compiler_params=_cparams("arbitrary", "arbitrary"),
        name="adaln_mods",
    )(cond, w_mod, b_mod.reshape(DEPTH, 1, N_MOD * D_MODEL))


def _inproj_body(x_ref, mod_ref, g_ref, w_ref, zx_ref, dt_ref, u_ref, q_ref, k_ref, v_ref):
    o_dt = SSD_D + SSD_CONV_DIM
    for sb in range(ROW_TILE // TOK_TILE):
        rows = slice(sb * TOK_TILE, (sb + 1) * TOK_TILE)
        x = x_ref[rows, :]
        h = _rms(x, g_ref[...]) * (1.0 + mod_ref[:, D_MODEL:2 * D_MODEL]) + mod_ref[:, 0:D_MODEL]
        p = _dot(h.astype(BF16), w_ref[...])
        zx_ref[rows, :] = p[:, 0:o_dt]
        dt_ref[rows, :] = p[:, o_dt:o_dt + LANES]
        rest = p[:, o_dt + 2 * SSD_HEADS:]
        for hf in range(S5_D // LANES):
            u_ref[hf, rows, :] = rest[:, hf * LANES:(hf + 1) * LANES]
        q_ref[rows, :] = rest[:, S5_D:S5_D + NAT_D]
        k_ref[rows, :] = rest[:, S5_D + NAT_D:S5_D + 2 * NAT_D]
        v_ref[rows, :] = rest[:, S5_D + 2 * NAT_D:S5_D + 3 * NAT_D]


def _inproj_call(x, cond_idx, mods4, norm_mix, w_in_p, layer):
    rows = x.shape[0]

    def tile(w):
        return pl.BlockSpec((ROW_TILE, w), lambda i: (i, 0))

    def out(w):
        return jax.ShapeDtypeStruct((rows, w), F32)

    uh = S5_D // LANES
    return pl.pallas_call(
        _inproj_body,
        out_shape=[out(1024), out(LANES), jax.ShapeDtypeStruct((uh, rows, LANES), F32),
                   out(NAT_D), out(NAT_D), out(NAT_D)],
        grid=(rows // ROW_TILE,),
        in_specs=[
            tile(D_MODEL),
            pl.BlockSpec((None, None, 1, 2 * D_MODEL), lambda i: (layer, cond_idx(i), 0, 0)),
            _const_spec((1, D_MODEL), layer, single_buffer=False),
            _const_spec((D_MODEL, IN_COLS), layer),
        ],
        out_specs=[tile(1024), tile(LANES), pl.BlockSpec((uh, ROW_TILE, LANES), lambda i: (0, i, 0)),
                   tile(NAT_D), tile(NAT_D), tile(NAT_D)],
        compiler_params=_cparams("parallel"),
        name="in_proj",
    )(x, mods4, norm_mix.reshape(DEPTH, 1, D_MODEL), w_in_p)


NAT_SCALE = NAT_HEAD_DIM ** -0.5


def _values_with_ones(v):
    return jnp.concatenate([v, jnp.ones(v.shape, BF16)], axis=1)


def _pair_attention(q, keys, v_ext, bias, lo):
    outs = []
    for hh in range(2):
        qm = jnp.where(lo if hh == 0 else jnp.logical_not(lo), q, 0.0).astype(BF16)
        s = _dot_nt(qm, keys)
        if bias is not None:
            b = bias(hh)
            nb = b.shape[1]
            s = jnp.concatenate([s[:, 0:nb] + b, s[:, nb:]], axis=1)
        p = jnp.exp(s - jnp.max(s, axis=-1, keepdims=True)).astype(BF16)
        o = _dot(p, v_ext)
        outs.append(o[:, 0:LANES] / o[:, LANES:2 * LANES])
    return jnp.where(lo, outs[0], outs[1])


def _ctx_attn_body(q_ref, k_ref, v_ref, o_ref):
    lo = lax.broadcasted_iota(jnp.int32, (1, LANES), 1) < NAT_HEAD_DIM
    for sq in range(ROW_TILE // TOK_TILE):
        rs = slice(sq * TOK_TILE, (sq + 1) * TOK_TILE)
        for p in range(NAT_D // LANES):
            cs = slice(p * LANES, (p + 1) * LANES)
            o_ref[rs, cs] = _pair_attention(q_ref[rs, cs] * NAT_SCALE, k_ref[rs, cs].astype(BF16),
                                            _values_with_ones(v_ref[rs, cs].astype(BF16)), None, lo)


def _ctx_attn_call(q, k, v):
    rows = q.shape[0]
    spec = pl.BlockSpec((ROW_TILE, NAT_D), lambda b: (b, 0))
    return pl.pallas_call(
        _ctx_attn_body,
        out_shape=jax.ShapeDtypeStruct((rows, NAT_D), F32),
        grid=(rows // ROW_TILE,),
        in_specs=[spec, spec, spec],
        out_specs=spec,
        compiler_params=_cparams("parallel"),
        name="ctx_attn",
    )(q, k, v)


NAT_QROWS = 4
NAT_HEAD_PAIRS = NAT_HEADS // 2
NAT_NPAIR = 2 * NAT_KH


def _nat_window_start(r, rows):
    kh = min(NAT_KH, rows)
    return int(np.clip(r - kh // 2, 0, rows - kh))


def _nat_blocks(rows):
    kh = min(NAT_KH, rows)
    out = []
    for j in range(rows // NAT_QROWS):
        rs = [_nat_window_start(r, rows) for r in range(j * NAT_QROWS, (j + 1) * NAT_QROWS)]
        first = min(rs)
        n = -(-(max(rs) + kh - first) // NAT_QROWS) * NAT_QROWS
        first = min(first, rows - n)
        out.append((first, n))
    return out


def _nat_pair_tiles(rpb):
    cols = np.arange(GRID_W)
    c_start = np.clip(cols - NAT_KW // 2, 0, GRID_W - NAT_KW)
    col_mask = (cols[None, :] >= c_start[:, None]) & (cols[None, :] < c_start[:, None] + NAT_KW)
    idx = cols[None, :] - cols[:, None] + NAT_KW - 1
    sel = (idx[None] == np.arange(2 * NAT_KW - 1)[:, None, None]).astype(np.float32)
    tiles = jnp.einsum('dhab,bqk->dhaqk', rpb.astype(F32), sel, precision=HIGHEST)
    tiles = jnp.where(col_mask, tiles, -jnp.inf)
    neg = jnp.full(tiles.shape[:2] + (1, GRID_W, GRID_W), -jnp.inf, F32)
    ext = jnp.concatenate([neg, tiles, neg], axis=2)
    pairs = jnp.concatenate([ext[:, :, 0:NAT_NPAIR], ext[:, :, 1:NAT_NPAIR + 1]], axis=-1)
    return pairs.reshape(DEPTH, NAT_HEAD_PAIRS, 2, NAT_NPAIR, GRID_W, 2 * GRID_W)


def _nat_block_bias(bias_ref, hh, j, first, n, rows, lo):
    kh = min(NAT_KH, rows)
    neg = jnp.full((GRID_W, 2 * GRID_W), -jnp.inf, F32)
    row_blocks = []
    for ql in range(NAT_QROWS):
        qr = j * NAT_QROWS + ql
        rs = _nat_window_start(qr, rows)
        pieces = []
        for m in range(n // 2):
            k0 = first + 2 * m
            ok0 = rs <= k0 < rs + kh
            ok1 = rs <= k0 + 1 < rs + kh
            if not (ok0 or ok1):
                pieces.append(neg)
                continue
            t = bias_ref[hh, k0 - qr + NAT_KH]
            if ok0 and ok1:
                pieces.append(t)
            elif ok0:
                pieces.append(jnp.where(lo, t, -jnp.inf))
            else:
                pieces.append(jnp.where(lo, -jnp.inf, t))
        row_blocks.append(jnp.concatenate(pieces, axis=1))
    return jnp.concatenate(row_blocks, axis=0)


def _nat_attn_body(q_ref, k_ref, v_ref, kc_ref, vc_ref, bias_ref, o_ref, *, rows):
    lo = lax.broadcasted_iota(jnp.int32, (1, LANES), 1) < NAT_HEAD_DIM
    kc = kc_ref[...].astype(BF16)
    vc = vc_ref[...].astype(BF16)
    nq = NAT_QROWS * GRID_W
    for j, (first, n) in enumerate(_nat_blocks(rows)):
        q = q_ref[j * nq:(j + 1) * nq, :] * NAT_SCALE
        ks = slice(first * GRID_W, (first + n) * GRID_W)
        keys = jnp.concatenate([k_ref[ks, :].astype(BF16), kc], axis=0)
        v_ext = _values_with_ones(jnp.concatenate([v_ref[ks, :].astype(BF16), vc], axis=0))

        def bias(hh, j=j, first=first, n=n):
            return _nat_block_bias(bias_ref, hh, j, first, n, rows, lo)

        o_ref[j * nq:(j + 1) * nq, :] = _pair_attention(q, keys, v_ext, bias, lo)


def _nat_attn_call(q, k, v, cache_k, cache_v, bias, layer, bs, seq):
    rows = seq // GRID_W
    assert rows % NAT_QROWS == 0 and 2 * GRID_W == LANES
    spec = pl.BlockSpec((seq, LANES), lambda p, b: (b, p))
    cspec = pl.BlockSpec((None, None, cache_k.shape[2], LANES), lambda p, b: (b, layer, 0, p))
    return pl.pallas_call(
        functools.partial(_nat_attn_body, rows=rows),
        out_shape=jax.ShapeDtypeStruct(q.shape, F32),
        grid=(NAT_HEAD_PAIRS, bs),
        in_specs=[spec, spec, spec, cspec, cspec,
                  pl.BlockSpec((None, None) + bias.shape[2:], lambda p, b: (layer, p, 0, 0, 0, 0))],
        out_specs=spec,
        compiler_params=_cparams("parallel", "parallel"),
        name="nat_attn",
    )(q, k, v, cache_k, cache_v, bias)


def _ssd_body(*refs, seq, has_h0):
    zx_ref, dt_ref, cw_ref, cb_ref, dtb_ref, a_ref, d_ref, nw_ref = refs[:8]
    refs = refs[8:]
    if has_h0:
        h0_ref, y_ref = refs[:2]
        refs = refs[2:]
        hout_ref = None
    else:
        y_ref, hout_ref = refs[:2]
        refs = refs[2:]
        h0_ref = None
    xbc_s, y_s, eb_s, dtv_s, st_s, h_s, tr_s = refs

    q = SSD_CHUNK
    nc = seq // q
    nh = SSD_HEADS
    hd = SSD_HEAD_DIM
    hpg = nh // SSD_GROUPS
    b_off = SSD_D
    c_off = SSD_D + SSD_GROUPS * SSD_STATE
    half = SSD_CONV // 2

    zeros8 = jnp.zeros((8, LANES), F32)
    for cblk in range(SSD_CONV_DIM // LANES):
        cs = slice(cblk * LANES, (cblk + 1) * LANES)
        xcol = slice(SSD_D + cblk * LANES, SSD_D + (cblk + 1) * LANES)
        xin = zx_ref[:, xcol]
        top = jnp.concatenate([zeros8, zx_ref[0:16, xcol]], axis=0)
        bot = jnp.concatenate([zx_ref[seq - 16:seq, xcol], zeros8], axis=0)
        acc = cb_ref[:, cs] + cw_ref[half:half + 1, cs] * xin
        acc_t = cb_ref[:, cs] + cw_ref[half:half + 1, cs] * top[8:16]
        acc_b = cb_ref[:, cs] + cw_ref[half:half + 1, cs] * bot[8:16]
        for kk in range(SSD_CONV):
            d = kk - half
            if d == 0:
                continue
            w = cw_ref[kk:kk + 1, cs]
            acc = acc + w * pltpu.roll(xin, (-d) % seq, axis=0)
            acc_t = acc_t + w * pltpu.roll(top, (-d) % 24, axis=0)[8:16]
            acc_b = acc_b + w * pltpu.roll(bot, (-d) % 24, axis=0)[8:16]
        xbc_s[:, cs] = _silu(acc)
        xbc_s[0:8, cs] = _silu(acc_t)
        xbc_s[seq - 8:seq, cs] = _silu(acc_b)

    lane = lax.broadcasted_iota(jnp.int32, (1, LANES), 1)
    xdt = dt_ref[...] + dtb_ref[...]
    dtv = jnp.maximum(xdt, 0.0) + jnp.log1p(jnp.exp(-jnp.abs(xdt)))
    dtv_s[...] = jnp.where(lane < 2 * nh, dtv, 0.0)

    ii = lax.broadcasted_iota(jnp.int32, (q, q), 0)
    jj = lax.broadcasted_iota(jnp.int32, (q, q), 1)
    lower = jj <= ii
    upper = jj >= ii
    is_fwd = lane < nh
    lo64 = lane < hd
    tri = jnp.concatenate([lower, upper], axis=1).astype(BF16)
    wide = 2 * SSD_D
    ek = lax.broadcasted_iota(jnp.int32, (2 * LANES, wide), 0) & (LANES - 1)
    el = lax.broadcasted_iota(jnp.int32, (2 * LANES, wide), 1) >> (hd.bit_length() - 1)
    expand = (ek == el).astype(BF16)
    grow = lax.broadcasted_iota(jnp.int32, (LANES, SSD_D), 0) >= SSD_STATE
    glane = lax.broadcasted_iota(jnp.int32, (LANES, SSD_D), 1) >= hpg * hd
    own = grow == glane

    def split2(x):
        hi = x.astype(BF16)
        return hi, (x - hi.astype(F32)).astype(BF16)

    def lane_expand(x):
        hi, mid = split2(x)
        return _dot(jnp.concatenate([hi, mid], axis=1), expand)

    def stage_cumsum(c):
        r0 = pl.multiple_of(c * q, q)
        k = dict(c=c, rs=pl.ds(r0, q))
        k['dt'] = dtv_s[k['rs'], :]
        da_c = k['dt'] * a_ref[...]
        rhs = jnp.concatenate([jnp.where(is_fwd, da_c, 0.0), jnp.where(is_fwd, 0.0, da_c)], axis=0)
        p0, p1 = split2(rhs)
        p2 = (rhs - p0.astype(F32) - p1.astype(F32)).astype(BF16)
        k['ac'] = _dot(tri, p0) + _dot(tri, p1) + _dot(tri, p2)
        return k

    def stage_expand(k):
        ac, dt_c = k['ac'], k['dt']
        k['ac_t'] = ac.T
        k['dt_t'] = dt_c.T
        last = jnp.where(is_fwd, ac[q - 1:q, :], ac[0:1, :])
        eb_s[k['rs'], :] = lane_expand(jnp.exp(ac))
        k['wb'] = lane_expand(dt_c * jnp.exp(last - ac))

    def stage_states(k):
        rs = k['rs']
        k['xs'] = xbc_s[rs, 0:SSD_D]
        ball = xbc_s[rs, b_off:c_off].astype(BF16)
        call = xbc_s[rs, c_off:c_off + SSD_GROUPS * SSD_STATE]
        xw = (jnp.concatenate([k['xs'], k['xs']], axis=1) * k['wb']).astype(BF16)
        st_s[k['c']] = _dot_tn(ball, xw)
        k['cb'] = [_dot_nt(jnp.where(lo64 if g == 0 else jnp.logical_not(lo64), call, 0.0).astype(BF16),
                           ball) for g in range(SSD_GROUPS)]

    def stage_intra(k, pr):
        ac, ac_t, dt_t = k['ac'], k['ac_t'], k['dt_t']
        ws = []
        for h in (2 * pr, 2 * pr + 1):
            seg_f = ac[:, h:h + 1] - ac_t[h:h + 1, :]
            seg_b = ac[:, nh + h:nh + h + 1] - ac_t[nh + h:nh + h + 1, :]
            w = k['cb'][h // hpg] * (
                jnp.exp(jnp.where(lower, seg_f, -jnp.inf)) * dt_t[h:h + 1, :]
                + jnp.exp(jnp.where(upper, seg_b, -jnp.inf)) * dt_t[nh + h:nh + h + 1, :])
            ws.append(w.astype(BF16))
        xp = k['xs'][:, pr * LANES:(pr + 1) * LANES]
        xbd = jnp.concatenate([jnp.where(lo64, xp, 0.0), jnp.where(lo64, 0.0, xp)],
                              axis=0).astype(BF16)
        y_s[k['rs'], pr * LANES:(pr + 1) * LANES] = _dot(jnp.concatenate(ws, axis=1), xbd)

    ilv = min(SSD_ILV, nc)

    def chunks(cg, carry):
        ks = [stage_cumsum(cg * ilv + j) for j in range(ilv)]
        for k in ks:
            stage_expand(k)
        for k in ks:
            stage_states(k)
        for pr in range(nh // 2):
            for k in ks:
                stage_intra(k, pr)
        return carry

    lax.fori_loop(0, nc // ilv, chunks, 0)

    if has_h0:
        for direction in range(2):
            tr_s[...] = jnp.zeros(tr_s.shape, F32)
            for h in range(nh):
                g = h // hpg
                tr_s[h * hd:(h + 1) * hd, g * SSD_STATE:(g + 1) * SSD_STATE] = h0_ref[direction * nh + h]
            for k in range(SSD_D // LANES):
                h_s[direction, :, k * LANES:(k + 1) * LANES] = tr_s[k * LANES:(k + 1) * LANES, :].T
    else:
        h_s[...] = jnp.zeros(h_s.shape, F32)

    def carry_states(kstep, carry):
        for direction in range(2):
            c = kstep if direction == 0 else nc - 1 - kstep
            r0 = pl.multiple_of(c * q, q)
            rs = pl.ds(r0, q)
            ds_ = slice(direction * SSD_D, (direction + 1) * SSD_D)
            edge = r0 + q - 1 if direction == 0 else r0
            call = xbc_s[rs, c_off:c_off + SSD_GROUPS * SSD_STATE].astype(BF16)
            h_in = h_s[direction]
            y_s[rs, :] += _dot(call, jnp.where(own, h_in, 0.0).astype(BF16)) * eb_s[rs, ds_]
            h_s[direction] = eb_s[pl.ds(edge, 1), ds_] * h_in + st_s[c][:, ds_]
        return carry

    lax.fori_loop(0, nc, carry_states, 0)

    y = y_s[...] + d_ref[...] * xbc_s[:, 0:SSD_D]
    y = y * _silu(zx_ref[:, 0:SSD_D])
    y_ref[...] = _rms(y, nw_ref[...])
    if hout_ref is not None:
        for direction in range(2):
            for k in range(SSD_D // LANES):
                tr_s[k * LANES:(k + 1) * LANES, :] = h_s[direction, :, k * LANES:(k + 1) * LANES].T
            for h in range(nh):
                g = h // hpg
                hout_ref[direction * nh + h] = tr_s[h * hd:(h + 1) * hd,
                                                    g * SSD_STATE:(g + 1) * SSD_STATE]


def _ssd_call(zx, dt, lw, layer, state, seq):
    has_h0 = state is not None
    rows = zx.shape[0]
    nseq = rows // seq
    nc = seq // SSD_CHUNK
    nst = 2 * SSD_HEADS

    def seqspec(w):
        return pl.BlockSpec((seq, w), lambda b: (b, 0))

    def par(shape):
        return _const_spec(shape, layer, single_buffer=False)

    in_specs = [seqspec(1024), seqspec(LANES), par((8, SSD_CONV_DIM)), par((1, SSD_CONV_DIM)),
                par((1, LANES)), par((1, LANES)), par((1, SSD_D)), par((1, SSD_D))]
    args = [zx, dt, lw['conv_w'], lw['conv_b'], lw['dt_bias'], lw['a'], lw['d'], lw['norm']]
    y_shape = jax.ShapeDtypeStruct((rows, SSD_D), F32)
    if has_h0:
        in_specs.append(pl.BlockSpec((None, None, nst, SSD_HEAD_DIM, SSD_STATE),
                                     lambda b: (b, layer, 0, 0, 0)))
        args.append(state)
        out_shape = y_shape
        out_specs = seqspec(SSD_D)
    else:
        out_shape = [y_shape, jax.ShapeDtypeStruct((nseq, nst, SSD_HEAD_DIM, SSD_STATE), F32)]
        out_specs = [seqspec(SSD_D),
                     pl.BlockSpec((None, nst, SSD_HEAD_DIM, SSD_STATE), lambda b: (b, 0, 0, 0))]
    return pl.pallas_call(
        functools.partial(_ssd_body, seq=seq, has_h0=has_h0),
        out_shape=out_shape,
        grid=(nseq,),
        in_specs=in_specs,
        out_specs=out_specs,
        scratch_shapes=[
            pltpu.VMEM((seq, SSD_CONV_DIM), F32),
            pltpu.VMEM((seq, SSD_D), F32),
            pltpu.VMEM((seq, 2 * SSD_D), F32),
            pltpu.VMEM((seq, LANES), F32),
            pltpu.VMEM((nc, SSD_GROUPS * SSD_STATE, 2 * SSD_D), F32),
            pltpu.VMEM((2, SSD_GROUPS * SSD_STATE, SSD_D), F32),
            pltpu.VMEM((SSD_D, SSD_GROUPS * SSD_STATE), F32),
        ],
        compiler_params=_cparams("parallel"),
        name="ssd_mixer",
    )(*args)


def _s5_tables(a_re, a_im, log_dt, b_re, b_im, c_re, c_im):
    t = S5_T
    gh = S5_HGROUPS
    f32 = lambda x: x.astype(F32)
    lam_r, lam_i = f32(a_re), f32(a_im)
    step = jnp.exp(f32(log_dt))[..., None]
    xr, xi = lam_r * step, lam_i * step

    def powers(ks):
        ks = jnp.asarray(ks, F32)[None, None, :, None, None]
        mag = jnp.exp(xr[:, :, None] * ks)
        return mag * jnp.cos(xi[:, :, None] * ks), mag * jnp.sin(xi[:, :, None] * ks)

    pw_r, pw_i = powers(np.arange(t + 1))
    nr, ni = pw_r[:, :, 1] - 1.0, pw_i[:, :, 1]
    den = lam_r * lam_r + lam_i * lam_i
    fr = (nr * lam_r + ni * lam_i) / den
    fi = (ni * lam_r - nr * lam_i) / den
    br, bi = f32(b_re)[:, None], f32(b_im)[:, None]
    bb_r = fr[..., None] * br - fi[..., None] * bi
    bb_i = fr[..., None] * bi + fi[..., None] * br
    cr, ci = f32(c_re), f32(c_im)

    kern = []
    for direction in range(2):
        pr = jnp.swapaxes(pw_r[:, direction, :t], 1, 2)[:, :, :, None, :]
        pi = jnp.swapaxes(pw_i[:, direction, :t], 1, 2)[:, :, :, None, :]
        cpr = cr[:, :, None] * pr - ci[:, :, None] * pi
        cpi = cr[:, :, None] * pi + ci[:, :, None] * pr
        lhs = jnp.concatenate([cpr, -cpi], axis=-1).reshape(DEPTH, S5_GROUPS, t * S5_GROUP_CH, 2 * S5_STATE)
        rhs = jnp.concatenate([bb_r[:, direction], bb_i[:, direction]], axis=2)
        k = jnp.einsum('dgxn,dgnk->dgxk', lhs, rhs, precision=HIGHEST)
        kern.append(jnp.swapaxes(k.reshape(DEPTH, S5_GROUPS, t, S5_GROUP_CH, S5_GROUP_CH), 1, 2))
    kf, kb = kern
    lag = [kb[:, -d] if d < 0 else (kf[:, 0] + kb[:, 0] if d == 0 else kf[:, d])
           for d in range(-(t - 1), t)]
    kt = jnp.stack(lag, axis=1).reshape(DEPTH, S5_LAGS, 2, gh, S5_GROUP_CH, S5_GROUP_CH)
    kt = jnp.transpose(kt, (0, 2, 1, 5, 3, 4)).reshape(DEPTH, 2, S5_LAGS, S5_GROUP_CH, LANES)

    def by_half(x):
        return jnp.transpose(x.reshape(DEPTH, -1, 2, gh, S5_STATE), (0, 2, 1, 3, 4))

    def dir_powers(direction, ks):
        pr, pi = powers(np.asarray(ks))
        return pr[:, direction], pi[:, direction]

    bt_r = jnp.transpose(bb_r.reshape(DEPTH, 2, 2, gh, S5_STATE, S5_GROUP_CH), (0, 1, 2, 5, 3, 4))
    bt_i = jnp.transpose(bb_i.reshape(DEPTH, 2, 2, gh, S5_STATE, S5_GROUP_CH), (0, 1, 2, 5, 3, 4))
    sw = []
    for direction, ks in ((0, [t - 1 - s for s in range(t)]), (1, list(range(t)))):
        pr, pi = dir_powers(direction, ks)
        pr = by_half(pr)[:, :, :, None]
        pi = by_half(pi)[:, :, :, None]
        wr, wi = bt_r[:, direction][:, :, None], bt_i[:, direction][:, :, None]
        sw += [pr * wr - pi * wi, pr * wi + pi * wr]
    sw = jnp.stack(sw, axis=3).reshape(DEPTH, 2, t, 4, S5_GROUP_CH, gh * S5_STATE)
    ct_r = jnp.transpose(cr.reshape(DEPTH, 2, gh, S5_GROUP_CH, S5_STATE), (0, 1, 4, 2, 3))
    ct_i = jnp.transpose(ci.reshape(DEPTH, 2, gh, S5_GROUP_CH, S5_STATE), (0, 1, 4, 2, 3))
    aw = []
    for direction, ks in ((0, list(range(1, t + 1))), (1, [t - k for k in range(t)])):
        pr, pi = dir_powers(direction, ks)
        pr = jnp.swapaxes(by_half(pr), 3, 4)[..., None]
        pi = jnp.swapaxes(by_half(pi), 3, 4)[..., None]
        wr, wi = ct_r[:, :, None], ct_i[:, :, None]
        aw += [wr * pr - wi * pi, -(wr * pi + wi * pr)]
    aw = jnp.stack(aw, axis=2).reshape(DEPTH, 2, 4, t, S5_STATE, LANES)

    apow = jnp.stack([pw_r[:, 0, t], pw_i[:, 0, t], pw_r[:, 1, t], pw_i[:, 1, t]], axis=1)
    seg_r, seg_i = powers(t * np.arange(S5_NC + 1))
    back_r, back_i = dir_powers(1, t * (S5_NC - 1 - np.arange(S5_NC)))
    pseg = jnp.stack([seg_r[:, 0, :S5_NC], seg_i[:, 0, :S5_NC], back_r, back_i], axis=1)
    nkb = S5_CW // LANES
    pseg = jnp.transpose(pseg.reshape(DEPTH, 4, S5_NC, nkb, LANES), (0, 1, 3, 2, 4))
    pseg = pseg.reshape(DEPTH, 4 * nkb, S5_NC, LANES)
    aseg = jnp.stack([seg_r[:, 0, S5_NC], seg_i[:, 0, S5_NC],
                      seg_r[:, 1, S5_NC], seg_i[:, 1, S5_NC]], axis=1)
    return (kt, sw, aw, apow.reshape(DEPTH, 1, S5_SW), pseg, aseg.reshape(DEPTH, 1, S5_SW))


def _s5_expand_operators(kt_ref, sw_ref, aw_ref, toep_s, sop_s, aop_s):
    gh = S5_HGROUPS
    sh_ch = S5_GROUP_CH.bit_length() - 1
    sh_st = S5_STATE.bit_length() - 1

    def group_mask(shape, row_shift, lane_shift):
        r = lax.broadcasted_iota(jnp.int32, shape, 0) >> row_shift
        c = lax.broadcasted_iota(jnp.int32, shape, 1) >> lane_shift
        return r == c

    def blockdiag(x, mask):
        return jnp.where(mask, jnp.concatenate([x] * gh, axis=0), 0.0).astype(BF16)

    m_kk = group_mask((LANES, LANES), sh_ch, sh_ch)
    m_ks = group_mask((LANES, gh * S5_STATE), sh_ch, sh_st)
    m_sk = group_mask((gh * S5_STATE, LANES), sh_st, sh_ch)
    cw = gh * S5_STATE
    for hf in range(2):
        lags = [blockdiag(kt_ref[hf, d], m_kk) for d in range(S5_LAGS)]
        for s in range(S5_T):
            for t in range(S5_T):
                toep_s[hf, s * LANES:(s + 1) * LANES, t * LANES:(t + 1) * LANES] = lags[t - s + S5_T - 1]
            for comp in range(4):
                sop_s[hf, s * LANES:(s + 1) * LANES, comp * cw:(comp + 1) * cw] = blockdiag(
                    sw_ref[hf, s, comp], m_ks)
        for comp in range(4):
            for t in range(S5_T):
                aop_s[hf, comp * cw:(comp + 1) * cw, t * LANES:(t + 1) * LANES] = blockdiag(
                    aw_ref[hf, comp, t], m_sk)


def _s5_body(up_ref, us_ref, kt_ref, sw_ref, aw_ref, at_ref, pseg_ref, aseg_ref, h0_ref,
             y_ref, hout_ref, toep_s, sop_s, aop_s, st_s, hin_s, yacc_s, xh_s, *, nblk_p, nseg_s):
    i = pl.program_id(0)

    @pl.when(i == 0)
    def _():
        _s5_expand_operators(kt_ref, sw_ref, aw_ref, toep_s, sop_s, aop_s)

    cw = S5_CW
    nkb = cw // LANES

    def seg_rows(c, s):
        return pl.ds(c * S5_T + s, S5_HB, stride=S5_T * S5_NC)

    def chunk_rows(c):
        return slice(c * S5_HB, (c + 1) * S5_HB)

    def gather(src_ref):
        for hf in range(2):
            for c in range(0, S5_NC, 2):
                xh_s[hf, c * S5_HB:(c + 2) * S5_HB, :] = jnp.concatenate(
                    [jnp.concatenate([src_ref[hf, seg_rows(cc, s), :] for s in range(S5_T)], axis=1)
                     for cc in (c, c + 1)], axis=0).astype(BF16)

    @pl.when(i < nblk_p)
    def _():
        gather(up_ref)

    @pl.when(i >= nblk_p)
    def _():
        gather(us_ref)

    for hf in range(2):
        xh = xh_s[hf]
        yacc_s[hf] = _dot(xh, toep_s[hf])
        sh = _dot(xh, sop_s[hf])
        for comp in range(4):
            for k in range(nkb // 2):
                col = (comp * (nkb // 2) + k) * LANES
                st_s[comp * nkb + hf * (nkb // 2) + k] = sh[:, col:col + LANES]

    def load(ref, comp, rows):
        return jnp.concatenate([ref[comp * nkb + k, rows, :] for k in range(nkb)], axis=1)

    def store(ref, comp, rows, val):
        for k in range(nkb):
            ref[comp * nkb + k, rows, :] = val[:, k * LANES:(k + 1) * LANES]

    def cmul(ar, ai, hr, hi):
        return ar * hr - ai * hi, ar * hi + ai * hr

    at = at_ref[...]
    ar_f, ai_f, ar_b, ai_b = [at[:, k * cw:(k + 1) * cw] for k in range(4)]

    def step(c, carry):
        hfr, hfi, hbr, hbi = carry
        rf = chunk_rows(c)
        rb = chunk_rows(S5_NC - 1 - c)
        for comp, val, rows in ((0, hfr, rf), (1, hfi, rf), (2, hbr, rb), (3, hbi, rb)):
            store(hin_s, comp, rows, val)
        fr, fi = cmul(ar_f, ai_f, hfr, hfi)
        br, bi = cmul(ar_b, ai_b, hbr, hbi)
        return (fr + load(st_s, 0, rf), fi + load(st_s, 1, rf),
                br + load(st_s, 2, rb), bi + load(st_s, 3, rb))

    zero = jnp.zeros((S5_HB, cw), F32)
    fin = (zero, zero, zero, zero)
    for c in range(S5_NC):
        fin = step(c, fin)
    hout_ref[...] = jnp.concatenate(fin, axis=-1)

    @pl.when(i >= nblk_p)
    def _():
        h0 = h0_ref[...]
        aseg = aseg_ref[...]
        sr_f, si_f, sr_b, si_b = [aseg[:, k * cw:(k + 1) * cw] for k in range(4)]
        ent = [[None] * S5_HB for _ in range(4)]
        for s in range(S5_HB // nseg_s):
            hr, hi = h0[s:s + 1, 0:cw], h0[s:s + 1, cw:2 * cw]
            for j in range(nseg_s):
                v = s * nseg_s + j
                ent[0][v], ent[1][v] = hr, hi
                hr, hi = cmul(sr_f, si_f, hr, hi)
                hr, hi = hr + fin[0][v:v + 1], hi + fin[1][v:v + 1]
            hr, hi = h0[s:s + 1, 2 * cw:3 * cw], h0[s:s + 1, 3 * cw:4 * cw]
            for j in range(nseg_s - 1, -1, -1):
                v = s * nseg_s + j
                ent[2][v], ent[3][v] = hr, hi
                hr, hi = cmul(sr_b, si_b, hr, hi)
                hr, hi = hr + fin[2][v:v + 1], hi + fin[3][v:v + 1]
        ent = [jnp.concatenate(rows, axis=0) for rows in ent]
        for direction in range(2):
            for k in range(nkb):
                kr = (2 * direction) * nkb + k
                ki = (2 * direction + 1) * nkb + k
                er = ent[2 * direction][:, k * LANES:(k + 1) * LANES]
                ei = ent[2 * direction + 1][:, k * LANES:(k + 1) * LANES]
                for c in range(S5_NC):
                    dr, di = cmul(pseg_ref[kr, c:c + 1, :], pseg_ref[ki, c:c + 1, :], er, ei)
                    hin_s[kr, chunk_rows(c), :] += dr
                    hin_s[ki, chunk_rows(c), :] += di

    for hf in range(2):
        hh = jnp.concatenate(
            [hin_s[comp * nkb + hf * (nkb // 2) + k] for comp in range(4) for k in range(nkb // 2)],
            axis=1).astype(BF16)
        y = yacc_s[hf] + _dot(hh, aop_s[hf])
        for c in range(S5_NC):
            for t in range(S5_T):
                y_ref[hf, seg_rows(c, t), :] = y[c * S5_HB:(c + 1) * S5_HB, t * LANES:(t + 1) * LANES]


def _s5_call(u_p, u_s, tables, layer, h0, seq_p, seq_s):
    kt, sw, aw, apow, pseg, aseg = tables
    seg_tok = S5_T * S5_NC
    assert seq_p == seg_tok and seq_s % seg_tok == 0 and S5_HB % (seq_s // seg_tok) == 0
    blk_tok = S5_ROWS * S5_T
    nblk_p = u_p.shape[1] // blk_tok
    nblk = nblk_p + u_s.shape[1] // blk_tok
    uh = S5_D // LANES

    def par(a):
        return _const_spec(a.shape[1:], layer)

    nlb = S5_SW // LANES
    return pl.pallas_call(
        functools.partial(_s5_body, nblk_p=nblk_p, nseg_s=seq_s // seg_tok),
        out_shape=[jax.ShapeDtypeStruct((uh, nblk * blk_tok, LANES), F32),
                   jax.ShapeDtypeStruct((nblk, S5_HB, S5_SW), F32)],
        grid=(nblk,),
        in_specs=[pl.BlockSpec((uh, blk_tok, LANES), lambda i: (0, jnp.minimum(i, nblk_p - 1), 0)),
                  pl.BlockSpec((uh, blk_tok, LANES), lambda i: (0, jnp.maximum(i - nblk_p, 0), 0)),
                  par(kt), par(sw), par(aw), par(apow), par(pseg), par(aseg),
                  pl.BlockSpec((None, None, S5_HB, S5_SW), lambda i: (layer, i, 0, 0))],
        out_specs=[pl.BlockSpec((uh, blk_tok, LANES), lambda i: (0, i, 0)),
                   pl.BlockSpec((None, S5_HB, S5_SW), lambda i: (i, 0, 0))],
        scratch_shapes=[pltpu.VMEM((2, S5_KW, S5_KW), BF16), pltpu.VMEM((2, S5_KW, S5_SH), BF16),
                        pltpu.VMEM((2, S5_SH, S5_KW), BF16),
                        pltpu.VMEM((nlb, S5_ROWS, LANES), F32), pltpu.VMEM((nlb, S5_ROWS, LANES), F32),
                        pltpu.VMEM((2, S5_ROWS, S5_KW), F32), pltpu.VMEM((2, S5_ROWS, S5_KW), BF16)],
        compiler_params=_cparams("arbitrary"),
        name="s5_mixer",
    )(u_p, u_s, kt, sw, aw, apow, pseg, aseg, h0)


def _gelu_tanh(x):
    return 0.5 * x * (1.0 + jnp.tanh(math.sqrt(2.0 / math.pi) * (x + 0.044715 * (x * x * x))))


def _out_body(*refs, final):
    (x_ref, yssd_ref, ynat_ref, u_ref, y5_ref, g1_ref, m2_ref, d5_ref,
     wglu_ref, bglu_ref, wout_ref, nm_ref, w1_ref, w2_ref) = refs[:14]
    if final:
        nf_ref, o_ref = refs[14:]
    else:
        (o_ref,) = refs[14:]
    uh = S5_D // LANES
    for sb in range(ROW_TILE // TOK_TILE):
        rows = slice(sb * TOK_TILE, (sb + 1) * TOK_TILE)
        y5_in = jnp.concatenate([y5_ref[hf, rows, :] for hf in range(uh)], axis=1)
        u = jnp.concatenate([u_ref[hf, rows, :] for hf in range(uh)], axis=1)
        g = _gelu_tanh(y5_in + d5_ref[...] * u)
        y5 = g * jax.nn.sigmoid(_dot(g.astype(BF16), wglu_ref[...]) + bglu_ref[...])
        mix = (_dot(yssd_ref[rows, :].astype(BF16), wout_ref[0:SSD_D, :])
               + _dot(y5.astype(BF16), wout_ref[SSD_D:SSD_D + S5_D, :])
               + _dot(ynat_ref[rows, :].astype(BF16), wout_ref[SSD_D + S5_D:, :]))
        x = x_ref[rows, :] + g1_ref[...] * mix
        h2 = _rms(x, nm_ref[...]) * (1.0 + m2_ref[:, D_MODEL:2 * D_MODEL]) + m2_ref[:, 0:D_MODEL]
        f = jnp.maximum(_dot(h2.astype(BF16), w1_ref[...]), 0.0)
        f = (f * f).astype(BF16)
        x = x + m2_ref[:, 2 * D_MODEL:3 * D_MODEL] * _dot(f, w2_ref[...])
        if final:
            x = _rms(x, nf_ref[...])
        o_ref[rows, :] = x


def _out_call(x, y_ssd, y_nat, u, y5, y5_tile_off, cond_idx, mods4, lw, norm_f, layer, final):
    rows = x.shape[0]
    uh = S5_D // LANES

    def tile(w):
        return pl.BlockSpec((ROW_TILE, w), lambda i: (i, 0))

    def par(shape):
        return _const_spec(shape, layer)

    in_specs = [
        tile(D_MODEL), tile(SSD_D), tile(NAT_D),
        pl.BlockSpec((uh, ROW_TILE, LANES), lambda i: (0, i, 0)),
        pl.BlockSpec((uh, ROW_TILE, LANES), lambda i: (0, i + y5_tile_off, 0)),
        pl.BlockSpec((None, None, 1, D_MODEL), lambda i: (layer, cond_idx(i), 0, 2)),
        pl.BlockSpec((None, None, 1, 3 * D_MODEL), lambda i: (layer, cond_idx(i), 0, 1)),
        par((1, S5_D)), par((S5_D, S5_D)), par((1, S5_D)),
        par((D_MODEL, D_MODEL)), par((1, D_MODEL)),
        par((D_MODEL, D_FF)), par((D_FF, D_MODEL))]
    args = [x, y_ssd, y_nat, u, y5, mods4, mods4, lw['s5_d'], lw['w_glu'], lw['b_glu'],
            lw['w_out'], lw['norm_mlp'], lw['w_ff1'], lw['w_ff2']]
    if final:
        in_specs.append(pl.BlockSpec((1, D_MODEL), lambda i: (0, 0)))
        args.append(norm_f.reshape(1, D_MODEL))
    return pl.pallas_call(
        functools.partial(_out_body, final=final),
        out_shape=jax.ShapeDtypeStruct((rows, D_MODEL), F32),
        grid=(rows // ROW_TILE,),
        in_specs=in_specs,
        out_specs=tile(D_MODEL),
        compiler_params=_cparams("parallel"),
        name="out_mlp",
    )(*args)


def _lane_pad(x):
    return jnp.pad(x, [(0, 0)] * (x.ndim - 1) + [(0, LANES - x.shape[-1])])


def kernel(x_prompt, x_sample, cache_nat_k, cache_nat_v, state_ssd, state_s5_re, state_s5_im,
           c, c_ctx, w_mod, b_mod, norm_mix, norm_mlp, w_in, ssd_conv_w, ssd_conv_b,
           ssd_dt_bias, ssd_a_log, ssd_d, ssd_norm, s5_a_re, s5_a_im, s5_log_dt,
           s5_b_re, s5_b_im, s5_c_re, s5_c_im, s5_d, s5_w_glu, s5_b_glu, nat_rpb,
           w_out, w_ff1, w_ff2, norm_f):
    bp, seq_p, _ = x_prompt.shape
    bs, seq_s, _ = x_sample.shape
    assert seq_p == TOK_TILE and seq_s == 4 * TOK_TILE
    rows_p = bp * seq_p
    rows_s = bs * seq_s

    ncp = -(-(1 + bs) // 8) * 8
    cond = jnp.concatenate([c_ctx[None, :], c, jnp.zeros((ncp - 1 - bs, D_MODEL), F32)], axis=0)
    mods = _mods_call(cond, w_mod, b_mod)
    mods4 = mods.reshape(DEPTH, ncp, 1, N_MOD * D_MODEL)

    assert w_in.shape[-1] == IN_COLS
    w_in_p = w_in.astype(BF16)
    ssd_w = {
        'conv_w': jnp.pad(ssd_conv_w.astype(F32), [(0, 0), (0, 8 - SSD_CONV), (0, 0)]),
        'conv_b': ssd_conv_b.astype(F32).reshape(DEPTH, 1, SSD_CONV_DIM),
        'dt_bias': _lane_pad(ssd_dt_bias.astype(F32).reshape(DEPTH, 1, 2 * SSD_HEADS)),
        'a': _lane_pad(-jnp.exp(ssd_a_log.astype(F32)).reshape(DEPTH, 1, 2 * SSD_HEADS)),
        'd': jnp.repeat(ssd_d.astype(F32), SSD_HEAD_DIM, axis=-1).reshape(DEPTH, 1, SSD_D),
        'norm': ssd_norm.astype(F32).reshape(DEPTH, 1, SSD_D),
    }
    out_w = {
        's5_d': s5_d.astype(F32).reshape(DEPTH, 1, S5_D),
        'w_glu': s5_w_glu.astype(BF16),
        'b_glu': s5_b_glu.astype(F32).reshape(DEPTH, 1, S5_D),
        'w_out': w_out.astype(BF16),
        'norm_mlp': norm_mlp.astype(F32).reshape(DEPTH, 1, D_MODEL),
        'w_ff1': w_ff1.astype(BF16),
        'w_ff2': w_ff2.astype(BF16),
    }
    s5_tabs = _s5_tables(s5_a_re, s5_a_im, s5_log_dt, s5_b_re, s5_b_im, s5_c_re, s5_c_im)
    nat_bias = _nat_pair_tiles(nat_rpb)

    cache_k = cache_nat_k.reshape(bs, DEPTH, -1, NAT_D)
    cache_v = cache_nat_v.reshape(bs, DEPTH, -1, NAT_D)
    st_ssd = state_ssd.reshape(bs, DEPTH, 2 * SSD_HEADS, SSD_HEAD_DIM, SSD_STATE)
    s5_tok = S5_ROWS * S5_T
    assert rows_p % s5_tok == 0 and rows_s % s5_tok == 0 and s5_tok % seq_s == 0
    nseq_s = s5_tok // seq_s
    st5 = jnp.stack([state_s5_re[:, :, 0], state_s5_im[:, :, 0],
                     state_s5_re[:, :, 1], state_s5_im[:, :, 1]], axis=2)
    st5 = jnp.transpose(st5.astype(F32).reshape(bs // nseq_s, nseq_s, DEPTH, S5_SW), (2, 0, 1, 3))
    st5 = jnp.pad(st5, [(0, 0), (rows_p // s5_tok, 0), (0, S5_HB - nseq_s), (0, 0)])

    assert rows_p % ROW_TILE == 0 and seq_s % ROW_TILE == 0
    tiles_per_seq = seq_s // ROW_TILE
    cond_p = lambda i: 0
    cond_s = lambda i: 1 + i // tiles_per_seq
    x_p = x_prompt.reshape(rows_p, D_MODEL)
    x_s = x_sample.reshape(rows_s, D_MODEL)
    new_k, new_v, new_ssd, new_s5 = [], [], [], []
    for l in range(DEPTH):
        final = l == DEPTH - 1
        zx_p, dt_p, u_p, q_p, k_p, v_p = _inproj_call(x_p, cond_p, mods4, norm_mix, w_in_p, l)
        zx_s, dt_s, u_s, q_s, k_s, v_s = _inproj_call(x_s, cond_s, mods4, norm_mix, w_in_p, l)
        new_k.append(k_p.reshape(bp, seq_p, NAT_D))
        new_v.append(v_p.reshape(bp, seq_p, NAT_D))

        y_nat_p = _ctx_attn_call(q_p, k_p, v_p)
        y_nat_s = _nat_attn_call(q_s, k_s, v_s, cache_k, cache_v, nat_bias, l, bs, seq_s)

        y_ssd_p, ssd_l = _ssd_call(zx_p, dt_p, ssd_w, l, None, seq_p)
        y_ssd_s = _ssd_call(zx_s, dt_s, ssd_w, l, st_ssd, seq_s)
        new_ssd.append(ssd_l)

        y5, s5_l = _s5_call(u_p, u_s, s5_tabs, l, st5, seq_p, seq_s)
        new_s5.append(s5_l[:rows_p // s5_tok])

        x_p = _out_call(x_p, y_ssd_p, y_nat_p, u_p, y5, 0, cond_p, mods4, out_w, norm_f, l, final)
        x_s = _out_call(x_s, y_ssd_s, y_nat_s, u_s, y5, rows_p // ROW_TILE, cond_s,
                        mods4, out_w, norm_f, l, final)

    y_prompt = x_p.reshape(bp, seq_p, D_MODEL)
    y_sample = x_s.reshape(bs, seq_s, D_MODEL)
    out_k = jnp.stack(new_k, axis=1).reshape(bp, DEPTH, seq_p, NAT_HEADS, NAT_HEAD_DIM)
    out_v = jnp.stack(new_v, axis=1).reshape(bp, DEPTH, seq_p, NAT_HEADS, NAT_HEAD_DIM)
    out_ssd = jnp.stack(new_ssd, axis=1).reshape(bp, DEPTH, 2, SSD_HEADS, SSD_HEAD_DIM, SSD_STATE)
    s5 = jnp.stack(new_s5, axis=0)[:, :, :s5_tok // seq_p]
    s5 = s5.reshape(DEPTH, bp, 4, S5_GROUPS, S5_STATE)
    s5 = jnp.transpose(s5, (1, 0, 2, 3, 4))
    out_re = s5[:, :, 0::2]
    out_im = s5[:, :, 1::2]
    return y_prompt, y_sample, out_k, out_v, out_ssd, out_re, out_im
```

```python
import functools
import math

import numpy as np
import jax
import jax.numpy as jnp
from jax import lax
from jax.experimental import pallas as pl
from jax.experimental.pallas import tpu as pltpu

F32 = jnp.float32
BF16 = jnp.bfloat16
HIGHEST = lax.Precision.HIGHEST

D_MODEL = 1024
DEPTH = 4
GRID_W = 64
SSD_HEADS = 6
SSD_HEAD_DIM = 64
SSD_D = SSD_HEADS * SSD_HEAD_DIM
SSD_GROUPS = 2
SSD_STATE = 64
SSD_CONV = 5
SSD_CHUNK = 128
SSD_CONV_DIM = SSD_D + 2 * SSD_GROUPS * SSD_STATE
SSD_ILV = 4
S5_GROUPS = 16
S5_GROUP_CH = 16
S5_D = S5_GROUPS * S5_GROUP_CH
S5_STATE = 64
NAT_HEADS = 6
NAT_HEAD_DIM = 64
NAT_D = NAT_HEADS * NAT_HEAD_DIM
NAT_KH = 8
NAT_KW = 16
D_FF = 4 * D_MODEL
N_MOD = 6
EPS = 1e-6

LANES = 128
TOK_TILE = 256
ROW_TILE = 512
IN_COLS = SSD_D + SSD_CONV_DIM + 2 * SSD_HEADS + S5_D + 3 * NAT_D
IN_COLS_PAD = -(-IN_COLS // LANES) * LANES
S5_T = 8
S5_HGROUPS = LANES // S5_GROUP_CH
S5_KW = S5_T * LANES
S5_CW = S5_GROUPS * S5_STATE
S5_SW = 4 * S5_CW
S5_SH = S5_SW // 2
S5_NC = 32
S5_HB = 8
S5_ROWS = S5_HB * S5_NC
S5_LAGS = 2 * S5_T - 1
VMEM_LIMIT = 56 * 1024 * 1024


def _cparams(*sem):
    return pltpu.CompilerParams(dimension_semantics=sem, vmem_limit_bytes=VMEM_LIMIT)


def _dot(a, b, precision=None):
    return jnp.dot(a, b, preferred_element_type=F32, precision=precision)


def _dot_nt(a, b, precision=None):
    return lax.dot_general(a, b, (((1,), (1,)), ((), ())), preferred_element_type=F32,
                           precision=precision)


def _dot_tn(a, b, precision=None):
    return lax.dot_general(a, b, (((0,), (0,)), ((), ())), preferred_element_type=F32,
                           precision=precision)


def _silu(x):
    hx = 0.5 * x
    return hx + hx * jnp.tanh(hx)


def _rms(x, g):
    return x * lax.rsqrt(jnp.mean(x * x, axis=-1, keepdims=True) + EPS) * g


def _const_spec(shape, layer, single_buffer=True):
    return pl.BlockSpec((None,) + tuple(shape), lambda *_: (layer,) + (0,) * len(shape),
                        pipeline_mode=pl.Buffered(1) if single_buffer else None)


def _mods_body(cond_ref, w_ref, b_ref, o_ref):
    s = _silu(cond_ref[...])
    o_ref[...] = _dot(s.astype(BF16), w_ref[...].astype(BF16)) + b_ref[...]


def _mods_call(cond, w_mod, b_mod):
    ncp = cond.shape[0]
    blk = 2 * D_MODEL
    return pl.pallas_call(
        _mods_body,
        out_shape=jax.ShapeDtypeStruct((DEPTH, ncp, N_MOD * D_MODEL), F32),
        grid=(DEPTH, N_MOD * D_MODEL // blk),
        in_specs=[
            pl.BlockSpec((ncp, D_MODEL), lambda l, j: (0, 0)),
            pl.BlockSpec((None, D_MODEL, blk), lambda l, j: (l, 0, j)),
            pl.BlockSpec((None, 1, blk), lambda l, j: (l, 0, j)),
        ],
        out_specs=pl.BlockSpec((None, ncp, blk), lambda l, j: (l, 0, j)),
        compiler_params=_cparams("arbitrary", "arbitrary"),
        name="adaln_mods",
    )(cond, w_mod, b_mod.reshape(DEPTH, 1, N_MOD * D_MODEL))


def _inproj_body(*refs, kv_cache):
    if kv_cache:
        x_ref, mod_ref, g_ref, w_ref, _, _, zx_ref, dt_ref, u_ref, q_ref, k_ref, v_ref = refs
    else:
        x_ref, mod_ref, g_ref, w_ref, zx_ref, dt_ref, u_ref, q_ref, k_ref, v_ref = refs
    o_dt = SSD_D + SSD_CONV_DIM
    for sb in range(ROW_TILE // TOK_TILE):
        rows = slice(sb * TOK_TILE, (sb + 1) * TOK_TILE)
        x = x_ref[rows, :]
        h = _rms(x, g_ref[...]) * (1.0 + mod_ref[:, D_MODEL:2 * D_MODEL]) + mod_ref[:, 0:D_MODEL]
        p = _dot(h.astype(BF16), w_ref[...])
        zx_ref[rows, :] = p[:, 0:o_dt]
        dt_ref[rows, :] = p[:, o_dt:o_dt + LANES]
        rest = p[:, o_dt + 2 * SSD_HEADS:]
        for hf in range(S5_D // LANES):
            u_ref[hf, rows, :] = rest[:, hf * LANES:(hf + 1) * LANES]
        q_ref[rows, :] = rest[:, S5_D:S5_D + NAT_D]
        kv_rows = sb if kv_cache else rows
        k_ref[kv_rows] = rest[:, S5_D + NAT_D:S5_D + 2 * NAT_D]
        v_ref[kv_rows] = rest[:, S5_D + 2 * NAT_D:S5_D + 3 * NAT_D]


def _inproj_call(x, cond_idx, mods4, norm_mix, w_in_p, layer, kv_cache=None):
    rows = x.shape[0]

    def tile(w):
        return pl.BlockSpec((ROW_TILE, w), lambda i: (i, 0))

    def out(w):
        return jax.ShapeDtypeStruct((rows, w), F32)

    uh = S5_D // LANES
    in_specs = [
        tile(D_MODEL),
        pl.BlockSpec((None, None, 1, 2 * D_MODEL), lambda i: (layer, cond_idx(i), 0, 0)),
        _const_spec((1, D_MODEL), layer, single_buffer=False),
        _const_spec((D_MODEL, IN_COLS_PAD), layer),
    ]
    args = [x, mods4, norm_mix.reshape(DEPTH, 1, D_MODEL), w_in_p]
    if kv_cache is None:
        kv_shapes = [out(NAT_D), out(NAT_D)]
        kv_specs = [tile(NAT_D), tile(NAT_D)]
        aliases = {}
    else:
        kv_shapes = [jax.ShapeDtypeStruct(b.shape, F32) for b in kv_cache]
        kv_specs = [pl.BlockSpec((ROW_TILE // TOK_TILE, None, TOK_TILE, NAT_D),
                                 lambda i: (i, layer, 0, 0))] * 2
        in_specs += [pl.BlockSpec(memory_space=pl.ANY)] * 2
        args += list(kv_cache)
        aliases = {4: 4, 5: 5}
    return pl.pallas_call(
        functools.partial(_inproj_body, kv_cache=kv_cache is not None),
        out_shape=[out(1024), out(LANES), jax.ShapeDtypeStruct((uh, rows, LANES), F32),
                   out(NAT_D)] + kv_shapes,
        grid=(rows // ROW_TILE,),
        in_specs=in_specs,
        out_specs=[tile(1024), tile(LANES), pl.BlockSpec((uh, ROW_TILE, LANES), lambda i: (0, i, 0)),
                   tile(NAT_D)] + kv_specs,
        input_output_aliases=aliases,
        compiler_params=_cparams("parallel"),
        name="in_proj",
    )(*args)


NAT_SCALE = NAT_HEAD_DIM ** -0.5


def _values_with_ones(v):
    return jnp.concatenate([v, jnp.ones(v.shape, BF16)], axis=1)


def _pair_attention(q, keys, v_ext, bias, lo):
    outs = []
    for hh in range(2):
        qm = jnp.where(lo if hh == 0 else jnp.logical_not(lo), q, 0.0).astype(BF16)
        s = _dot_nt(qm, keys)
        if bias is not None:
            b = bias(hh)
            nb = b.shape[1]
            s = jnp.concatenate([s[:, 0:nb] + b, s[:, nb:]], axis=1)
        p = jnp.exp(s - jnp.max(s, axis=-1, keepdims=True)).astype(BF16)
        o = _dot(p, v_ext)
        outs.append(o[:, 0:LANES] / o[:, LANES:2 * LANES])
    return jnp.where(lo, outs[0], outs[1])


def _ctx_attn_body(q_ref, k_ref, v_ref, o_ref):
    lo = lax.broadcasted_iota(jnp.int32, (1, LANES), 1) < NAT_HEAD_DIM
    for sq in range(ROW_TILE // TOK_TILE):
        rs = slice(sq * TOK_TILE, (sq + 1) * TOK_TILE)
        for p in range(NAT_D // LANES):
            cs = slice(p * LANES, (p + 1) * LANES)
            o_ref[rs, cs] = _pair_attention(q_ref[rs, cs] * NAT_SCALE, k_ref[sq, :, cs].astype(BF16),
                                            _values_with_ones(v_ref[sq, :, cs].astype(BF16)), None, lo)


def _ctx_attn_call(q, k, v, layer):
    rows = q.shape[0]
    spec = pl.BlockSpec((ROW_TILE, NAT_D), lambda b: (b, 0))
    kvspec = pl.BlockSpec((ROW_TILE // TOK_TILE, None, TOK_TILE, NAT_D), lambda b: (b, layer, 0, 0))
    return pl.pallas_call(
        _ctx_attn_body,
        out_shape=jax.ShapeDtypeStruct((rows, NAT_D), F32),
        grid=(rows // ROW_TILE,),
        in_specs=[spec, kvspec, kvspec],
        out_specs=spec,
        compiler_params=_cparams("parallel"),
        name="ctx_attn",
    )(q, k, v)


NAT_QROWS = 4
NAT_HEAD_PAIRS = NAT_HEADS // 2
NAT_NPAIR = 2 * NAT_KH


def _nat_window_start(r, rows):
    kh = min(NAT_KH, rows)
    return int(np.clip(r - kh // 2, 0, rows - kh))


def _nat_blocks(rows):
    kh = min(NAT_KH, rows)
    out = []
    for j in range(rows // NAT_QROWS):
        rs = [_nat_window_start(r, rows) for r in range(j * NAT_QROWS, (j + 1) * NAT_QROWS)]
        first = min(rs)
        n = -(-(max(rs) + kh - first) // NAT_QROWS) * NAT_QROWS
        first = min(first, rows - n)
        out.append((first, n))
    return out


def _nat_pair_tiles(rpb):
    cols = np.arange(GRID_W)
    c_start = np.clip(cols - NAT_KW // 2, 0, GRID_W - NAT_KW)
    col_mask = (cols[None, :] >= c_start[:, None]) & (cols[None, :] < c_start[:, None] + NAT_KW)
    idx = cols[None, :] - cols[:, None] + NAT_KW - 1
    sel = (idx[None] == np.arange(2 * NAT_KW - 1)[:, None, None]).astype(np.float32)
    tiles = jnp.einsum('dhab,bqk->dhaqk', rpb.astype(F32), sel, precision=HIGHEST)
    tiles = jnp.where(col_mask, tiles, -jnp.inf)
    neg = jnp.full(tiles.shape[:2] + (1, GRID_W, GRID_W), -jnp.inf, F32)
    ext = jnp.concatenate([neg, tiles, neg], axis=2)
    pairs = jnp.concatenate([ext[:, :, 0:NAT_NPAIR], ext[:, :, 1:NAT_NPAIR + 1]], axis=-1)
    return pairs.reshape(DEPTH, NAT_HEAD_PAIRS, 2, NAT_NPAIR, GRID_W, 2 * GRID_W)


def _nat_block_bias(bias_ref, hh, j, first, n, rows, lo):
    kh = min(NAT_KH, rows)
    neg = jnp.full((GRID_W, 2 * GRID_W), -jnp.inf, F32)
    row_blocks = []
    for ql in range(NAT_QROWS):
        qr = j * NAT_QROWS + ql
        rs = _nat_window_start(qr, rows)
        pieces = []
        for m in range(n // 2):
            k0 = first + 2 * m
            ok0 = rs <= k0 < rs + kh
            ok1 = rs <= k0 + 1 < rs + kh
            if not (ok0 or ok1):
                pieces.append(neg)
                continue
            t = bias_ref[hh, k0 - qr + NAT_KH]
            if ok0 and ok1:
                pieces.append(t)
            elif ok0:
                pieces.append(jnp.where(lo, t, -jnp.inf))
            else:
                pieces.append(jnp.where(lo, -jnp.inf, t))
        row_blocks.append(jnp.concatenate(pieces, axis=1))
    return jnp.concatenate(row_blocks, axis=0)


def _nat_attn_body(q_ref, k_ref, v_ref, kc_ref, vc_ref, bias_ref, o_ref, *, rows):
    lo = lax.broadcasted_iota(jnp.int32, (1, LANES), 1) < NAT_HEAD_DIM
    kc = kc_ref[...].astype(BF16)
    vc = vc_ref[...].astype(BF16)
    nq = NAT_QROWS * GRID_W
    for j, (first, n) in enumerate(_nat_blocks(rows)):
        q = q_ref[j * nq:(j + 1) * nq, :] * NAT_SCALE
        ks = slice(first * GRID_W, (first + n) * GRID_W)
        keys = jnp.concatenate([k_ref[ks, :].astype(BF16), kc], axis=0)
        v_ext = _values_with_ones(jnp.concatenate([v_ref[ks, :].astype(BF16), vc], axis=0))

        def bias(hh, j=j, first=first, n=n):
            return _nat_block_bias(bias_ref, hh, j, first, n, rows, lo)

        o_ref[j * nq:(j + 1) * nq, :] = _pair_attention(q, keys, v_ext, bias, lo)


def _nat_attn_call(q, k, v, cache_k, cache_v, bias, layer, bs, seq):
    rows = seq // GRID_W
    assert rows % NAT_QROWS == 0 and 2 * GRID_W == LANES
    spec = pl.BlockSpec((seq, LANES), lambda p, b: (b, p))
    cspec = pl.BlockSpec((None, None, cache_k.shape[2], LANES), lambda p, b: (b, layer, 0, p))
    return pl.pallas_call(
        functools.partial(_nat_attn_body, rows=rows),
        out_shape=jax.ShapeDtypeStruct(q.shape, F32),
        grid=(NAT_HEAD_PAIRS, bs),
        in_specs=[spec, spec, spec, cspec, cspec,
                  pl.BlockSpec((None, None) + bias.shape[2:], lambda p, b: (layer, p, 0, 0, 0, 0))],
        out_specs=spec,
        compiler_params=_cparams("parallel", "parallel"),
        name="nat_attn",
    )(q, k, v, cache_k, cache_v, bias)


def _ssd_body(*refs, seq, has_h0):
    zx_ref, dt_ref, cw_ref, cb_ref, dtb_ref, a_ref, d_ref, nw_ref = refs[:8]
    refs = refs[8:]
    if has_h0:
        h0_ref, y_ref = refs[:2]
        refs = refs[2:]
        hout_ref = None
    else:
        y_ref, hout_ref = refs[:2]
        refs = refs[2:]
        h0_ref = None
    xbc_s, y_s, eb_s, dtv_s, st_s, h_s, tr_s = refs

    q = SSD_CHUNK
    nc = seq // q
    nh = SSD_HEADS
    hd = SSD_HEAD_DIM
    hpg = nh // SSD_GROUPS
    b_off = SSD_D
    c_off = SSD_D + SSD_GROUPS * SSD_STATE
    half = SSD_CONV // 2

    zeros8 = jnp.zeros((8, LANES), F32)
    for cblk in range(SSD_CONV_DIM // LANES):
        cs = slice(cblk * LANES, (cblk + 1) * LANES)
        xcol = slice(SSD_D + cblk * LANES, SSD_D + (cblk + 1) * LANES)
        xin = zx_ref[:, xcol]
        top = jnp.concatenate([zeros8, zx_ref[0:16, xcol]], axis=0)
        bot = jnp.concatenate([zx_ref[seq - 16:seq, xcol], zeros8], axis=0)
        acc = cb_ref[:, cs] + cw_ref[half:half + 1, cs] * xin
        acc_t = cb_ref[:, cs] + cw_ref[half:half + 1, cs] * top[8:16]
        acc_b = cb_ref[:, cs] + cw_ref[half:half + 1, cs] * bot[8:16]
        for kk in range(SSD_CONV):
            d = kk - half
            if d == 0:
                continue
            w = cw_ref[kk:kk + 1, cs]
            acc = acc + w * pltpu.roll(xin, (-d) % seq, axis=0)
            acc_t = acc_t + w * pltpu.roll(top, (-d) % 24, axis=0)[8:16]
            acc_b = acc_b + w * pltpu.roll(bot, (-d) % 24, axis=0)[8:16]
        xbc_s[:, cs] = _silu(acc)
        xbc_s[0:8, cs] = _silu(acc_t)
        xbc_s[seq - 8:seq, cs] = _silu(acc_b)

    lane = lax.broadcasted_iota(jnp.int32, (1, LANES), 1)
    xdt = dt_ref[...] + dtb_ref[...]
    dtv = jnp.maximum(xdt, 0.0) + jnp.log1p(jnp.exp(-jnp.abs(xdt)))
    dtv_s[...] = jnp.where(lane < 2 * nh, dtv, 0.0)

    ii = lax.broadcasted_iota(jnp.int32, (q, q), 0)
    jj = lax.broadcasted_iota(jnp.int32, (q, q), 1)
    lower = jj <= ii
    upper = jj >= ii
    is_fwd = lane < nh
    lo64 = lane < hd
    tri = jnp.concatenate([lower, upper], axis=1).astype(BF16)
    wide = 2 * SSD_D
    ek = lax.broadcasted_iota(jnp.int32, (2 * LANES, wide), 0) & (LANES - 1)
    el = lax.broadcasted_iota(jnp.int32, (2 * LANES, wide), 1) >> (hd.bit_length() - 1)
    expand = (ek == el).astype(BF16)
    grow = lax.broadcasted_iota(jnp.int32, (LANES, SSD_D), 0) >= SSD_STATE
    glane = lax.broadcasted_iota(jnp.int32, (LANES, SSD_D), 1) >= hpg * hd
    own = grow == glane

    def split2(x):
        hi = x.astype(BF16)
        return hi, (x - hi.astype(F32)).astype(BF16)

    def lane_expand(x):
        hi, mid = split2(x)
        return _dot(jnp.concatenate([hi, mid], axis=1), expand)

    def stage_cumsum(c):
        r0 = pl.multiple_of(c * q, q)
        k = dict(c=c, rs=pl.ds(r0, q))
        k['dt'] = dtv_s[k['rs'], :]
        da_c = k['dt'] * a_ref[...]
        rhs = jnp.concatenate([jnp.where(is_fwd, da_c, 0.0), jnp.where(is_fwd, 0.0, da_c)], axis=0)
        p0, p1 = split2(rhs)
        p2 = (rhs - p0.astype(F32) - p1.astype(F32)).astype(BF16)
        k['ac'] = _dot(tri, p0) + _dot(tri, p1) + _dot(tri, p2)
        return k

    def stage_expand(k):
        ac, dt_c = k['ac'], k['dt']
        k['ac_t'] = ac.T
        k['dt_t'] = dt_c.T
        last = jnp.where(is_fwd, ac[q - 1:q, :], ac[0:1, :])
        eb_s[k['rs'], :] = lane_expand(jnp.exp(ac))
        k['wb'] = lane_expand(dt_c * jnp.exp(last - ac))

    def stage_states(k):
        rs = k['rs']
        k['xs'] = xbc_s[rs, 0:SSD_D]
        ball = xbc_s[rs, b_off:c_off].astype(BF16)
        call = xbc_s[rs, c_off:c_off + SSD_GROUPS * SSD_STATE]
        xw = (jnp.concatenate([k['xs'], k['xs']], axis=1) * k['wb']).astype(BF16)
        st_s[k['c']] = _dot_tn(ball, xw)
        k['cb'] = [_dot_nt(jnp.where(lo64 if g == 0 else jnp.logical_not(lo64), call, 0.0).astype(BF16),
                           ball) for g in range(SSD_GROUPS)]

    def stage_intra(k, pr):
        ac, ac_t, dt_t = k['ac'], k['ac_t'], k['dt_t']
        ws = []
        for h in (2 * pr, 2 * pr + 1):
            seg_f = ac[:, h:h + 1] - ac_t[h:h + 1, :]
            seg_b = ac[:, nh + h:nh + h + 1] - ac_t[nh + h:nh + h + 1, :]
            w = k['cb'][h // hpg] * (
                jnp.exp(jnp.where(lower, seg_f, -jnp.inf)) * dt_t[h:h + 1, :]
                + jnp.exp(jnp.where(upper, seg_b, -jnp.inf)) * dt_t[nh + h:nh + h + 1, :])
            ws.append(w.astype(BF16))
        xp = k['xs'][:, pr * LANES:(pr + 1) * LANES]
        xbd = jnp.concatenate([jnp.where(lo64, xp, 0.0), jnp.where(lo64, 0.0, xp)],
                              axis=0).astype(BF16)
        y_s[k['rs'], pr * LANES:(pr + 1) * LANES] = _dot(jnp.concatenate(ws, axis=1), xbd)

    ilv = min(SSD_ILV, nc)

    def chunks(cg, carry):
        ks = [stage_cumsum(cg * ilv + j) for j in range(ilv)]
        for k in ks:
            stage_expand(k)
        for k in ks:
            stage_states(k)
        for pr in range(nh // 2):
            for k in ks:
                stage_intra(k, pr)
        return carry

    lax.fori_loop(0, nc // ilv, chunks, 0)

    if has_h0:
        for direction in range(2):
            tr_s[...] = jnp.zeros(tr_s.shape, F32)
            for h in range(nh):
                g = h // hpg
                tr_s[h * hd:(h + 1) * hd, g * SSD_STATE:(g + 1) * SSD_STATE] = h0_ref[direction * nh + h]
            for k in range(SSD_D // LANES):
                h_s[direction, :, k * LANES:(k + 1) * LANES] = tr_s[k * LANES:(k + 1) * LANES, :].T
    else:
        h_s[...] = jnp.zeros(h_s.shape, F32)

    def carry_states(kstep, carry):
        for direction in range(2):
            c = kstep if direction == 0 else nc - 1 - kstep
            r0 = pl.multiple_of(c * q, q)
            rs = pl.ds(r0, q)
            ds_ = slice(direction * SSD_D, (direction + 1) * SSD_D)
            edge = r0 + q - 1 if direction == 0 else r0
            call = xbc_s[rs, c_off:c_off + SSD_GROUPS * SSD_STATE].astype(BF16)
            h_in = h_s[direction]
            y_s[rs, :] += _dot(call, jnp.where(own, h_in, 0.0).astype(BF16)) * eb_s[rs, ds_]
            h_s[direction] = eb_s[pl.ds(edge, 1), ds_] * h_in + st_s[c][:, ds_]
        return carry

    lax.fori_loop(0, nc, carry_states, 0)

    y = y_s[...] + d_ref[...] * xbc_s[:, 0:SSD_D]
    y = y * _silu(zx_ref[:, 0:SSD_D])
    y_ref[...] = _rms(y, nw_ref[...])
    if hout_ref is not None:
        for direction in range(2):
            for k in range(SSD_D // LANES):
                tr_s[k * LANES:(k + 1) * LANES, :] = h_s[direction, :, k * LANES:(k + 1) * LANES].T
            for h in range(nh):
                g = h // hpg
                hout_ref[direction * nh + h] = tr_s[h * hd:(h + 1) * hd,
                                                    g * SSD_STATE:(g + 1) * SSD_STATE]


def _ssd_call(zx, dt, lw, layer, state, seq):
    has_h0 = state is not None
    rows = zx.shape[0]
    nseq = rows // seq
    nc = seq // SSD_CHUNK
    nst = 2 * SSD_HEADS

    def seqspec(w):
        return pl.BlockSpec((seq, w), lambda b: (b, 0))

    def par(shape):
        return _const_spec(shape, layer, single_buffer=False)

    in_specs = [seqspec(1024), seqspec(LANES), par((8, SSD_CONV_DIM)), par((1, SSD_CONV_DIM)),
                par((1, LANES)), par((1, LANES)), par((1, SSD_D)), par((1, SSD_D))]
    args = [zx, dt, lw['conv_w'], lw['conv_b'], lw['dt_bias'], lw['a'], lw['d'], lw['norm']]
    y_shape = jax.ShapeDtypeStruct((rows, SSD_D), F32)
    if has_h0:
        in_specs.append(pl.BlockSpec((None, None, nst, SSD_HEAD_DIM, SSD_STATE),
                                     lambda b: (b, layer, 0, 0, 0)))
        args.append(state)
        out_shape = y_shape
        out_specs = seqspec(SSD_D)
    else:
        out_shape = [y_shape, jax.ShapeDtypeStruct((nseq, nst, SSD_HEAD_DIM, SSD_STATE), F32)]
        out_specs = [seqspec(SSD_D),
                     pl.BlockSpec((None, nst, SSD_HEAD_DIM, SSD_STATE), lambda b: (b, 0, 0, 0))]
    return pl.pallas_call(
        functools.partial(_ssd_body, seq=seq, has_h0=has_h0),
        out_shape=out_shape,
        grid=(nseq,),
        in_specs=in_specs,
        out_specs=out_specs,
        scratch_shapes=[
            pltpu.VMEM((seq, SSD_CONV_DIM), F32),
            pltpu.VMEM((seq, SSD_D), F32),
            pltpu.VMEM((seq, 2 * SSD_D), F32),
            pltpu.VMEM((seq, LANES), F32),
            pltpu.VMEM((nc, SSD_GROUPS * SSD_STATE, 2 * SSD_D), F32),
            pltpu.VMEM((2, SSD_GROUPS * SSD_STATE, SSD_D), F32),
            pltpu.VMEM((SSD_D, SSD_GROUPS * SSD_STATE), F32),
        ],
        compiler_params=_cparams("parallel"),
        name="ssd_mixer",
    )(*args)


def _s5_tables(a_re, a_im, log_dt, b_re, b_im, c_re, c_im):
    t = S5_T
    gh = S5_HGROUPS
    f32 = lambda x: x.astype(F32)
    lam_r, lam_i = f32(a_re), f32(a_im)
    step = jnp.exp(f32(log_dt))[..., None]
    xr, xi = lam_r * step, lam_i * step

    def powers(ks):
        ks = jnp.asarray(ks, F32)[None, None, :, None, None]
        mag = jnp.exp(xr[:, :, None] * ks)
        return mag * jnp.cos(xi[:, :, None] * ks), mag * jnp.sin(xi[:, :, None] * ks)

    pw_r, pw_i = powers(np.arange(t + 1))
    nr, ni = pw_r[:, :, 1] - 1.0, pw_i[:, :, 1]
    den = lam_r * lam_r + lam_i * lam_i
    fr = (nr * lam_r + ni * lam_i) / den
    fi = (ni * lam_r - nr * lam_i) / den
    br, bi = f32(b_re)[:, None], f32(b_im)[:, None]
    bb_r = fr[..., None] * br - fi[..., None] * bi
    bb_i = fr[..., None] * bi + fi[..., None] * br
    cr, ci = f32(c_re), f32(c_im)

    kern = []
    for direction in range(2):
        pr = jnp.swapaxes(pw_r[:, direction, :t], 1, 2)[:, :, :, None, :]
        pi = jnp.swapaxes(pw_i[:, direction, :t], 1, 2)[:, :, :, None, :]
        cpr = cr[:, :, None] * pr - ci[:, :, None] * pi
        cpi = cr[:, :, None] * pi + ci[:, :, None] * pr
        lhs = jnp.concatenate([cpr, -cpi], axis=-1).reshape(DEPTH, S5_GROUPS, t * S5_GROUP_CH, 2 * S5_STATE)
        rhs = jnp.concatenate([bb_r[:, direction], bb_i[:, direction]], axis=2)
        k = jnp.einsum('dgxn,dgnk->dgxk', lhs, rhs, precision=HIGHEST)
        kern.append(jnp.swapaxes(k.reshape(DEPTH, S5_GROUPS, t, S5_GROUP_CH, S5_GROUP_CH), 1, 2))
    kf, kb = kern
    lag = [kb[:, -d] if d < 0 else (kf[:, 0] + kb[:, 0] if d == 0 else kf[:, d])
           for d in range(-(t - 1), t)]
    kt = jnp.stack(lag, axis=1).reshape(DEPTH, S5_LAGS, 2, gh, S5_GROUP_CH, S5_GROUP_CH)
    kt = jnp.transpose(kt, (0, 2, 1, 5, 3, 4)).reshape(DEPTH, 2, S5_LAGS, S5_GROUP_CH, LANES)

    def by_half(x):
        return jnp.transpose(x.reshape(DEPTH, -1, 2, gh, S5_STATE), (0, 2, 1, 3, 4))

    def dir_powers(direction, ks):
        pr, pi = powers(np.asarray(ks))
        return pr[:, direction], pi[:, direction]

    bt_r = jnp.transpose(bb_r.reshape(DEPTH, 2, 2, gh, S5_STATE, S5_GROUP_CH), (0, 1, 2, 5, 3, 4))
    bt_i = jnp.transpose(bb_i.reshape(DEPTH, 2, 2, gh, S5_STATE, S5_GROUP_CH), (0, 1, 2, 5, 3, 4))
    sw = []
    for direction, ks in ((0, [t - 1 - s for s in range(t)]), (1, list(range(t)))):
        pr, pi = dir_powers(direction, ks)
        pr = by_half(pr)[:, :, :, None]
        pi = by_half(pi)[:, :, :, None]
        wr, wi = bt_r[:, direction][:, :, None], bt_i[:, direction][:, :, None]
        sw += [pr * wr - pi * wi, pr * wi + pi * wr]
    sw = jnp.stack(sw, axis=3).reshape(DEPTH, 2, t, 4, S5_GROUP_CH, gh * S5_STATE)
    ct_r = jnp.transpose(cr.reshape(DEPTH, 2, gh, S5_GROUP_CH, S5_STATE), (0, 1, 4, 2, 3))
    ct_i = jnp.transpose(ci.reshape(DEPTH, 2, gh, S5_GROUP_CH, S5_STATE), (0, 1, 4, 2, 3))
    aw = []
    for direction, ks in ((0, list(range(1, t + 1))), (1, [t - k for k in range(t)])):
        pr, pi = dir_powers(direction, ks)
        pr = jnp.swapaxes(by_half(pr), 3, 4)[..., None]
        pi = jnp.swapaxes(by_half(pi), 3, 4)[..., None]
        wr, wi = ct_r[:, :, None], ct_i[:, :, None]
        aw += [wr * pr - wi * pi, -(wr * pi + wi * pr)]
    aw = jnp.stack(aw, axis=2).reshape(DEPTH, 2, 4, t, S5_STATE, LANES)

    apow = jnp.stack([pw_r[:, 0, t], pw_i[:, 0, t], pw_r[:, 1, t], pw_i[:, 1, t]], axis=1)
    seg_r, seg_i = powers(t * np.arange(S5_NC + 1))
    back_r, back_i = dir_powers(1, t * (S5_NC - 1 - np.arange(S5_NC)))
    pseg = jnp.stack([seg_r[:, 0, :S5_NC], seg_i[:, 0, :S5_NC], back_r, back_i], axis=1)
    nkb = S5_CW // LANES
    pseg = jnp.transpose(pseg.reshape(DEPTH, 4, S5_NC, nkb, LANES), (0, 1, 3, 2, 4))
    pseg = pseg.reshape(DEPTH, 4 * nkb, S5_NC, LANES)
    aseg = jnp.stack([seg_r[:, 0, S5_NC], seg_i[:, 0, S5_NC],
                      seg_r[:, 1, S5_NC], seg_i[:, 1, S5_NC]], axis=1)
    return (kt, sw, aw, apow.reshape(DEPTH, 1, S5_SW), pseg, aseg.reshape(DEPTH, 1, S5_SW))


def _s5_expand_operators(kt_ref, sw_ref, aw_ref, toep_s, sop_s, aop_s):
    gh = S5_HGROUPS
    sh_ch = S5_GROUP_CH.bit_length() - 1
    sh_st = S5_STATE.bit_length() - 1

    def group_mask(shape, row_shift, lane_shift):
        r = lax.broadcasted_iota(jnp.int32, shape, 0) >> row_shift
        c = lax.broadcasted_iota(jnp.int32, shape, 1) >> lane_shift
        return r == c

    def blockdiag(x, mask):
        return jnp.where(mask, jnp.concatenate([x] * gh, axis=0), 0.0).astype(BF16)

    m_kk = group_mask((LANES, LANES), sh_ch, sh_ch)
    m_ks = group_mask((LANES, gh * S5_STATE), sh_ch, sh_st)
    m_sk = group_mask((gh * S5_STATE, LANES), sh_st, sh_ch)
    cw = gh * S5_STATE
    for hf in range(2):
        lags = [blockdiag(kt_ref[hf, d], m_kk) for d in range(S5_LAGS)]
        for s in range(S5_T):
            for t in range(S5_T):
                toep_s[hf, s * LANES:(s + 1) * LANES, t * LANES:(t + 1) * LANES] = lags[t - s + S5_T - 1]
            for comp in range(4):
                sop_s[hf, s * LANES:(s + 1) * LANES, comp * cw:(comp + 1) * cw] = blockdiag(
                    sw_ref[hf, s, comp], m_ks)
        for comp in range(4):
            for t in range(S5_T):
                aop_s[hf, comp * cw:(comp + 1) * cw, t * LANES:(t + 1) * LANES] = blockdiag(
                    aw_ref[hf, comp, t], m_sk)


def _s5_body(up_ref, us_ref, kt_ref, sw_ref, aw_ref, at_ref, pseg_ref, aseg_ref, h0_ref,
             y_ref, hout_ref, toep_s, sop_s, aop_s, st_s, hin_s, yacc_s, xh_s, *, nblk_p, nseg_s):
    i = pl.program_id(0)

    @pl.when(i == 0)
    def _():
        _s5_expand_operators(kt_ref, sw_ref, aw_ref, toep_s, sop_s, aop_s)

    cw = S5_CW
    nkb = cw // LANES

    def seg_rows(c, s):
        return pl.ds(c * S5_T + s, S5_HB, stride=S5_T * S5_NC)

    def chunk_rows(c):
        return slice(c * S5_HB, (c + 1) * S5_HB)

    def gather(src_ref):
        for hf in range(2):
            for c in range(0, S5_NC, 2):
                xh_s[hf, c * S5_HB:(c + 2) * S5_HB, :] = jnp.concatenate(
                    [jnp.concatenate([src_ref[hf, seg_rows(cc, s), :] for s in range(S5_T)], axis=1)
                     for cc in (c, c + 1)], axis=0).astype(BF16)

    @pl.when(i < nblk_p)
    def _():
        gather(up_ref)

    @pl.when(i >= nblk_p)
    def _():
        gather(us_ref)

    for hf in range(2):
        xh = xh_s[hf]
        yacc_s[hf] = _dot(xh, toep_s[hf])
        sh = _dot(xh, sop_s[hf])
        for comp in range(4):
            for k in range(nkb // 2):
                col = (comp * (nkb // 2) + k) * LANES
                st_s[comp * nkb + hf * (nkb // 2) + k] = sh[:, col:col + LANES]

    def load(ref, comp, rows):
        return jnp.concatenate([ref[comp * nkb + k, rows, :] for k in range(nkb)], axis=1)

    def store(ref, comp, rows, val):
        for k in range(nkb):
            ref[comp * nkb + k, rows, :] = val[:, k * LANES:(k + 1) * LANES]

    def cmul(ar, ai, hr, hi):
        return ar * hr - ai * hi, ar * hi + ai * hr

    at = at_ref[...]
    ar_f, ai_f, ar_b, ai_b = [at[:, k * cw:(k + 1) * cw] for k in range(4)]

    def step(c, carry):
        hfr, hfi, hbr, hbi = carry
        rf = chunk_rows(c)
        rb = chunk_rows(S5_NC - 1 - c)
        for comp, val, rows in ((0, hfr, rf), (1, hfi, rf), (2, hbr, rb), (3, hbi, rb)):
            store(hin_s, comp, rows, val)
        fr, fi = cmul(ar_f, ai_f, hfr, hfi)
        br, bi = cmul(ar_b, ai_b, hbr, hbi)
        return (fr + load(st_s, 0, rf), fi + load(st_s, 1, rf),
                br + load(st_s, 2, rb), bi + load(st_s, 3, rb))

    zero = jnp.zeros((S5_HB, cw), F32)
    fin = (zero, zero, zero, zero)
    for c in range(S5_NC):
        fin = step(c, fin)
    hout_ref[...] = jnp.concatenate(fin, axis=-1)

    @pl.when(i >= nblk_p)
    def _():
        h0 = h0_ref[...]
        aseg = aseg_ref[...]
        sr_f, si_f, sr_b, si_b = [aseg[:, k * cw:(k + 1) * cw] for k in range(4)]
        ent = [[None] * S5_HB for _ in range(4)]
        for s in range(S5_HB // nseg_s):
            hr, hi = h0[s:s + 1, 0:cw], h0[s:s + 1, cw:2 * cw]
            for j in range(nseg_s):
                v = s * nseg_s + j
                ent[0][v], ent[1][v] = hr, hi
                hr, hi = cmul(sr_f, si_f, hr, hi)
                hr, hi = hr + fin[0][v:v + 1], hi + fin[1][v:v + 1]
            hr, hi = h0[s:s + 1, 2 * cw:3 * cw], h0[s:s + 1, 3 * cw:4 * cw]
            for j in range(nseg_s - 1, -1, -1):
                v = s * nseg_s + j
                ent[2][v], ent[3][v] = hr, hi
                hr, hi = cmul(sr_b, si_b, hr, hi)
                hr, hi = hr + fin[2][v:v + 1], hi + fin[3][v:v + 1]
        ent = [jnp.concatenate(rows, axis=0) for rows in ent]
        for direction in range(2):
            for k in range(nkb):
                kr = (2 * direction) * nkb + k
                ki = (2 * direction + 1) * nkb + k
                er = ent[2 * direction][:, k * LANES:(k + 1) * LANES]
                ei = ent[2 * direction + 1][:, k * LANES:(k + 1) * LANES]
                for c in range(S5_NC):
                    dr, di = cmul(pseg_ref[kr, c:c + 1, :], pseg_ref[ki, c:c + 1, :], er, ei)
                    hin_s[kr, chunk_rows(c), :] += dr
                    hin_s[ki, chunk_rows(c), :] += di

    for hf in range(2):
        hh = jnp.concatenate(
            [hin_s[comp * nkb + hf * (nkb // 2) + k] for comp in range(4) for k in range(nkb // 2)],
            axis=1).astype(BF16)
        y = yacc_s[hf] + _dot(hh, aop_s[hf])
        for c in range(S5_NC):
            for t in range(S5_T):
                y_ref[hf, seg_rows(c, t), :] = y[c * S5_HB:(c + 1) * S5_HB, t * LANES:(t + 1) * LANES]


def _s5_call(u_p, u_s, tables, layer, h0, seq_p, seq_s):
    kt, sw, aw, apow, pseg, aseg = tables
    seg_tok = S5_T * S5_NC
    assert seq_p == seg_tok and seq_s % seg_tok == 0 and S5_HB % (seq_s // seg_tok) == 0
    blk_tok = S5_ROWS * S5_T
    nblk_p = u_p.shape[1] // blk_tok
    nblk = nblk_p + u_s.shape[1] // blk_tok
    uh = S5_D // LANES

    def par(a):
        return _const_spec(a.shape[1:], layer)

    nlb = S5_SW // LANES
    return pl.pallas_call(
        functools.partial(_s5_body, nblk_p=nblk_p, nseg_s=seq_s // seg_tok),
        out_shape=[jax.ShapeDtypeStruct((uh, nblk * blk_tok, LANES), F32),
                   jax.ShapeDtypeStruct((nblk, S5_HB, S5_SW), F32)],
        grid=(nblk,),
        in_specs=[pl.BlockSpec((uh, blk_tok, LANES), lambda i: (0, jnp.minimum(i, nblk_p - 1), 0)),
                  pl.BlockSpec((uh, blk_tok, LANES), lambda i: (0, jnp.maximum(i - nblk_p, 0), 0)),
                  par(kt), par(sw), par(aw), par(apow), par(pseg), par(aseg),
                  pl.BlockSpec((None, None, S5_HB, S5_SW), lambda i: (layer, i, 0, 0))],
        out_specs=[pl.BlockSpec((uh, blk_tok, LANES), lambda i: (0, i, 0)),
                   pl.BlockSpec((None, S5_HB, S5_SW), lambda i: (i, 0, 0))],
        scratch_shapes=[pltpu.VMEM((2, S5_KW, S5_KW), BF16), pltpu.VMEM((2, S5_KW, S5_SH), BF16),
                        pltpu.VMEM((2, S5_SH, S5_KW), BF16),
                        pltpu.VMEM((nlb, S5_ROWS, LANES), F32), pltpu.VMEM((nlb, S5_ROWS, LANES), F32),
                        pltpu.VMEM((2, S5_ROWS, S5_KW), F32), pltpu.VMEM((2, S5_ROWS, S5_KW), BF16)],
        compiler_params=_cparams("arbitrary"),
        name="s5_mixer",
    )(u_p, u_s, kt, sw, aw, apow, pseg, aseg, h0)


def _gelu_tanh(x):
    return 0.5 * x * (1.0 + jnp.tanh(math.sqrt(2.0 / math.pi) * (x + 0.044715 * (x * x * x))))


def _out_body(*refs, final):
    (x_ref, yssd_ref, ynat_ref, u_ref, y5_ref, g1_ref, m2_ref, d5_ref,
     wglu_ref, bglu_ref, wout_ref, nm_ref, w1_ref, w2_ref) = refs[:14]
    if final:
        nf_ref, o_ref = refs[14:]
    else:
        (o_ref,) = refs[14:]
    uh = S5_D // LANES
    for sb in range(ROW_TILE // TOK_TILE):
        rows = slice(sb * TOK_TILE, (sb + 1) * TOK_TILE)
        y5_in = jnp.concatenate([y5_ref[hf, rows, :] for hf in range(uh)], axis=1)
        u = jnp.concatenate([u_ref[hf, rows, :] for hf in range(uh)], axis=1)
        g = _gelu_tanh(y5_in + d5_ref[...] * u)
        y5 = g * jax.nn.sigmoid(_dot(g.astype(BF16), wglu_ref[...]) + bglu_ref[...])
        mix = (_dot(yssd_ref[rows, :].astype(BF16), wout_ref[0:SSD_D, :])
               + _dot(y5.astype(BF16), wout_ref[SSD_D:SSD_D + S5_D, :])
               + _dot(ynat_ref[rows, :].astype(BF16), wout_ref[SSD_D + S5_D:, :]))
        x = x_ref[rows, :] + g1_ref[...] * mix
        h2 = _rms(x, nm_ref[...]) * (1.0 + m2_ref[:, D_MODEL:2 * D_MODEL]) + m2_ref[:, 0:D_MODEL]
        f = jnp.maximum(_dot(h2.astype(BF16), w1_ref[...]), 0.0)
        f = (f * f).astype(BF16)
        x = x + m2_ref[:, 2 * D_MODEL:3 * D_MODEL] * _dot(f, w2_ref[...])
        if final:
            x = _rms(x, nf_ref[...])
        o_ref[rows, :] = x


def _out_call(x, y_ssd, y_nat, u, y5, y5_tile_off, cond_idx, mods4, lw, norm_f, layer, final):
    rows = x.shape[0]
    uh = S5_D // LANES

    def tile(w):
        return pl.BlockSpec((ROW_TILE, w), lambda i: (i, 0))

    def par(shape):
        return _const_spec(shape, layer)

    in_specs = [
        tile(D_MODEL), tile(SSD_D), tile(NAT_D),
        pl.BlockSpec((uh, ROW_TILE, LANES), lambda i: (0, i, 0)),
        pl.BlockSpec((uh, ROW_TILE, LANES), lambda i: (0, i + y5_tile_off, 0)),
        pl.BlockSpec((None, None, 1, D_MODEL), lambda i: (layer, cond_idx(i), 0, 2)),
        pl.BlockSpec((None, None, 1, 3 * D_MODEL), lambda i: (layer, cond_idx(i), 0, 1)),
        par((1, S5_D)), par((S5_D, S5_D)), par((1, S5_D)),
        par((D_MODEL, D_MODEL)), par((1, D_MODEL)),
        par((D_MODEL, D_FF)), par((D_FF, D_MODEL))]
    args = [x, y_ssd, y_nat, u, y5, mods4, mods4, lw['s5_d'], lw['w_glu'], lw['b_glu'],
            lw['w_out'], lw['norm_mlp'], lw['w_ff1'], lw['w_ff2']]
    if final:
        in_specs.append(pl.BlockSpec((1, D_MODEL), lambda i: (0, 0)))
        args.append(norm_f.reshape(1, D_MODEL))
    return pl.pallas_call(
        functools.partial(_out_body, final=final),
        out_shape=jax.ShapeDtypeStruct((rows, D_MODEL), F32),
        grid=(rows // ROW_TILE,),
        in_specs=in_specs,
        out_specs=tile(D_MODEL),
        compiler_params=_cparams("parallel"),
        name="out_mlp",
    )(*args)


def _lane_pad(x):
    return jnp.pad(x, [(0, 0)] * (x.ndim - 1) + [(0, LANES - x.shape[-1])])


def kernel(x_prompt, x_sample, cache_nat_k, cache_nat_v, state_ssd, state_s5_re, state_s5_im,
           c, c_ctx, w_mod, b_mod, norm_mix, norm_mlp, w_in, ssd_conv_w, ssd_conv_b,
           ssd_dt_bias, ssd_a_log, ssd_d, ssd_norm, s5_a_re, s5_a_im, s5_log_dt,
           s5_b_re, s5_b_im, s5_c_re, s5_c_im, s5_d, s5_w_glu, s5_b_glu, nat_rpb,
           w_out, w_ff1, w_ff2, norm_f):
    bp, seq_p, _ = x_prompt.shape
    bs, seq_s, _ = x_sample.shape
    assert seq_p == TOK_TILE and seq_s == 4 * TOK_TILE
    rows_p = bp * seq_p
    rows_s = bs * seq_s

    ncp = -(-(1 + bs) // 8) * 8
    cond = jnp.concatenate([c_ctx[None, :], c, jnp.zeros((ncp - 1 - bs, D_MODEL), F32)], axis=0)
    mods = _mods_call(cond, w_mod, b_mod)
    mods4 = mods.reshape(DEPTH, ncp, 1, N_MOD * D_MODEL)

    assert w_in.shape[-1] == IN_COLS
    w_in_p = jnp.pad(w_in.astype(BF16), [(0, 0), (0, 0), (0, IN_COLS_PAD - IN_COLS)])
    ssd_w = {
        'conv_w': jnp.pad(ssd_conv_w.astype(F32), [(0, 0), (0, 8 - SSD_CONV), (0, 0)]),
        'conv_b': ssd_conv_b.astype(F32).reshape(DEPTH, 1, SSD_CONV_DIM),
        'dt_bias': _lane_pad(ssd_dt_bias.astype(F32).reshape(DEPTH, 1, 2 * SSD_HEADS)),
        'a': _lane_pad(-jnp.exp(ssd_a_log.astype(F32)).reshape(DEPTH, 1, 2 * SSD_HEADS)),
        'd': jnp.repeat(ssd_d.astype(F32), SSD_HEAD_DIM, axis=-1).reshape(DEPTH, 1, SSD_D),
        'norm': ssd_norm.astype(F32).reshape(DEPTH, 1, SSD_D),
    }
    out_w = {
        's5_d': s5_d.astype(F32).reshape(DEPTH, 1, S5_D),
        'w_glu': s5_w_glu.astype(BF16),
        'b_glu': s5_b_glu.astype(F32).reshape(DEPTH, 1, S5_D),
        'w_out': w_out.astype(BF16),
        'norm_mlp': norm_mlp.astype(F32).reshape(DEPTH, 1, D_MODEL),
        'w_ff1': w_ff1.astype(BF16),
        'w_ff2': w_ff2.astype(BF16),
    }
    s5_tabs = _s5_tables(s5_a_re, s5_a_im, s5_log_dt, s5_b_re, s5_b_im, s5_c_re, s5_c_im)
    nat_bias = _nat_pair_tiles(nat_rpb)

    cache_k = cache_nat_k.reshape(bs, DEPTH, -1, NAT_D)
    cache_v = cache_nat_v.reshape(bs, DEPTH, -1, NAT_D)
    st_ssd = state_ssd.reshape(bs, DEPTH, 2 * SSD_HEADS, SSD_HEAD_DIM, SSD_STATE)
    s5_tok = S5_ROWS * S5_T
    assert rows_p % s5_tok == 0 and rows_s % s5_tok == 0 and s5_tok % seq_s == 0
    nseq_s = s5_tok // seq_s
    st5 = jnp.stack([state_s5_re[:, :, 0], state_s5_im[:, :, 0],
                     state_s5_re[:, :, 1], state_s5_im[:, :, 1]], axis=2)
    st5 = jnp.transpose(st5.astype(F32).reshape(bs // nseq_s, nseq_s, DEPTH, S5_SW), (2, 0, 1, 3))
    st5 = jnp.pad(st5, [(0, 0), (rows_p // s5_tok, 0), (0, S5_HB - nseq_s), (0, 0)])

    assert rows_p % ROW_TILE == 0 and seq_s % ROW_TILE == 0
    tiles_per_seq = seq_s // ROW_TILE
    cond_p = lambda i: 0
    cond_s = lambda i: 1 + i // tiles_per_seq
    x_p = x_prompt.reshape(rows_p, D_MODEL)
    x_s = x_sample.reshape(rows_s, D_MODEL)
    k_all = jnp.zeros((bp, DEPTH, seq_p, NAT_D), F32)
    v_all = jnp.zeros((bp, DEPTH, seq_p, NAT_D), F32)
    new_ssd, new_s5 = [], []
    for l in range(DEPTH):
        final = l == DEPTH - 1
        zx_p, dt_p, u_p, q_p, k_all, v_all = _inproj_call(x_p, cond_p, mods4, norm_mix, w_in_p, l,
                                                         kv_cache=(k_all, v_all))
        zx_s, dt_s, u_s, q_s, k_s, v_s = _inproj_call(x_s, cond_s, mods4, norm_mix, w_in_p, l)

        y_nat_p = _ctx_attn_call(q_p, k_all, v_all, l)
        y_nat_s = _nat_attn_call(q_s, k_s, v_s, cache_k, cache_v, nat_bias, l, bs, seq_s)

        y_ssd_p, ssd_l = _ssd_call(zx_p, dt_p, ssd_w, l, None, seq_p)
        y_ssd_s = _ssd_call(zx_s, dt_s, ssd_w, l, st_ssd, seq_s)
        new_ssd.append(ssd_l)

        y5, s5_l = _s5_call(u_p, u_s, s5_tabs, l, st5, seq_p, seq_s)
        new_s5.append(s5_l[:rows_p // s5_tok])

        x_p = _out_call(x_p, y_ssd_p, y_nat_p, u_p, y5, 0, cond_p, mods4, out_w, norm_f, l, final)
        x_s = _out_call(x_s, y_ssd_s, y_nat_s, u_s, y5, rows_p // ROW_TILE, cond_s,
                        mods4, out_w, norm_f, l, final)

    y_prompt = x_p.reshape(bp, seq_p, D_MODEL)
    y_sample = x_s.reshape(bs, seq_s, D_MODEL)
    out_k = k_all.reshape(bp, DEPTH, seq_p, NAT_HEADS, NAT_HEAD_DIM)
    out_v = v_all.reshape(bp, DEPTH, seq_p, NAT_HEADS, NAT_HEAD_DIM)
    out_ssd = jnp.stack(new_ssd, axis=1).reshape(bp, DEPTH, 2, SSD_HEADS, SSD_HEAD_DIM, SSD_STATE)
    s5 = jnp.stack(new_s5, axis=0)[:, :, :s5_tok // seq_p]
    s5 = s5.reshape(DEPTH, bp, 4, S5_GROUPS, S5_STATE)
    s5 = jnp.transpose(s5, (1, 0, 2, 3, 4))
    out_re = s5[:, :, 0::2]
    out_im = s5[:, :, 1::2]
    return y_prompt, y_sample, out_k, out_v, out_ssd, out_re, out_im
```

```python
import functools
import math

import numpy as np
import jax
import jax.numpy as jnp
from jax import lax
from jax.experimental import pallas as pl
from jax.experimental.pallas import tpu as pltpu

F32 = jnp.float32
BF16 = jnp.bfloat16
HIGHEST = lax.Precision.HIGHEST

D_MODEL = 1024
DEPTH = 4
GRID_W = 64
SSD_HEADS = 6
SSD_HEAD_DIM = 64
SSD_D = SSD_HEADS * SSD_HEAD_DIM
SSD_GROUPS = 2
SSD_STATE = 64
SSD_CONV = 5
SSD_CHUNK = 128
SSD_CONV_DIM = SSD_D + 2 * SSD_GROUPS * SSD_STATE
SSD_ILV = 4
S5_GROUPS = 16
S5_GROUP_CH = 16
S5_D = S5_GROUPS * S5_GROUP_CH
S5_STATE = 64
NAT_HEADS = 6
NAT_HEAD_DIM = 64
NAT_D = NAT_HEADS * NAT_HEAD_DIM
NAT_KH = 8
NAT_KW = 16
D_FF = 4 * D_MODEL
N_MOD = 6
EPS = 1e-6

LANES = 128
TOK_TILE = 256
ROW_TILE = 512
IN_COLS = SSD_D + SSD_CONV_DIM + 2 * SSD_HEADS + S5_D + 3 * NAT_D
IN_COLS_PAD = -(-IN_COLS // LANES) * LANES
S5_T = 8
S5_HGROUPS = LANES // S5_GROUP_CH
S5_KW = S5_T * LANES
S5_CW = S5_GROUPS * S5_STATE
S5_SW = 4 * S5_CW
S5_SH = S5_SW // 2
S5_NC = 32
S5_HB = 8
S5_ROWS = S5_HB * S5_NC
S5_LAGS = 2 * S5_T - 1
VMEM_LIMIT = 56 * 1024 * 1024


def _cparams(*sem):
    return pltpu.CompilerParams(dimension_semantics=sem, vmem_limit_bytes=VMEM_LIMIT)


def _dot(a, b, precision=None):
    return jnp.dot(a, b, preferred_element_type=F32, precision=precision)


def _dot_nt(a, b, precision=None):
    return lax.dot_general(a, b, (((1,), (1,)), ((), ())), preferred_element_type=F32,
                           precision=precision)


def _dot_tn(a, b, precision=None):
    return lax.dot_general(a, b, (((0,), (0,)), ((), ())), preferred_element_type=F32,
                           precision=precision)


def _silu(x):
    hx = 0.5 * x
    return hx + hx * jnp.tanh(hx)


def _rms(x, g):
    return x * lax.rsqrt(jnp.mean(x * x, axis=-1, keepdims=True) + EPS) * g


def _const_spec(shape, layer, single_buffer=True):
    return pl.BlockSpec((None,) + tuple(shape), lambda *_: (layer,) + (0,) * len(shape),
                        pipeline_mode=pl.Buffered(1) if single_buffer else None)


def _mods_body(cond_ref, w_ref, b_ref, o_ref):
    s = _silu(cond_ref[...])
    o_ref[...] = _dot(s.astype(BF16), w_ref[...].astype(BF16)) + b_ref[...]


def _mods_call(cond, w_mod, b_mod):
    ncp = cond.shape[0]
    blk = 2 * D_MODEL
    return pl.pallas_call(
        _mods_body,
        out_shape=jax.ShapeDtypeStruct((DEPTH, ncp, N_MOD * D_MODEL), F32),
        grid=(DEPTH, N_MOD * D_MODEL // blk),
        in_specs=[
            pl.BlockSpec((ncp, D_MODEL), lambda l, j: (0, 0)),
            pl.BlockSpec((None, D_MODEL, blk), lambda l, j: (l, 0, j)),
            pl.BlockSpec((None, 1, blk), lambda l, j: (l, 0, j)),
        ],
        out_specs=pl.BlockSpec((None, ncp, blk), lambda l, j: (l, 0, j)),
        compiler_params=_cparams("arbitrary", "arbitrary"),
        name="adaln_mods",
    )(cond, w_mod, b_mod.reshape(DEPTH, 1, N_MOD * D_MODEL))


def _inproj_body(*refs, kv_cache):
    if kv_cache:
        x_ref, mod_ref, g_ref, w_ref, _, _, zx_ref, dt_ref, u_ref, q_ref, k_ref, v_ref = refs
    else:
        x_ref, mod_ref, g_ref, w_ref, zx_ref, dt_ref, u_ref, q_ref, k_ref, v_ref = refs
    o_dt = SSD_D + SSD_CONV_DIM
    for sb in range(ROW_TILE // TOK_TILE):
        rows = slice(sb * TOK_TILE, (sb + 1) * TOK_TILE)
        x = x_ref[rows, :]
        h = _rms(x, g_ref[...]) * (1.0 + mod_ref[:, D_MODEL:2 * D_MODEL]) + mod_ref[:, 0:D_MODEL]
        p = _dot(h.astype(BF16), w_ref[...])
        zx_ref[rows, :] = p[:, 0:o_dt]
        dt_ref[rows, :] = p[:, o_dt:o_dt + LANES]
        rest = p[:, o_dt + 2 * SSD_HEADS:]
        for hf in range(S5_D // LANES):
            u_ref[hf, rows, :] = rest[:, hf * LANES:(hf + 1) * LANES]
        q_ref[rows, :] = rest[:, S5_D:S5_D + NAT_D]
        kv_rows = sb if kv_cache else rows
        k_ref[kv_rows] = rest[:, S5_D + NAT_D:S5_D + 2 * NAT_D]
        v_ref[kv_rows] = rest[:, S5_D + 2 * NAT_D:S5_D + 3 * NAT_D]


def _inproj_call(x, cond_idx, mods4, norm_mix, w_in_p, layer, kv_cache=None):
    rows = x.shape[0]

    def tile(w):
        return pl.BlockSpec((ROW_TILE, w), lambda i: (i, 0))

    def out(w):
        return jax.ShapeDtypeStruct((rows, w), F32)

    uh = S5_D // LANES
    in_specs = [
        tile(D_MODEL),
        pl.BlockSpec((None, None, 1, 2 * D_MODEL), lambda i: (layer, cond_idx(i), 0, 0)),
        _const_spec((1, D_MODEL), layer, single_buffer=False),
        _const_spec((D_MODEL, IN_COLS_PAD), layer),
    ]
    args = [x, mods4, norm_mix.reshape(DEPTH, 1, D_MODEL), w_in_p]
    if kv_cache is None:
        kv_shapes = [out(NAT_D), out(NAT_D)]
        kv_specs = [tile(NAT_D), tile(NAT_D)]
        aliases = {}
    else:
        kv_shapes = [jax.ShapeDtypeStruct(b.shape, F32) for b in kv_cache]
        kv_specs = [pl.BlockSpec((ROW_TILE // TOK_TILE, None, TOK_TILE, NAT_D),
                                 lambda i: (i, layer, 0, 0))] * 2
        in_specs += [pl.BlockSpec(memory_space=pl.ANY)] * 2
        args += list(kv_cache)
        aliases = {4: 4, 5: 5}
    return pl.pallas_call(
        functools.partial(_inproj_body, kv_cache=kv_cache is not None),
        out_shape=[out(1024), out(LANES), jax.ShapeDtypeStruct((uh, rows, LANES), F32),
                   out(NAT_D)] + kv_shapes,
        grid=(rows // ROW_TILE,),
        in_specs=in_specs,
        out_specs=[tile(1024), tile(LANES), pl.BlockSpec((uh, ROW_TILE, LANES), lambda i: (0, i, 0)),
                   tile(NAT_D)] + kv_specs,
        input_output_aliases=aliases,
        compiler_params=_cparams("parallel"),
        name="in_proj",
    )(*args)


NAT_SCALE = NAT_HEAD_DIM ** -0.5


def _values_with_ones(v):
    return jnp.concatenate([v, jnp.ones(v.shape, BF16)], axis=1)


def _pair_attention(q, keys, v_ext, bias, lo):
    outs = []
    for hh in range(2):
        qm = jnp.where(lo if hh == 0 else jnp.logical_not(lo), q, 0.0).astype(BF16)
        s = _dot_nt(qm, keys)
        if bias is not None:
            b = bias(hh)
            nb = b.shape[1]
            s = jnp.concatenate([s[:, 0:nb] + b, s[:, nb:]], axis=1)
        p = jnp.exp(s - jnp.max(s, axis=-1, keepdims=True)).astype(BF16)
        o = _dot(p, v_ext)
        outs.append(o[:, 0:LANES] / o[:, LANES:2 * LANES])
    return jnp.where(lo, outs[0], outs[1])


def _ctx_attn_body(q_ref, k_ref, v_ref, o_ref):
    lo = lax.broadcasted_iota(jnp.int32, (1, LANES), 1) < NAT_HEAD_DIM
    for sq in range(ROW_TILE // TOK_TILE):
        rs = slice(sq * TOK_TILE, (sq + 1) * TOK_TILE)
        for p in range(NAT_D // LANES):
            cs = slice(p * LANES, (p + 1) * LANES)
            o_ref[rs, cs] = _pair_attention(q_ref[rs, cs] * NAT_SCALE, k_ref[sq, :, cs].astype(BF16),
                                            _values_with_ones(v_ref[sq, :, cs].astype(BF16)), None, lo)


def _ctx_attn_call(q, k, v, layer):
    rows = q.shape[0]
    spec = pl.BlockSpec((ROW_TILE, NAT_D), lambda b: (b, 0))
    kvspec = pl.BlockSpec((ROW_TILE // TOK_TILE, None, TOK_TILE, NAT_D), lambda b: (b, layer, 0, 0))
    return pl.pallas_call(
        _ctx_attn_body,
        out_shape=jax.ShapeDtypeStruct((rows, NAT_D), F32),
        grid=(rows // ROW_TILE,),
        in_specs=[spec, kvspec, kvspec],
        out_specs=spec,
        compiler_params=_cparams("parallel"),
        name="ctx_attn",
    )(q, k, v)


NAT_QROWS = 4
NAT_HEAD_PAIRS = NAT_HEADS // 2
NAT_NPAIR = 2 * NAT_KH


def _nat_window_start(r, rows):
    kh = min(NAT_KH, rows)
    return int(np.clip(r - kh // 2, 0, rows - kh))


def _nat_blocks(rows):
    kh = min(NAT_KH, rows)
    out = []
    for j in range(rows // NAT_QROWS):
        rs = [_nat_window_start(r, rows) for r in range(j * NAT_QROWS, (j + 1) * NAT_QROWS)]
        first = min(rs)
        n = -(-(max(rs) + kh - first) // NAT_QROWS) * NAT_QROWS
        first = min(first, rows - n)
        out.append((first, n))
    return out


def _nat_pair_tiles(rpb):
    cols = np.arange(GRID_W)
    c_start = np.clip(cols - NAT_KW // 2, 0, GRID_W - NAT_KW)
    col_mask = (cols[None, :] >= c_start[:, None]) & (cols[None, :] < c_start[:, None] + NAT_KW)
    idx = cols[None, :] - cols[:, None] + NAT_KW - 1
    sel = (idx[None] == np.arange(2 * NAT_KW - 1)[:, None, None]).astype(np.float32)
    tiles = jnp.einsum('dhab,bqk->dhaqk', rpb.astype(F32), sel, precision=HIGHEST)
    tiles = jnp.where(col_mask, tiles, -jnp.inf)
    neg = jnp.full(tiles.shape[:2] + (1, GRID_W, GRID_W), -jnp.inf, F32)
    ext = jnp.concatenate([neg, tiles, neg], axis=2)
    pairs = jnp.concatenate([ext[:, :, 0:NAT_NPAIR], ext[:, :, 1:NAT_NPAIR + 1]], axis=-1)
    return pairs.reshape(DEPTH, NAT_HEAD_PAIRS, 2, NAT_NPAIR, GRID_W, 2 * GRID_W)


def _nat_block_bias(bias_ref, hh, j, first, n, rows, lo):
    kh = min(NAT_KH, rows)
    neg = jnp.full((GRID_W, 2 * GRID_W), -jnp.inf, F32)
    row_blocks = []
    for ql in range(NAT_QROWS):
        qr = j * NAT_QROWS + ql
        rs = _nat_window_start(qr, rows)
        pieces = []
        for m in range(n // 2):
            k0 = first + 2 * m
            ok0 = rs <= k0 < rs + kh
            ok1 = rs <= k0 + 1 < rs + kh
            if not (ok0 or ok1):
                pieces.append(neg)
                continue
            t = bias_ref[hh, k0 - qr + NAT_KH]
            if ok0 and ok1:
                pieces.append(t)
            elif ok0:
                pieces.append(jnp.where(lo, t, -jnp.inf))
            else:
                pieces.append(jnp.where(lo, -jnp.inf, t))
        row_blocks.append(jnp.concatenate(pieces, axis=1))
    return jnp.concatenate(row_blocks, axis=0)


def _nat_attn_body(q_ref, k_ref, v_ref, kc_ref, vc_ref, bias_ref, o_ref, *, rows):
    lo = lax.broadcasted_iota(jnp.int32, (1, LANES), 1) < NAT_HEAD_DIM
    kc = kc_ref[...].astype(BF16)
    vc = vc_ref[...].astype(BF16)
    nq = NAT_QROWS * GRID_W
    for j, (first, n) in enumerate(_nat_blocks(rows)):
        q = q_ref[j * nq:(j + 1) * nq, :] * NAT_SCALE
        ks = slice(first * GRID_W, (first + n) * GRID_W)
        keys = jnp.concatenate([k_ref[ks, :].astype(BF16), kc], axis=0)
        v_ext = _values_with_ones(jnp.concatenate([v_ref[ks, :].astype(BF16), vc], axis=0))

        def bias(hh, j=j, first=first, n=n):
            return _nat_block_bias(bias_ref, hh, j, first, n, rows, lo)

        o_ref[j * nq:(j + 1) * nq, :] = _pair_attention(q, keys, v_ext, bias, lo)


def _nat_attn_call(q, k, v, cache_k, cache_v, bias, layer, bs, seq):
    rows = seq // GRID_W
    assert rows % NAT_QROWS == 0 and 2 * GRID_W == LANES
    spec = pl.BlockSpec((seq, LANES), lambda p, b: (b, p))
    cspec = pl.BlockSpec((None, None, cache_k.shape[2], LANES), lambda p, b: (b, layer, 0, p))
    return pl.pallas_call(
        functools.partial(_nat_attn_body, rows=rows),
        out_shape=jax.ShapeDtypeStruct(q.shape, F32),
        grid=(NAT_HEAD_PAIRS, bs),
        in_specs=[spec, spec, spec, cspec, cspec,
                  pl.BlockSpec((None, None) + bias.shape[2:], lambda p, b: (layer, p, 0, 0, 0, 0))],
        out_specs=spec,
        compiler_params=_cparams("parallel", "parallel"),
        name="nat_attn",
    )(q, k, v, cache_k, cache_v, bias)


def _ssd_body(*refs, seq, has_h0):
    zx_ref, dt_ref, cw_ref, cb_ref, dtb_ref, a_ref, d_ref, nw_ref = refs[:8]
    refs = refs[8:]
    if has_h0:
        h0_ref, y_ref = refs[:2]
        refs = refs[2:]
        hout_ref = None
    else:
        y_ref, hout_ref = refs[1:3]
        refs = refs[3:]
        h0_ref = None
    xbc_s, y_s, eb_s, dtv_s, st_s, h_s, tr_s = refs

    q = SSD_CHUNK
    nc = seq // q
    nh = SSD_HEADS
    hd = SSD_HEAD_DIM
    hpg = nh // SSD_GROUPS
    b_off = SSD_D
    c_off = SSD_D + SSD_GROUPS * SSD_STATE
    half = SSD_CONV // 2

    zeros8 = jnp.zeros((8, LANES), F32)
    for cblk in range(SSD_CONV_DIM // LANES):
        cs = slice(cblk * LANES, (cblk + 1) * LANES)
        xcol = slice(SSD_D + cblk * LANES, SSD_D + (cblk + 1) * LANES)
        xin = zx_ref[:, xcol]
        top = jnp.concatenate([zeros8, zx_ref[0:16, xcol]], axis=0)
        bot = jnp.concatenate([zx_ref[seq - 16:seq, xcol], zeros8], axis=0)
        acc = cb_ref[:, cs] + cw_ref[half:half + 1, cs] * xin
        acc_t = cb_ref[:, cs] + cw_ref[half:half + 1, cs] * top[8:16]
        acc_b = cb_ref[:, cs] + cw_ref[half:half + 1, cs] * bot[8:16]
        for kk in range(SSD_CONV):
            d = kk - half
            if d == 0:
                continue
            w = cw_ref[kk:kk + 1, cs]
            acc = acc + w * pltpu.roll(xin, (-d) % seq, axis=0)
            acc_t = acc_t + w * pltpu.roll(top, (-d) % 24, axis=0)[8:16]
            acc_b = acc_b + w * pltpu.roll(bot, (-d) % 24, axis=0)[8:16]
        xbc_s[:, cs] = _silu(acc)
        xbc_s[0:8, cs] = _silu(acc_t)
        xbc_s[seq - 8:seq, cs] = _silu(acc_b)

    lane = lax.broadcasted_iota(jnp.int32, (1, LANES), 1)
    xdt = dt_ref[...] + dtb_ref[...]
    dtv = jnp.maximum(xdt, 0.0) + jnp.log1p(jnp.exp(-jnp.abs(xdt)))
    dtv_s[...] = jnp.where(lane < 2 * nh, dtv, 0.0)

    ii = lax.broadcasted_iota(jnp.int32, (q, q), 0)
    jj = lax.broadcasted_iota(jnp.int32, (q, q), 1)
    lower = jj <= ii
    upper = jj >= ii
    is_fwd = lane < nh
    lo64 = lane < hd
    tri = jnp.concatenate([lower, upper], axis=1).astype(BF16)
    wide = 2 * SSD_D
    ek = lax.broadcasted_iota(jnp.int32, (2 * LANES, wide), 0) & (LANES - 1)
    el = lax.broadcasted_iota(jnp.int32, (2 * LANES, wide), 1) >> (hd.bit_length() - 1)
    expand = (ek == el).astype(BF16)
    grow = lax.broadcasted_iota(jnp.int32, (LANES, SSD_D), 0) >= SSD_STATE
    glane = lax.broadcasted_iota(jnp.int32, (LANES, SSD_D), 1) >= hpg * hd
    own = grow == glane

    def split2(x):
        hi = x.astype(BF16)
        return hi, (x - hi.astype(F32)).astype(BF16)

    def lane_expand(x):
        hi, mid = split2(x)
        return _dot(jnp.concatenate([hi, mid], axis=1), expand)

    def stage_cumsum(c):
        r0 = pl.multiple_of(c * q, q)
        k = dict(c=c, rs=pl.ds(r0, q))
        k['dt'] = dtv_s[k['rs'], :]
        da_c = k['dt'] * a_ref[...]
        rhs = jnp.concatenate([jnp.where(is_fwd, da_c, 0.0), jnp.where(is_fwd, 0.0, da_c)], axis=0)
        p0, p1 = split2(rhs)
        p2 = (rhs - p0.astype(F32) - p1.astype(F32)).astype(BF16)
        k['ac'] = _dot(tri, p0) + _dot(tri, p1) + _dot(tri, p2)
        return k

    def stage_expand(k):
        ac, dt_c = k['ac'], k['dt']
        k['ac_t'] = ac.T
        k['dt_t'] = dt_c.T
        last = jnp.where(is_fwd, ac[q - 1:q, :], ac[0:1, :])
        eb_s[k['rs'], :] = lane_expand(jnp.exp(ac))
        k['wb'] = lane_expand(dt_c * jnp.exp(last - ac))

    def stage_states(k):
        rs = k['rs']
        k['xs'] = xbc_s[rs, 0:SSD_D]
        ball = xbc_s[rs, b_off:c_off].astype(BF16)
        call = xbc_s[rs, c_off:c_off + SSD_GROUPS * SSD_STATE]
        xw = (jnp.concatenate([k['xs'], k['xs']], axis=1) * k['wb']).astype(BF16)
        st_s[k['c']] = _dot_tn(ball, xw)
        k['cb'] = [_dot_nt(jnp.where(lo64 if g == 0 else jnp.logical_not(lo64), call, 0.0).astype(BF16),
                           ball) for g in range(SSD_GROUPS)]

    def stage_intra(k, pr):
        ac, ac_t, dt_t = k['ac'], k['ac_t'], k['dt_t']
        ws = []
        for h in (2 * pr, 2 * pr + 1):
            seg_f = ac[:, h:h + 1] - ac_t[h:h + 1, :]
            seg_b = ac[:, nh + h:nh + h + 1] - ac_t[nh + h:nh + h + 1, :]
            w = k['cb'][h // hpg] * (
                jnp.exp(jnp.where(lower, seg_f, -jnp.inf)) * dt_t[h:h + 1, :]
                + jnp.exp(jnp.where(upper, seg_b, -jnp.inf)) * dt_t[nh + h:nh + h + 1, :])
            ws.append(w.astype(BF16))
        xp = k['xs'][:, pr * LANES:(pr + 1) * LANES]
        xbd = jnp.concatenate([jnp.where(lo64, xp, 0.0), jnp.where(lo64, 0.0, xp)],
                              axis=0).astype(BF16)
        y_s[k['rs'], pr * LANES:(pr + 1) * LANES] = _dot(jnp.concatenate(ws, axis=1), xbd)

    ilv = min(SSD_ILV, nc)

    def chunks(cg, carry):
        ks = [stage_cumsum(cg * ilv + j) for j in range(ilv)]
        for k in ks:
            stage_expand(k)
        for k in ks:
            stage_states(k)
        for pr in range(nh // 2):
            for k in ks:
                stage_intra(k, pr)
        return carry

    lax.fori_loop(0, nc // ilv, chunks, 0)

    if has_h0:
        for direction in range(2):
            tr_s[...] = jnp.zeros(tr_s.shape, F32)
            for h in range(nh):
                g = h // hpg
                tr_s[h * hd:(h + 1) * hd, g * SSD_STATE:(g + 1) * SSD_STATE] = h0_ref[direction * nh + h]
            for k in range(SSD_D // LANES):
                h_s[direction, :, k * LANES:(k + 1) * LANES] = tr_s[k * LANES:(k + 1) * LANES, :].T
    else:
        h_s[...] = jnp.zeros(h_s.shape, F32)

    def carry_states(kstep, carry):
        for direction in range(2):
            c = kstep if direction == 0 else nc - 1 - kstep
            r0 = pl.multiple_of(c * q, q)
            rs = pl.ds(r0, q)
            ds_ = slice(direction * SSD_D, (direction + 1) * SSD_D)
            edge = r0 + q - 1 if direction == 0 else r0
            call = xbc_s[rs, c_off:c_off + SSD_GROUPS * SSD_STATE].astype(BF16)
            h_in = h_s[direction]
            y_s[rs, :] += _dot(call, jnp.where(own, h_in, 0.0).astype(BF16)) * eb_s[rs, ds_]
            h_s[direction] = eb_s[pl.ds(edge, 1), ds_] * h_in + st_s[c][:, ds_]
        return carry

    lax.fori_loop(0, nc, carry_states, 0)

    y = y_s[...] + d_ref[...] * xbc_s[:, 0:SSD_D]
    y = y * _silu(zx_ref[:, 0:SSD_D])
    y_ref[...] = _rms(y, nw_ref[...])
    if hout_ref is not None:
        for direction in range(2):
            for k in range(SSD_D // LANES):
                tr_s[k * LANES:(k + 1) * LANES, :] = h_s[direction, :, k * LANES:(k + 1) * LANES].T
            for h in range(nh):
                g = h // hpg
                hout_ref[direction * nh + h] = tr_s[h * hd:(h + 1) * hd,
                                                    g * SSD_STATE:(g + 1) * SSD_STATE]


def _ssd_call(zx, dt, lw, layer, state, seq, state_out=None):
    has_h0 = state is not None
    rows = zx.shape[0]
    nseq = rows // seq
    nc = seq // SSD_CHUNK
    nst = 2 * SSD_HEADS

    def seqspec(w):
        return pl.BlockSpec((seq, w), lambda b: (b, 0))

    def par(shape):
        return _const_spec(shape, layer, single_buffer=False)

    in_specs = [seqspec(1024), seqspec(LANES), par((8, SSD_CONV_DIM)), par((1, SSD_CONV_DIM)),
                par((1, LANES)), par((1, LANES)), par((1, SSD_D)), par((1, SSD_D))]
    args = [zx, dt, lw['conv_w'], lw['conv_b'], lw['dt_bias'], lw['a'], lw['d'], lw['norm']]
    y_shape = jax.ShapeDtypeStruct((rows, SSD_D), F32)
    if has_h0:
        in_specs.append(pl.BlockSpec((None, None, nst, SSD_HEAD_DIM, SSD_STATE),
                                     lambda b: (b, layer, 0, 0, 0)))
        args.append(state)
        aliases = {}
        out_shape = y_shape
        out_specs = seqspec(SSD_D)
    else:
        in_specs.append(pl.BlockSpec(memory_space=pl.ANY))
        args.append(state_out)
        aliases = {8: 1}
        out_shape = [y_shape, jax.ShapeDtypeStruct(state_out.shape, F32)]
        out_specs = [seqspec(SSD_D),
                     pl.BlockSpec((None, None, nst, SSD_HEAD_DIM, SSD_STATE),
                                  lambda b: (b, layer, 0, 0, 0))]
    return pl.pallas_call(
        functools.partial(_ssd_body, seq=seq, has_h0=has_h0),
        out_shape=out_shape,
        grid=(nseq,),
        in_specs=in_specs,
        out_specs=out_specs,
        input_output_aliases=aliases,
        scratch_shapes=[
            pltpu.VMEM((seq, SSD_CONV_DIM), F32),
            pltpu.VMEM((seq, SSD_D), F32),
            pltpu.VMEM((seq, 2 * SSD_D), F32),
            pltpu.VMEM((seq, LANES), F32),
            pltpu.VMEM((nc, SSD_GROUPS * SSD_STATE, 2 * SSD_D), F32),
            pltpu.VMEM((2, SSD_GROUPS * SSD_STATE, SSD_D), F32),
            pltpu.VMEM((SSD_D, SSD_GROUPS * SSD_STATE), F32),
        ],
        compiler_params=_cparams("parallel"),
        name="ssd_mixer",
    )(*args)


def _s5_tables(a_re, a_im, log_dt, b_re, b_im, c_re, c_im):
    t = S5_T
    gh = S5_HGROUPS
    f32 = lambda x: x.astype(F32)
    lam_r, lam_i = f32(a_re), f32(a_im)
    step = jnp.exp(f32(log_dt))[..., None]
    xr, xi = lam_r * step, lam_i * step

    def powers(ks):
        ks = jnp.asarray(ks, F32)[None, None, :, None, None]
        mag = jnp.exp(xr[:, :, None] * ks)
        return mag * jnp.cos(xi[:, :, None] * ks), mag * jnp.sin(xi[:, :, None] * ks)

    pw_r, pw_i = powers(np.arange(t + 1))
    nr, ni = pw_r[:, :, 1] - 1.0, pw_i[:, :, 1]
    den = lam_r * lam_r + lam_i * lam_i
    fr = (nr * lam_r + ni * lam_i) / den
    fi = (ni * lam_r - nr * lam_i) / den
    br, bi = f32(b_re)[:, None], f32(b_im)[:, None]
    bb_r = fr[..., None] * br - fi[..., None] * bi
    bb_i = fr[..., None] * bi + fi[..., None] * br
    cr, ci = f32(c_re), f32(c_im)

    kern = []
    for direction in range(2):
        pr = jnp.swapaxes(pw_r[:, direction, :t], 1, 2)[:, :, :, None, :]
        pi = jnp.swapaxes(pw_i[:, direction, :t], 1, 2)[:, :, :, None, :]
        cpr = cr[:, :, None] * pr - ci[:, :, None] * pi
        cpi = cr[:, :, None] * pi + ci[:, :, None] * pr
        lhs = jnp.concatenate([cpr, -cpi], axis=-1).reshape(DEPTH, S5_GROUPS, t * S5_GROUP_CH, 2 * S5_STATE)
        rhs = jnp.concatenate([bb_r[:, direction], bb_i[:, direction]], axis=2)
        k = jnp.einsum('dgxn,dgnk->dgxk', lhs, rhs, precision=HIGHEST)
        kern.append(jnp.swapaxes(k.reshape(DEPTH, S5_GROUPS, t, S5_GROUP_CH, S5_GROUP_CH), 1, 2))
    kf, kb = kern
    lag = [kb[:, -d] if d < 0 else (kf[:, 0] + kb[:, 0] if d == 0 else kf[:, d])
           for d in range(-(t - 1), t)]
    kt = jnp.stack(lag, axis=1).reshape(DEPTH, S5_LAGS, 2, gh, S5_GROUP_CH, S5_GROUP_CH)
    kt = jnp.transpose(kt, (0, 2, 1, 5, 3, 4)).reshape(DEPTH, 2, S5_LAGS, S5_GROUP_CH, LANES)

    def by_half(x):
        return jnp.transpose(x.reshape(DEPTH, -1, 2, gh, S5_STATE), (0, 2, 1, 3, 4))

    def dir_powers(direction, ks):
        pr, pi = powers(np.asarray(ks))
        return pr[:, direction], pi[:, direction]

    bt_r = jnp.transpose(bb_r.reshape(DEPTH, 2, 2, gh, S5_STATE, S5_GROUP_CH), (0, 1, 2, 5, 3, 4))
    bt_i = jnp.transpose(bb_i.reshape(DEPTH, 2, 2, gh, S5_STATE, S5_GROUP_CH), (0, 1, 2, 5, 3, 4))
    sw = []
    for direction, ks in ((0, [t - 1 - s for s in range(t)]), (1, list(range(t)))):
        pr, pi = dir_powers(direction, ks)
        pr = by_half(pr)[:, :, :, None]
        pi = by_half(pi)[:, :, :, None]
        wr, wi = bt_r[:, direction][:, :, None], bt_i[:, direction][:, :, None]
        sw += [pr * wr - pi * wi, pr * wi + pi * wr]
    sw = jnp.stack(sw, axis=3).reshape(DEPTH, 2, t, 4, S5_GROUP_CH, gh * S5_STATE)
    ct_r = jnp.transpose(cr.reshape(DEPTH, 2, gh, S5_GROUP_CH, S5_STATE), (0, 1, 4, 2, 3))
    ct_i = jnp.transpose(ci.reshape(DEPTH, 2, gh, S5_GROUP_CH, S5_STATE), (0, 1, 4, 2, 3))
    aw = []
    for direction, ks in ((0, list(range(1, t + 1))), (1, [t - k for k in range(t)])):
        pr, pi = dir_powers(direction, ks)
        pr = jnp.swapaxes(by_half(pr), 3, 4)[..., None]
        pi = jnp.swapaxes(by_half(pi), 3, 4)[..., None]
        wr, wi = ct_r[:, :, None], ct_i[:, :, None]
        aw += [wr * pr - wi * pi, -(wr * pi + wi * pr)]
    aw = jnp.stack(aw, axis=2).reshape(DEPTH, 2, 4, t, S5_STATE, LANES)

    apow = jnp.stack([pw_r[:, 0, t], pw_i[:, 0, t], pw_r[:, 1, t], pw_i[:, 1, t]], axis=1)
    seg_r, seg_i = powers(t * np.arange(S5_NC + 1))
    back_r, back_i = dir_powers(1, t * (S5_NC - 1 - np.arange(S5_NC)))
    pseg = jnp.stack([seg_r[:, 0, :S5_NC], seg_i[:, 0, :S5_NC], back_r, back_i], axis=1)
    nkb = S5_CW // LANES
    pseg = jnp.transpose(pseg.reshape(DEPTH, 4, S5_NC, nkb, LANES), (0, 1, 3, 2, 4))
    pseg = pseg.reshape(DEPTH, 4 * nkb, S5_NC, LANES)
    aseg = jnp.stack([seg_r[:, 0, S5_NC], seg_i[:, 0, S5_NC],
                      seg_r[:, 1, S5_NC], seg_i[:, 1, S5_NC]], axis=1)
    return (kt, sw, aw, apow.reshape(DEPTH, 1, S5_SW), pseg, aseg.reshape(DEPTH, 1, S5_SW))


def _s5_expand_operators(kt_ref, sw_ref, aw_ref, toep_s, sop_s, aop_s):
    gh = S5_HGROUPS
    sh_ch = S5_GROUP_CH.bit_length() - 1
    sh_st = S5_STATE.bit_length() - 1

    def group_mask(shape, row_shift, lane_shift):
        r = lax.broadcasted_iota(jnp.int32, shape, 0) >> row_shift
        c = lax.broadcasted_iota(jnp.int32, shape, 1) >> lane_shift
        return r == c

    def blockdiag(x, mask):
        return jnp.where(mask, jnp.concatenate([x] * gh, axis=0), 0.0).astype(BF16)

    m_kk = group_mask((LANES, LANES), sh_ch, sh_ch)
    m_ks = group_mask((LANES, gh * S5_STATE), sh_ch, sh_st)
    m_sk = group_mask((gh * S5_STATE, LANES), sh_st, sh_ch)
    cw = gh * S5_STATE
    for hf in range(2):
        lags = [blockdiag(kt_ref[hf, d], m_kk) for d in range(S5_LAGS)]
        for s in range(S5_T):
            for t in range(S5_T):
                toep_s[hf, s * LANES:(s + 1) * LANES, t * LANES:(t + 1) * LANES] = lags[t - s + S5_T - 1]
            for comp in range(4):
                sop_s[hf, s * LANES:(s + 1) * LANES, comp * cw:(comp + 1) * cw] = blockdiag(
                    sw_ref[hf, s, comp], m_ks)
        for comp in range(4):
            for t in range(S5_T):
                aop_s[hf, comp * cw:(comp + 1) * cw, t * LANES:(t + 1) * LANES] = blockdiag(
                    aw_ref[hf, comp, t], m_sk)


def _s5_body(up_ref, us_ref, kt_ref, sw_ref, aw_ref, at_ref, pseg_ref, aseg_ref, h0_ref,
             y_ref, hout_ref, toep_s, sop_s, aop_s, st_s, hin_s, yacc_s, xh_s, *, nblk_p, nseg_s):
    i = pl.program_id(0)

    @pl.when(i == 0)
    def _():
        _s5_expand_operators(kt_ref, sw_ref, aw_ref, toep_s, sop_s, aop_s)

    cw = S5_CW
    nkb = cw // LANES

    def seg_rows(c, s):
        return pl.ds(c * S5_T + s, S5_HB, stride=S5_T * S5_NC)

    def chunk_rows(c):
        return slice(c * S5_HB, (c + 1) * S5_HB)

    def gather(src_ref):
        for hf in range(2):
            for c in range(0, S5_NC, 2):
                xh_s[hf, c * S5_HB:(c + 2) * S5_HB, :] = jnp.concatenate(
                    [jnp.concatenate([src_ref[hf, seg_rows(cc, s), :] for s in range(S5_T)], axis=1)
                     for cc in (c, c + 1)], axis=0).astype(BF16)

    @pl.when(i < nblk_p)
    def _():
        gather(up_ref)

    @pl.when(i >= nblk_p)
    def _():
        gather(us_ref)

    for hf in range(2):
        xh = xh_s[hf]
        yacc_s[hf] = _dot(xh, toep_s[hf])
        sh = _dot(xh, sop_s[hf])
        for comp in range(4):
            for k in range(nkb // 2):
                col = (comp * (nkb // 2) + k) * LANES
                st_s[comp * nkb + hf * (nkb // 2) + k] = sh[:, col:col + LANES]

    def load(ref, comp, rows):
        return jnp.concatenate([ref[comp * nkb + k, rows, :] for k in range(nkb)], axis=1)

    def store(ref, comp, rows, val):
        for k in range(nkb):
            ref[comp * nkb + k, rows, :] = val[:, k * LANES:(k + 1) * LANES]

    def cmul(ar, ai, hr, hi):
        return ar * hr - ai * hi, ar * hi + ai * hr

    at = at_ref[...]
    ar_f, ai_f, ar_b, ai_b = [at[:, k * cw:(k + 1) * cw] for k in range(4)]

    def step(c, carry):
        hfr, hfi, hbr, hbi = carry
        rf = chunk_rows(c)
        rb = chunk_rows(S5_NC - 1 - c)
        for comp, val, rows in ((0, hfr, rf), (1, hfi, rf), (2, hbr, rb), (3, hbi, rb)):
            store(hin_s, comp, rows, val)
        fr, fi = cmul(ar_f, ai_f, hfr, hfi)
        br, bi = cmul(ar_b, ai_b, hbr, hbi)
        return (fr + load(st_s, 0, rf), fi + load(st_s, 1, rf),
                br + load(st_s, 2, rb), bi + load(st_s, 3, rb))

    zero = jnp.zeros((S5_HB, cw), F32)
    fin = (zero, zero, zero, zero)
    for c in range(S5_NC):
        fin = step(c, fin)
    hout_ref[...] = jnp.concatenate(fin, axis=-1)

    @pl.when(i >= nblk_p)
    def _():
        h0 = h0_ref[...]
        aseg = aseg_ref[...]
        sr_f, si_f, sr_b, si_b = [aseg[:, k * cw:(k + 1) * cw] for k in range(4)]
        ent = [[None] * S5_HB for _ in range(4)]
        for s in range(S5_HB // nseg_s):
            hr, hi = h0[s:s + 1, 0:cw], h0[s:s + 1, cw:2 * cw]
            for j in range(nseg_s):
                v = s * nseg_s + j
                ent[0][v], ent[1][v] = hr, hi
                hr, hi = cmul(sr_f, si_f, hr, hi)
                hr, hi = hr + fin[0][v:v + 1], hi + fin[1][v:v + 1]
            hr, hi = h0[s:s + 1, 2 * cw:3 * cw], h0[s:s + 1, 3 * cw:4 * cw]
            for j in range(nseg_s - 1, -1, -1):
                v = s * nseg_s + j
                ent[2][v], ent[3][v] = hr, hi
                hr, hi = cmul(sr_b, si_b, hr, hi)
                hr, hi = hr + fin[2][v:v + 1], hi + fin[3][v:v + 1]
        ent = [jnp.concatenate(rows, axis=0) for rows in ent]
        for direction in range(2):
            for k in range(nkb):
                kr = (2 * direction) * nkb + k
                ki = (2 * direction + 1) * nkb + k
                er = ent[2 * direction][:, k * LANES:(k + 1) * LANES]
                ei = ent[2 * direction + 1][:, k * LANES:(k + 1) * LANES]
                for c in range(S5_NC):
                    dr, di = cmul(pseg_ref[kr, c:c + 1, :], pseg_ref[ki, c:c + 1, :], er, ei)
                    hin_s[kr, chunk_rows(c), :] += dr
                    hin_s[ki, chunk_rows(c), :] += di

    for hf in range(2):
        hh = jnp.concatenate(
            [hin_s[comp * nkb + hf * (nkb // 2) + k] for comp in range(4) for k in range(nkb // 2)],
            axis=1).astype(BF16)
        y = yacc_s[hf] + _dot(hh, aop_s[hf])
        for c in range(S5_NC):
            for t in range(S5_T):
                y_ref[hf, seg_rows(c, t), :] = y[c * S5_HB:(c + 1) * S5_HB, t * LANES:(t + 1) * LANES]


def _s5_call(u_p, u_s, tables, layer, h0, seq_p, seq_s):
    kt, sw, aw, apow, pseg, aseg = tables
    seg_tok = S5_T * S5_NC
    assert seq_p == seg_tok and seq_s % seg_tok == 0 and S5_HB % (seq_s // seg_tok) == 0
    blk_tok = S5_ROWS * S5_T
    nblk_p = u_p.shape[1] // blk_tok
    nblk = nblk_p + u_s.shape[1] // blk_tok
    uh = S5_D // LANES

    def par(a):
        return _const_spec(a.shape[1:], layer)

    nlb = S5_SW // LANES
    return pl.pallas_call(
        functools.partial(_s5_body, nblk_p=nblk_p, nseg_s=seq_s // seg_tok),
        out_shape=[jax.ShapeDtypeStruct((uh, nblk * blk_tok, LANES), F32),
                   jax.ShapeDtypeStruct((nblk, S5_HB, S5_SW), F32)],
        grid=(nblk,),
        in_specs=[pl.BlockSpec((uh, blk_tok, LANES), lambda i: (0, jnp.minimum(i, nblk_p - 1), 0)),
                  pl.BlockSpec((uh, blk_tok, LANES), lambda i: (0, jnp.maximum(i - nblk_p, 0), 0)),
                  par(kt), par(sw), par(aw), par(apow), par(pseg), par(aseg),
                  pl.BlockSpec((None, None, S5_HB, S5_SW), lambda i: (layer, i, 0, 0))],
        out_specs=[pl.BlockSpec((uh, blk_tok, LANES), lambda i: (0, i, 0)),
                   pl.BlockSpec((None, S5_HB, S5_SW), lambda i: (i, 0, 0))],
        scratch_shapes=[pltpu.VMEM((2, S5_KW, S5_KW), BF16), pltpu.VMEM((2, S5_KW, S5_SH), BF16),
                        pltpu.VMEM((2, S5_SH, S5_KW), BF16),
                        pltpu.VMEM((nlb, S5_ROWS, LANES), F32), pltpu.VMEM((nlb, S5_ROWS, LANES), F32),
                        pltpu.VMEM((2, S5_ROWS, S5_KW), F32), pltpu.VMEM((2, S5_ROWS, S5_KW), BF16)],
        compiler_params=_cparams("arbitrary"),
        name="s5_mixer",
    )(u_p, u_s, kt, sw, aw, apow, pseg, aseg, h0)


def _gelu_tanh(x):
    return 0.5 * x * (1.0 + jnp.tanh(math.sqrt(2.0 / math.pi) * (x + 0.044715 * (x * x * x))))


def _out_body(*refs, final):
    (x_ref, yssd_ref, ynat_ref, u_ref, y5_ref, g1_ref, m2_ref, d5_ref,
     wglu_ref, bglu_ref, wout_ref, nm_ref, w1_ref, w2_ref) = refs[:14]
    if final:
        nf_ref, o_ref = refs[14:]
    else:
        (o_ref,) = refs[14:]
    uh = S5_D // LANES
    for sb in range(ROW_TILE // TOK_TILE):
        rows = slice(sb * TOK_TILE, (sb + 1) * TOK_TILE)
        y5_in = jnp.concatenate([y5_ref[hf, rows, :] for hf in range(uh)], axis=1)
        u = jnp.concatenate([u_ref[hf, rows, :] for hf in range(uh)], axis=1)
        g = _gelu_tanh(y5_in + d5_ref[...] * u)
        y5 = g * jax.nn.sigmoid(_dot(g.astype(BF16), wglu_ref[...]) + bglu_ref[...])
        mix = (_dot(yssd_ref[rows, :].astype(BF16), wout_ref[0:SSD_D, :])
               + _dot(y5.astype(BF16), wout_ref[SSD_D:SSD_D + S5_D, :])
               + _dot(ynat_ref[rows, :].astype(BF16), wout_ref[SSD_D + S5_D:, :]))
        x = x_ref[rows, :] + g1_ref[...] * mix
        h2 = _rms(x, nm_ref[...]) * (1.0 + m2_ref[:, D_MODEL:2 * D_MODEL]) + m2_ref[:, 0:D_MODEL]
        f = jnp.maximum(_dot(h2.astype(BF16), w1_ref[...]), 0.0)
        f = (f * f).astype(BF16)
        x = x + m2_ref[:, 2 * D_MODEL:3 * D_MODEL] * _dot(f, w2_ref[...])
        if final:
            x = _rms(x, nf_ref[...])
        o_ref[rows, :] = x


def _out_call(x, y_ssd, y_nat, u, y5, y5_tile_off, cond_idx, mods4, lw, norm_f, layer, final):
    rows = x.shape[0]
    uh = S5_D // LANES

    def tile(w):
        return pl.BlockSpec((ROW_TILE, w), lambda i: (i, 0))

    def par(shape):
        return _const_spec(shape, layer)

    in_specs = [
        tile(D_MODEL), tile(SSD_D), tile(NAT_D),
        pl.BlockSpec((uh, ROW_TILE, LANES), lambda i: (0, i, 0)),
        pl.BlockSpec((uh, ROW_TILE, LANES), lambda i: (0, i + y5_tile_off, 0)),
        pl.BlockSpec((None, None, 1, D_MODEL), lambda i: (layer, cond_idx(i), 0, 2)),
        pl.BlockSpec((None, None, 1, 3 * D_MODEL), lambda i: (layer, cond_idx(i), 0, 1)),
        par((1, S5_D)), par((S5_D, S5_D)), par((1, S5_D)),
        par((D_MODEL, D_MODEL)), par((1, D_MODEL)),
        par((D_MODEL, D_FF)), par((D_FF, D_MODEL))]
    args = [x, y_ssd, y_nat, u, y5, mods4, mods4, lw['s5_d'], lw['w_glu'], lw['b_glu'],
            lw['w_out'], lw['norm_mlp'], lw['w_ff1'], lw['w_ff2']]
    if final:
        in_specs.append(pl.BlockSpec((1, D_MODEL), lambda i: (0, 0)))
        args.append(norm_f.reshape(1, D_MODEL))
    return pl.pallas_call(
        functools.partial(_out_body, final=final),
        out_shape=jax.ShapeDtypeStruct((rows, D_MODEL), F32),
        grid=(rows // ROW_TILE,),
        in_specs=in_specs,
        out_specs=tile(D_MODEL),
        compiler_params=_cparams("parallel"),
        name="out_mlp",
    )(*args)


def _lane_pad(x):
    return jnp.pad(x, [(0, 0)] * (x.ndim - 1) + [(0, LANES - x.shape[-1])])


def kernel(x_prompt, x_sample, cache_nat_k, cache_nat_v, state_ssd, state_s5_re, state_s5_im,
           c, c_ctx, w_mod, b_mod, norm_mix, norm_mlp, w_in, ssd_conv_w, ssd_conv_b,
           ssd_dt_bias, ssd_a_log, ssd_d, ssd_norm, s5_a_re, s5_a_im, s5_log_dt,
           s5_b_re, s5_b_im, s5_c_re, s5_c_im, s5_d, s5_w_glu, s5_b_glu, nat_rpb,
           w_out, w_ff1, w_ff2, norm_f):
    bp, seq_p, _ = x_prompt.shape
    bs, seq_s, _ = x_sample.shape
    assert seq_p == TOK_TILE and seq_s == 4 * TOK_TILE
    rows_p = bp * seq_p
    rows_s = bs * seq_s

    ncp = -(-(1 + bs) // 8) * 8
    cond = jnp.concatenate([c_ctx[None, :], c, jnp.zeros((ncp - 1 - bs, D_MODEL), F32)], axis=0)
    mods = _mods_call(cond, w_mod, b_mod)
    mods4 = mods.reshape(DEPTH, ncp, 1, N_MOD * D_MODEL)

    assert w_in.shape[-1] == IN_COLS
    w_in_p = jnp.pad(w_in.astype(BF16), [(0, 0), (0, 0), (0, IN_COLS_PAD - IN_COLS)])
    ssd_w = {
        'conv_w': jnp.pad(ssd_conv_w.astype(F32), [(0, 0), (0, 8 - SSD_CONV), (0, 0)]),
        'conv_b': ssd_conv_b.astype(F32).reshape(DEPTH, 1, SSD_CONV_DIM),
        'dt_bias': _lane_pad(ssd_dt_bias.astype(F32).reshape(DEPTH, 1, 2 * SSD_HEADS)),
        'a': _lane_pad(-jnp.exp(ssd_a_log.astype(F32)).reshape(DEPTH, 1, 2 * SSD_HEADS)),
        'd': jnp.repeat(ssd_d.astype(F32), SSD_HEAD_DIM, axis=-1).reshape(DEPTH, 1, SSD_D),
        'norm': ssd_norm.astype(F32).reshape(DEPTH, 1, SSD_D),
    }
    out_w = {
        's5_d': s5_d.astype(F32).reshape(DEPTH, 1, S5_D),
        'w_glu': s5_w_glu.astype(BF16),
        'b_glu': s5_b_glu.astype(F32).reshape(DEPTH, 1, S5_D),
        'w_out': w_out.astype(BF16),
        'norm_mlp': norm_mlp.astype(F32).reshape(DEPTH, 1, D_MODEL),
        'w_ff1': w_ff1.astype(BF16),
        'w_ff2': w_ff2.astype(BF16),
    }
    s5_tabs = _s5_tables(s5_a_re, s5_a_im, s5_log_dt, s5_b_re, s5_b_im, s5_c_re, s5_c_im)
    nat_bias = _nat_pair_tiles(nat_rpb)

    cache_k = cache_nat_k.reshape(bs, DEPTH, -1, NAT_D)
    cache_v = cache_nat_v.reshape(bs, DEPTH, -1, NAT_D)
    st_ssd = state_ssd.reshape(bs, DEPTH, 2 * SSD_HEADS, SSD_HEAD_DIM, SSD_STATE)
    s5_tok = S5_ROWS * S5_T
    assert rows_p % s5_tok == 0 and rows_s % s5_tok == 0 and s5_tok % seq_s == 0
    nseq_s = s5_tok // seq_s
    st5 = jnp.stack([state_s5_re[:, :, 0], state_s5_im[:, :, 0],
                     state_s5_re[:, :, 1], state_s5_im[:, :, 1]], axis=2)
    st5 = jnp.transpose(st5.astype(F32).reshape(bs // nseq_s, nseq_s, DEPTH, S5_SW), (2, 0, 1, 3))
    st5 = jnp.pad(st5, [(0, 0), (rows_p // s5_tok, 0), (0, S5_HB - nseq_s), (0, 0)])

    assert rows_p % ROW_TILE == 0 and seq_s % ROW_TILE == 0
    tiles_per_seq = seq_s // ROW_TILE
    cond_p = lambda i: 0
    cond_s = lambda i: 1 + i // tiles_per_seq
    x_p = x_prompt.reshape(rows_p, D_MODEL)
    x_s = x_sample.reshape(rows_s, D_MODEL)
    k_all = jnp.zeros((bp, DEPTH, seq_p, NAT_D), F32)
    v_all = jnp.zeros((bp, DEPTH, seq_p, NAT_D), F32)
    ssd_all = jnp.zeros((bp, DEPTH, 2 * SSD_HEADS, SSD_HEAD_DIM, SSD_STATE), F32)
    new_s5 = []
    for l in range(DEPTH):
        final = l == DEPTH - 1
        zx_p, dt_p, u_p, q_p, k_all, v_all = _inproj_call(x_p, cond_p, mods4, norm_mix, w_in_p, l,
                                                         kv_cache=(k_all, v_all))
        zx_s, dt_s, u_s, q_s, k_s, v_s = _inproj_call(x_s, cond_s, mods4, norm_mix, w_in_p, l)

        y_nat_p = _ctx_attn_call(q_p, k_all, v_all, l)
        y_nat_s = _nat_attn_call(q_s, k_s, v_s, cache_k, cache_v, nat_bias, l, bs, seq_s)

        y_ssd_p, ssd_all = _ssd_call(zx_p, dt_p, ssd_w, l, None, seq_p, state_out=ssd_all)
        y_ssd_s = _ssd_call(zx_s, dt_s, ssd_w, l, st_ssd, seq_s)

        y5, s5_l = _s5_call(u_p, u_s, s5_tabs, l, st5, seq_p, seq_s)
        new_s5.append(s5_l[:rows_p // s5_tok])

        x_p = _out_call(x_p, y_ssd_p, y_nat_p, u_p, y5, 0, cond_p, mods4, out_w, norm_f, l, final)
        x_s = _out_call(x_s, y_ssd_s, y_nat_s, u_s, y5, rows_p // ROW_TILE, cond_s,
                        mods4, out_w, norm_f, l, final)

    y_prompt = x_p.reshape(bp, seq_p, D_MODEL)
    y_sample = x_s.reshape(bs, seq_s, D_MODEL)
    out_k = k_all.reshape(bp, DEPTH, seq_p, NAT_HEADS, NAT_HEAD_DIM)
    out_v = v_all.reshape(bp, DEPTH, seq_p, NAT_HEADS, NAT_HEAD_DIM)
    out_ssd = ssd_all.reshape(bp, DEPTH, 2, SSD_HEADS, SSD_HEAD_DIM, SSD_STATE)
    s5 = jnp.stack(new_s5, axis=0)[:, :, :s5_tok // seq_p]
    s5 = s5.reshape(DEPTH, bp, 4, S5_GROUPS, S5_STATE)
    s5 = jnp.transpose(s5, (1, 0, 2, 3, 4))
    out_re = s5[:, :, 0::2]
    out_im = s5[:, :, 1::2]
    return y_prompt, y_sample, out_k, out_v, out_ssd, out_re, out_im
```
